```python
import math
import jax, jax.numpy as jnp
from jax import lax
import numpy as np

D_MODEL = 1024
BATCH = 4
SEQ = 8192
DEPTH = 4

N_BRANCH = 4
BRANCH_W = 512
NORM_EPS = 1e-6
LRU_HEADS = 8
LRU_HD = BRANCH_W // LRU_HEADS
CONV_W = 4
LRU_C = 8.0
S5_GROUP = 16
S5_GROUPS = BRANCH_W // S5_GROUP
S5_STATE = 64
RET_HEADS = 4
RET_HD = BRANCH_W // RET_HEADS
RET_CHUNK = 128
RET_ROPE_BASE = 10000.0
NSA_HEADS = 8
NSA_KV_HEADS = 2
NSA_HD = BRANCH_W // NSA_HEADS
NSA_GQA = NSA_HEADS // NSA_KV_HEADS
CMP_LEN = 32
CMP_STRIDE = 16
CMP_HIDDEN = 256
SLC_BLOCK = 64
SLC_TOPK = 16
WIN = 512
Q_BLOCK = 128
ROPE_THETA = 500000.0
ROPE_DIM = NSA_HD // 4
FORCE_SCORE = 1e4
NEG = -1e30
NSA_KV_W = 6 * NSA_KV_HEADS * NSA_HD
IN_SIZES = (BRANCH_W, BRANCH_W, 3 * BRANCH_W, NSA_HEADS * NSA_HD, NSA_KV_W, 3 * NSA_HEADS,
            N_BRANCH * BRANCH_W, N_BRANCH * D_MODEL)
D_IN = sum(IN_SIZES)

kernel_name = 'hybrid_lru_s5_retention_nsa_block'


def rms_norm(x, g):
    xf = x.astype(jnp.float32)
    y = xf * lax.rsqrt(jnp.mean(xf * xf, axis=-1, keepdims=True) + NORM_EPS)
    return (y * g.astype(jnp.float32)).astype(x.dtype)


def split_columns(p, sizes):
    outs, off = [], 0
    for n in sizes:
        outs.append(p[..., off:off + n])
        off += n
    return outs


def rope_half(x, pos, base, rot_dim):
    half = rot_dim // 2
    inv = base ** (-jnp.arange(half, dtype=jnp.float32) * 2.0 / rot_dim)
    ang = pos.astype(jnp.float32)[:, None] * inv[None, :]
    cos = jnp.cos(ang)[:, None, :]
    sin = jnp.sin(ang)[:, None, :]
    xf = x.astype(jnp.float32)
    x1 = xf[..., :half]
    x2 = xf[..., half:rot_dim]
    out = jnp.concatenate([x1 * cos - x2 * sin, x2 * cos + x1 * sin, xf[..., rot_dim:]], axis=-1)
    return out.astype(x.dtype)


def masked_softmax(s, mask):
    s = jnp.where(mask, s, NEG)
    p = jax.nn.softmax(s, axis=-1)
    return jnp.where(mask, p, 0.0)


def causal_depthwise_conv(x, w, b):
    c = x.shape[-1]
    y = lax.conv_general_dilated(x, w[:, None, :].astype(x.dtype), window_strides=(1,),
                                 padding=((CONV_W - 1, 0),), dimension_numbers=('NWC', 'WIO', 'NWC'),
                                 feature_group_count=c)
    return y + b


def rg_lru_mixer(u, conv_w, conv_b, wa, ba, wx, bx, lam):
    bsz, s, w = u.shape
    xc = causal_depthwise_conv(u, conv_w, conv_b)
    xh = xc.reshape(bsz, s, LRU_HEADS, LRU_HD)
    r = jax.nn.sigmoid((jnp.einsum('bshi,hij->bshj', xh, wa).reshape(bsz, s, w) + ba).astype(jnp.float32))
    i = jax.nn.sigmoid((jnp.einsum('bshi,hij->bshj', xh, wx).reshape(bsz, s, w) + bx).astype(jnp.float32))
    log_a = -LRU_C * r * jax.nn.softplus(-lam.astype(jnp.float32))
    a = jnp.exp(log_a)
    mult = jnp.sqrt(-jnp.expm1(2.0 * log_a))
    bterm = mult * i * xc.astype(jnp.float32)

    def op(l, r_):
        a1, b1 = l
        a2, b2 = r_
        return a1 * a2, a2 * b1 + b2

    _, h = lax.associative_scan(op, (a, bterm), axis=1)
    return h.astype(u.dtype)


def s5_mixer(u, lam_re, lam_im, b_re, b_im, c_re, c_im, log_dt, d_skip, w_glu, b_glu):
    bsz, s, w = u.shape
    f32 = jnp.float32
    uf = u.astype(f32).reshape(bsz, s, S5_GROUPS, S5_GROUP)
    dt = jnp.exp(log_dt.astype(f32))[:, None]
    lr = jnp.minimum(lam_re.astype(f32), -1e-4)
    li = lam_im.astype(f32)
    mag = jnp.exp(lr * dt)
    ab_re = mag * jnp.cos(li * dt)
    ab_im = mag * jnp.sin(li * dt)
    den = lr * lr + li * li
    nr = ab_re - 1.0
    ni = ab_im
    f_re = (nr * lr + ni * li) / den
    f_im = (ni * lr - nr * li) / den
    br = b_re.astype(f32)
    bi = b_im.astype(f32)
    bb_re = f_re[..., None] * br - f_im[..., None] * bi
    bb_im = f_re[..., None] * bi + f_im[..., None] * br
    x_re = jnp.einsum('bsgc,gpc->bsgp', uf, bb_re)
    x_im = jnp.einsum('bsgc,gpc->bsgp', uf, bb_im)
    a_re = jnp.broadcast_to(ab_re, x_re.shape)
    a_im = jnp.broadcast_to(ab_im, x_im.shape)

    def op(l, r_):
        ar1, ai1, br1, bi1 = l
        ar2, ai2, br2, bi2 = r_
        return (ar2 * ar1 - ai2 * ai1, ar2 * ai1 + ai2 * ar1,
                ar2 * br1 - ai2 * bi1 + br2, ar2 * bi1 + ai2 * br1 + bi2)

    _, _, h_re, h_im = lax.associative_scan(op, (a_re, a_im, x_re, x_im), axis=1)
    y = (jnp.einsum('bsgp,gcp->bsgc', h_re, c_re.astype(f32))
         - jnp.einsum('bsgp,gcp->bsgc', h_im, c_im.astype(f32)))
    y = y.reshape(bsz, s, w) + d_skip.astype(f32) * u.astype(f32)
    z = jax.nn.gelu(y).astype(u.dtype)
    g = jnp.einsum('bsw,wv->bsv', z, w_glu) + b_glu
    return (g[..., :w] * jax.nn.sigmoid(g[..., w:])).astype(u.dtype)


def retention_mixer(q, k, v):
    bsz, s, nh, dh = q.shape
    f32 = jnp.float32
    pos = jnp.arange(s)
    q = rope_half(q, pos, RET_ROPE_BASE, dh).astype(f32)
    k = rope_half(k, pos, RET_ROPE_BASE, dh).astype(f32) * (dh ** -0.5)
    v = v.astype(f32)
    gamma = 1.0 - jnp.exp(jnp.linspace(math.log(1.0 / 32.0), math.log(1.0 / 512.0), nh, dtype=f32))
    log_g = jnp.log(gamma)
    c = RET_CHUNK
    nc = s // c
    n = jnp.arange(c, dtype=f32)
    diff = n[:, None] - n[None, :]
    dmask = jnp.where(diff[None] >= 0, jnp.exp(jnp.maximum(diff, 0.0)[None] * log_g[:, None, None]), 0.0)
    xi = jnp.exp((n[None, :] + 1.0) * log_g[:, None])
    zeta = jnp.exp((c - 1.0 - n)[None, :] * log_g[:, None])
    chunk_decay = jnp.exp(c * log_g)
    qc = q.reshape(bsz, nc, c, nh, dh)
    kc = k.reshape(bsz, nc, c, nh, dh)
    vc = v.reshape(bsz, nc, c, nh, dh)
    scores = jnp.einsum('bcihd,bcjhd->bchij', qc, kc) * dmask[None, None]
    inner = jnp.einsum('bchij,bcjhe->bcihe', scores, vc)
    kv = jnp.einsum('bcjhd,hj,bcjhe->bchde', kc, zeta, vc)

    def step(r_state, kv_c):
        return chunk_decay[None, :, None, None] * r_state + kv_c, r_state

    _, r_prev = lax.scan(step, jnp.zeros((bsz, nh, dh, dh), f32), jnp.moveaxis(kv, 1, 0))
    r_prev = jnp.moveaxis(r_prev, 0, 1)
    cross = jnp.einsum('bcihd,hi,bchde->bcihe', qc, xi, r_prev)
    o = (inner + cross).reshape(bsz, s, nh, dh)
    mu = jnp.mean(o, axis=-1, keepdims=True)
    var = jnp.mean(jnp.square(o - mu), axis=-1, keepdims=True)
    o = (o - mu) * lax.rsqrt(var + 1e-5)
    return o.reshape(bsz, s, nh * dh)


def compress_blocks(kraw, pos_emb, w1, w2):
    bsz, s, hkv, dh = kraw.shape
    ncmp = (s - CMP_LEN) // CMP_STRIDE + 1
    idx = jnp.arange(ncmp)[:, None] * CMP_STRIDE + jnp.arange(CMP_LEN)[None, :]
    blocks = kraw[:, idx] + pos_emb[None, None, :, None, :]
    flat = jnp.moveaxis(blocks, 3, 2).reshape(bsz, ncmp, hkv, CMP_LEN * dh)
    hid = jax.nn.gelu(jnp.einsum('bnhf,fk->bnhk', flat, w1))
    return jnp.einsum('bnhk,kd->bnhd', hid, w2)


def nsa_mixer(q, k_cr, v_cr, k_slc, v_slc, k_win, v_win, bgate, pos_k, pos_v, ck_w1, ck_w2, cv_w1, cv_w2):
    bsz, s, hq, dh = q.shape
    hkv = NSA_KV_HEADS
    f32 = jnp.float32
    pos = jnp.arange(s)
    q = rope_half(q, pos, ROPE_THETA, ROPE_DIM)
    k_slc = rope_half(k_slc, pos, ROPE_THETA, ROPE_DIM)
    k_win = rope_half(k_win, pos, ROPE_THETA, ROPE_DIM)
    k_cmp = compress_blocks(k_cr, pos_k, ck_w1, ck_w2)
    v_cmp = compress_blocks(v_cr, pos_v, cv_w1, cv_w2)
    ncmp = k_cmp.shape[1]
    cmp_end = jnp.arange(ncmp) * CMP_STRIDE + CMP_LEN - 1
    k_cmp = rope_half(k_cmp, cmp_end, ROPE_THETA, ROPE_DIM)
    nsel = s // SLC_BLOCK
    topk = min(SLC_TOPK, nsel)
    cmp_tok_blk = (jnp.arange(ncmp)[:, None] * CMP_STRIDE + jnp.arange(CMP_LEN)[None, :]) // SLC_BLOCK
    overlap = jnp.mean((cmp_tok_blk[..., None] == jnp.arange(nsel)).astype(f32), axis=1)
    k_blk = jnp.transpose(k_slc.reshape(bsz, nsel, SLC_BLOCK, hkv, dh), (0, 3, 1, 2, 4))
    v_blk = jnp.transpose(v_slc.reshape(bsz, nsel, SLC_BLOCK, hkv, dh), (0, 3, 1, 2, 4))
    k_win_p = jnp.pad(k_win, ((0, 0), (WIN, 0), (0, 0), (0, 0)))
    v_win_p = jnp.pad(v_win, ((0, 0), (WIN, 0), (0, 0), (0, 0)))
    nqb = s // Q_BLOCK
    qg = q.reshape(bsz, nqb, Q_BLOCK, hkv, NSA_GQA, dh)
    gates = jax.nn.sigmoid(bgate.astype(f32)).reshape(bsz, nqb, Q_BLOCK, 3, hkv, NSA_GQA)
    scale = dh ** -0.5
    b_ix = jnp.arange(bsz)[:, None, None, None]
    h_ix = jnp.arange(hkv)[None, :, None, None]
    blk_id = jnp.arange(nsel)
    wlen = WIN + Q_BLOCK

    def block_fn(args):
        qb, gb, blk = args
        t = blk * Q_BLOCK + jnp.arange(Q_BLOCK)
        s_c = jnp.einsum('bqhgd,bnhd->bhgqn', qb, k_cmp).astype(f32) * scale
        p_c = masked_softmax(s_c, cmp_end[None, :] <= t[:, None])
        o_c = jnp.einsum('bhgqn,bnhd->bqhgd', p_c.astype(v_cmp.dtype), v_cmp)
        imp = jnp.einsum('bhgqn,nj->bhqj', p_c, overlap)
        cur = t // SLC_BLOCK
        valid = blk_id[None, :] <= cur[:, None]
        forced = (blk_id[None, :] == 0) | (blk_id[None, :] == cur[:, None]) | (blk_id[None, :] == cur[:, None] - 1)
        score = jnp.where(valid, jnp.where(forced, FORCE_SCORE, imp), -1.0)
        top_val, top_idx = lax.top_k(score, topk)
        sel_ok = top_val >= 0.0
        kg = k_blk[b_ix, h_ix, top_idx].reshape(bsz, hkv, Q_BLOCK, topk * SLC_BLOCK, dh)
        vg = v_blk[b_ix, h_ix, top_idx].reshape(bsz, hkv, Q_BLOCK, topk * SLC_BLOCK, dh)
        kpos = (top_idx[..., None] * SLC_BLOCK + jnp.arange(SLC_BLOCK)).reshape(bsz, hkv, Q_BLOCK, topk * SLC_BLOCK)
        m_s = jnp.repeat(sel_ok, SLC_BLOCK, axis=-1) & (kpos <= t[None, None, :, None])
        s_s = jnp.einsum('bqhgd,bhqkd->bhgqk', qb, kg).astype(f32) * scale
        p_s = masked_softmax(s_s, m_s[:, :, None])
        o_s = jnp.einsum('bhgqk,bhqkd->bqhgd', p_s.astype(vg.dtype), vg)
        start = blk * Q_BLOCK
        kw = lax.dynamic_slice_in_dim(k_win_p, start, wlen, axis=1)
        vw = lax.dynamic_slice_in_dim(v_win_p, start, wlen, axis=1)
        wpos = start - WIN + jnp.arange(wlen)
        m_w = (wpos[None, :] <= t[:, None]) & (wpos[None, :] > t[:, None] - WIN) & (wpos[None, :] >= 0)
        s_w = jnp.einsum('bqhgd,bkhd->bhgqk', qb, kw).astype(f32) * scale
        p_w = masked_softmax(s_w, m_w)
        o_w = jnp.einsum('bhgqk,bkhd->bqhgd', p_w.astype(vw.dtype), vw)
        out = (gb[:, :, 0][..., None] * o_c + gb[:, :, 1][..., None] * o_s + gb[:, :, 2][..., None] * o_w)
        return out.astype(q.dtype)

    outs = lax.map(block_fn, (jnp.moveaxis(qg, 1, 0), jnp.moveaxis(gates, 1, 0), jnp.arange(nqb)))
    return jnp.moveaxis(outs, 0, 1).reshape(bsz, s, hq * dh)


def hybrid_layer(x, norm_g, w_in, lru_conv_w, lru_conv_b, lru_wa, lru_ba, lru_wx, lru_bx, lru_lambda,
                 s5_lambda_re, s5_lambda_im, s5_b_re, s5_b_im, s5_c_re, s5_c_im, s5_log_dt, s5_d,
                 s5_w_glu, s5_b_glu, nsa_pos_k, nsa_pos_v, nsa_ck_w1, nsa_ck_w2, nsa_cv_w1, nsa_cv_w2,
                 w_branch, w_out):
    bsz, s, d = x.shape
    h = rms_norm(x, norm_g)
    proj = jnp.einsum('bsd,de->bse', h, w_in)
    u_lru, u_s5, qkv_ret, q_nsa, kv_nsa, bg_nsa, gate_paths, merge = split_columns(proj, IN_SIZES)
    y_lru = rg_lru_mixer(u_lru, lru_conv_w, lru_conv_b, lru_wa, lru_ba, lru_wx, lru_bx, lru_lambda)
    y_s5 = s5_mixer(u_s5, s5_lambda_re, s5_lambda_im, s5_b_re, s5_b_im, s5_c_re, s5_c_im,
                    s5_log_dt, s5_d, s5_w_glu, s5_b_glu)
    qkv = qkv_ret.reshape(bsz, s, 3, RET_HEADS, RET_HD)
    y_ret = retention_mixer(qkv[:, :, 0], qkv[:, :, 1], qkv[:, :, 2]).astype(x.dtype)
    kvs = kv_nsa.reshape(bsz, s, 6, NSA_KV_HEADS, NSA_HD)
    y_nsa = nsa_mixer(q_nsa.reshape(bsz, s, NSA_HEADS, NSA_HD), kvs[:, :, 0], kvs[:, :, 1], kvs[:, :, 2],
                      kvs[:, :, 3], kvs[:, :, 4], kvs[:, :, 5], bg_nsa, nsa_pos_k, nsa_pos_v,
                      nsa_ck_w1, nsa_ck_w2, nsa_cv_w1, nsa_cv_w2)
    ys = jnp.stack([y_lru, y_s5, y_ret, y_nsa.astype(x.dtype)], axis=2)
    ys = ys * jax.nn.silu(gate_paths.reshape(bsz, s, N_BRANCH, BRANCH_W))
    branch = jnp.einsum('bsnw,nwd->bsnd', ys, w_branch)
    merged = jnp.sum(jax.nn.sigmoid(merge.reshape(bsz, s, N_BRANCH, d)) * branch, axis=2)
    return x + jnp.einsum('bsd,de->bse', merged, w_out)


def setup_inputs(seed: int = 0) -> dict:
    key = jax.random.key(seed)
    ks = iter(jax.random.split(key, 40))
    f32 = jnp.float32

    def nrm(shape, scale):
        return jax.random.normal(next(ks), shape, f32) * scale

    W = BRANCH_W
    x = nrm((BATCH, SEQ, D_MODEL), 1.0)
    norm_g = 1.0 + nrm((DEPTH, D_MODEL), 0.01)
    w_in = nrm((DEPTH, D_MODEL, D_IN), D_MODEL ** -0.5)
    lru_conv_w = nrm((DEPTH, CONV_W, W), CONV_W ** -0.5)
    lru_conv_b = nrm((DEPTH, W), 0.01)
    lru_wa = nrm((DEPTH, LRU_HEADS, LRU_HD, LRU_HD), LRU_HD ** -0.5)
    lru_ba = nrm((DEPTH, W), 0.01)
    lru_wx = nrm((DEPTH, LRU_HEADS, LRU_HD, LRU_HD), LRU_HD ** -0.5)
    lru_bx = nrm((DEPTH, W), 0.01)
    a_c = jax.random.uniform(next(ks), (DEPTH, W), f32, minval=0.9, maxval=0.999)
    a0 = a_c ** (1.0 / LRU_C)
    lru_lambda = jnp.log(a0) - jnp.log1p(-a0)
    s5_lambda_re = -0.5 + nrm((DEPTH, S5_GROUPS, S5_STATE), 0.01)
    s5_lambda_im = (jnp.pi * jnp.arange(S5_STATE, dtype=f32))[None, None, :] + nrm((DEPTH, S5_GROUPS, S5_STATE), 0.01)
    s5_b_re = nrm((DEPTH, S5_GROUPS, S5_STATE, S5_GROUP), (2.0 * S5_GROUP) ** -0.5)
    s5_b_im = nrm((DEPTH, S5_GROUPS, S5_STATE, S5_GROUP), (2.0 * S5_GROUP) ** -0.5)
    s5_c_re = nrm((DEPTH, S5_GROUPS, S5_GROUP, S5_STATE), (2.0 * S5_STATE) ** -0.5)
    s5_c_im = nrm((DEPTH, S5_GROUPS, S5_GROUP, S5_STATE), (2.0 * S5_STATE) ** -0.5)
    s5_log_dt = jax.random.uniform(next(ks), (DEPTH, S5_GROUPS), f32, minval=math.log(0.001), maxval=math.log(0.1))
    s5_d = nrm((DEPTH, W), 1.0)
    s5_w_glu = nrm((DEPTH, W, 2 * W), W ** -0.5)
    s5_b_glu = nrm((DEPTH, 2 * W), 0.01)
    nsa_pos_k = nrm((DEPTH, CMP_LEN, NSA_HD), 0.02)
    nsa_pos_v = nrm((DEPTH, CMP_LEN, NSA_HD), 0.02)
    nsa_ck_w1 = nrm((DEPTH, CMP_LEN * NSA_HD, CMP_HIDDEN), (CMP_LEN * NSA_HD) ** -0.5)
    nsa_ck_w2 = nrm((DEPTH, CMP_HIDDEN, NSA_HD), CMP_HIDDEN ** -0.5)
    nsa_cv_w1 = nrm((DEPTH, CMP_LEN * NSA_HD, CMP_HIDDEN), (CMP_LEN * NSA_HD) ** -0.5)
    nsa_cv_w2 = nrm((DEPTH, CMP_HIDDEN, NSA_HD), CMP_HIDDEN ** -0.5)
    w_branch = nrm((DEPTH, N_BRANCH, W, D_MODEL), W ** -0.5)
    w_out = nrm((DEPTH, D_MODEL, D_MODEL), D_MODEL ** -0.5)
    final_norm_g = 1.0 + nrm((D_MODEL,), 0.01)
    return {'x': x, 'norm_g': norm_g, 'w_in': w_in, 'lru_conv_w': lru_conv_w, 'lru_conv_b': lru_conv_b,
            'lru_wa': lru_wa, 'lru_ba': lru_ba, 'lru_wx': lru_wx, 'lru_bx': lru_bx, 'lru_lambda': lru_lambda,
            's5_lambda_re': s5_lambda_re, 's5_lambda_im': s5_lambda_im, 's5_b_re': s5_b_re, 's5_b_im': s5_b_im,
            's5_c_re': s5_c_re, 's5_c_im': s5_c_im, 's5_log_dt': s5_log_dt, 's5_d': s5_d,
            's5_w_glu': s5_w_glu, 's5_b_glu': s5_b_glu, 'nsa_pos_k': nsa_pos_k, 'nsa_pos_v': nsa_pos_v,
            'nsa_ck_w1': nsa_ck_w1, 'nsa_ck_w2': nsa_ck_w2, 'nsa_cv_w1': nsa_cv_w1, 'nsa_cv_w2': nsa_cv_w2,
            'w_branch': w_branch, 'w_out': w_out, 'final_norm_g': final_norm_g}


def reference(x, norm_g, w_in, lru_conv_w, lru_conv_b, lru_wa, lru_ba, lru_wx, lru_bx, lru_lambda,
              s5_lambda_re, s5_lambda_im, s5_b_re, s5_b_im, s5_c_re, s5_c_im, s5_log_dt, s5_d,
              s5_w_glu, s5_b_glu, nsa_pos_k, nsa_pos_v, nsa_ck_w1, nsa_ck_w2, nsa_cv_w1, nsa_cv_w2,
              w_branch, w_out, final_norm_g):
    for l in range(DEPTH):
        x = hybrid_layer(x, norm_g[l], w_in[l], lru_conv_w[l], lru_conv_b[l], lru_wa[l], lru_ba[l],
                         lru_wx[l], lru_bx[l], lru_lambda[l], s5_lambda_re[l], s5_lambda_im[l],
                         s5_b_re[l], s5_b_im[l], s5_c_re[l], s5_c_im[l], s5_log_dt[l], s5_d[l],
                         s5_w_glu[l], s5_b_glu[l], nsa_pos_k[l], nsa_pos_v[l], nsa_ck_w1[l], nsa_ck_w2[l],
                         nsa_cv_w1[l], nsa_cv_w2[l], w_branch[l], w_out[l])
    return rms_norm(x, final_norm_g)
```

```python
import functools
import math

import jax
import jax.numpy as jnp
from jax import lax
from jax.experimental import pallas as pl
from jax.experimental.pallas import tpu as pltpu

F32 = jnp.float32
MM = jnp.bfloat16

D_MODEL = 1024
DEPTH = 4
N_BRANCH = 4
BRANCH_W = 512
NORM_EPS = 1e-6
LRU_HEADS = 8
LRU_HD = BRANCH_W // LRU_HEADS
CONV_W = 4
LRU_C = 8.0
S5_GROUP = 16
S5_GROUPS = BRANCH_W // S5_GROUP
S5_STATE = 64
S5_N = S5_GROUPS * S5_STATE
RET_HEADS = 4
RET_HD = BRANCH_W // RET_HEADS
RET_CHUNK = 128
RET_ROPE_BASE = 10000.0
NSA_HEADS = 8
NSA_KV_HEADS = 2
NSA_HD = BRANCH_W // NSA_HEADS
NSA_GQA = NSA_HEADS // NSA_KV_HEADS
CMP_LEN = 32
CMP_STRIDE = 16
CMP_HIDDEN = 256
SLC_BLOCK = 64
SLC_TOPK = 16
WIN = 512
Q_BLOCK = 128
ROPE_THETA = 500000.0
ROPE_DIM = NSA_HD // 4
FORCE_SCORE = 1e4
NEG = -1e30
NSA_KV_W = 6 * NSA_KV_HEADS * NSA_HD
IN_SIZES = (BRANCH_W, BRANCH_W, 3 * BRANCH_W, NSA_HEADS * NSA_HD, NSA_KV_W, 3 * NSA_HEADS,
            N_BRANCH * BRANCH_W, N_BRANCH * D_MODEL)
LANE = 128
MIX_W = sum(IN_SIZES[:5])
BG_OFF = MIX_W
GATE_OFF = BG_OFF + IN_SIZES[5]
MERGE_OFF = GATE_OFF + IN_SIZES[6]
PROJ_W = MIX_W + LANE
VMEM_LIMIT = 56 * 1024 * 1024

TM_PROJ = 256
T_LRU = 256
T_S5 = 256
T_RET = 512


def _cparams(*sem):
    return pltpu.CompilerParams(dimension_semantics=sem, vmem_limit_bytes=VMEM_LIMIT)


def _rms(x, g):
    ms = jnp.mean(x * x, axis=-1, keepdims=True)
    return (x * lax.rsqrt(ms + NORM_EPS)) * g


def _dot(a, b):
    return jnp.dot(a, b, preferred_element_type=F32)


def _dot_nt(a, b):
    return lax.dot_general(a, b, (((1,), (1,)), ((), ())), preferred_element_type=F32)


def _dot_tn(a, b):
    return lax.dot_general(a, b, (((0,), (0,)), ((), ())), preferred_element_type=F32)


def _inproj_kernel(x_ref, g_ref, w_ref, o_ref):
    h = _rms(x_ref[0], g_ref[...])
    o_ref[0] = _dot(h.astype(MM), w_ref[...])


def _inproj(x, g, w):
    b, s, d = x.shape
    n = w.shape[1]
    return pl.pallas_call(
        _inproj_kernel,
        grid=(b, s // TM_PROJ),
        in_specs=[pl.BlockSpec((1, TM_PROJ, d), lambda i, j: (i, j, 0)),
                  pl.BlockSpec((1, d), lambda i, j: (0, 0)),
                  pl.BlockSpec((d, n), lambda i, j: (0, 0))],
        out_specs=pl.BlockSpec((1, TM_PROJ, n), lambda i, j: (i, j, 0)),
        out_shape=jax.ShapeDtypeStruct((b, s, n), F32),
        compiler_params=_cparams("parallel", "arbitrary"),
        name="inproj",
    )(x, g, w)


def _lru_kernel(u_ref, cw_ref, cb_ref, wab_ref, bab_ref, lam_ref, o_ref, ubuf, hcar):
    t = T_LRU
    w = BRANCH_W

    @pl.when(pl.program_id(1) == 0)
    def _():
        ubuf[0:8, :] = jnp.zeros((8, w), F32)
        hcar[...] = jnp.zeros((1, w), F32)

    ubuf[8:8 + t, :] = u_ref[0]
    xc = cb_ref[...] + cw_ref[0:1, :] * ubuf[5:5 + t, :]
    for k in range(1, CONV_W):
        xc = xc + cw_ref[k:k + 1, :] * ubuf[5 + k:5 + k + t, :]
    ubuf[0:8, :] = ubuf[t:t + 8, :]

    ri = _dot(xc.astype(MM), wab_ref[...]) + bab_ref[...]
    r = jax.nn.sigmoid(ri[:, :w])
    gi = jax.nn.sigmoid(ri[:, w:])
    z = -lam_ref[...]
    softplus = jnp.maximum(z, 0.0) + jnp.log1p(jnp.exp(-jnp.abs(z)))
    a = jnp.exp(-LRU_C * r * softplus)
    bt = jnp.sqrt(1.0 - a * a) * gi * xc

    row = lax.broadcasted_iota(jnp.int32, (t, w), 0)
    k = 1
    while k < t:
        keep = row >= k
        a_sh = jnp.where(keep, pltpu.roll(a, k, 0), 1.0)
        b_sh = jnp.where(keep, pltpu.roll(bt, k, 0), 0.0)
        bt = a * b_sh + bt
        a = a * a_sh
        k *= 2
    h = bt + a * hcar[...]
    o_ref[0] = h
    hcar[...] = h[t - 1:t, :]


def _lru(proj, cw, cb, wab, bab, lam):
    b, s, _ = proj.shape
    w = BRANCH_W
    return pl.pallas_call(
        _lru_kernel,
        grid=(b, s // T_LRU),
        in_specs=[pl.BlockSpec((1, T_LRU, w), lambda i, j: (i, j, 0)),
                  pl.BlockSpec((CONV_W, w), lambda i, j: (0, 0)),
                  pl.BlockSpec((1, w), lambda i, j: (0, 0)),
                  pl.BlockSpec((w, 2 * w), lambda i, j: (0, 0)),
                  pl.BlockSpec((1, 2 * w), lambda i, j: (0, 0)),
                  pl.BlockSpec((1, w), lambda i, j: (0, 0))],
        out_specs=pl.BlockSpec((1, T_LRU, w), lambda i, j: (i, j, 0)),
        out_shape=jax.ShapeDtypeStruct((b, s, w), F32),
        scratch_shapes=[pltpu.VMEM((T_LRU + 8, w), F32), pltpu.VMEM((1, w), F32)],
        compiler_params=_cparams("parallel", "arbitrary"),
        name="rg_lru",
    )(proj, cw, cb, wab, bab, lam)


def _block_diag(blocks):
    n, r, c = blocks.shape
    eye = jnp.eye(n, dtype=blocks.dtype)
    return jnp.einsum("nrc,nm->nrmc", blocks, eye).reshape(n * r, n * c)


def _s5_kernel(u_ref, wb_ref, apr_ref, api_ref, wc_ref, d_ref, wg_ref, bg_ref, o_ref, car):
    t = T_S5
    n = S5_N
    w = BRANCH_W

    @pl.when(pl.program_id(1) == 0)
    def _():
        car[...] = jnp.zeros((2, n), F32)

    u = u_ref[0]
    x = _dot(u.astype(MM), wb_ref[...])
    xr = x[:, :n]
    xi = x[:, n:]
    row = lax.broadcasted_iota(jnp.int32, (t, n), 0)
    ar = apr_ref[0:1, :]
    ai = api_ref[0:1, :]
    cr = car[0:1, :]
    ci = car[1:2, :]
    first = row == 0
    xr = xr + jnp.where(first, ar * cr - ai * ci, 0.0)
    xi = xi + jnp.where(first, ar * ci + ai * cr, 0.0)
    k = 1
    step = 0
    while k < t:
        ar = apr_ref[step:step + 1, :]
        ai = api_ref[step:step + 1, :]
        keep = row >= k
        xr_s = jnp.where(keep, pltpu.roll(xr, k, 0), 0.0)
        xi_s = jnp.where(keep, pltpu.roll(xi, k, 0), 0.0)
        xr, xi = xr + (ar * xr_s - ai * xi_s), xi + (ar * xi_s + ai * xr_s)
        k *= 2
        step += 1
    car[0:1, :] = xr[t - 1:t, :]
    car[1:2, :] = xi[t - 1:t, :]
    y = _dot(xr.astype(MM), wc_ref[0:n, :]) + _dot(xi.astype(MM), wc_ref[n:2 * n, :])
    y = y + d_ref[...] * u
    z = jax.nn.gelu(y)
    g = _dot(z.astype(MM), wg_ref[...]) + bg_ref[...]
    o_ref[0] = g[:, :w] * jax.nn.sigmoid(g[:, w:])


def _s5_tables(lam_re, lam_im, b_re, b_im, c_re, c_im, log_dt):
    dt = jnp.exp(log_dt)[:, None]
    lr = jnp.minimum(lam_re, -1e-4)
    li = lam_im
    mag = jnp.exp(lr * dt)
    ab_re = mag * jnp.cos(li * dt)
    ab_im = mag * jnp.sin(li * dt)
    den = lr * lr + li * li
    nr = ab_re - 1.0
    ni = ab_im
    f_re = (nr * lr + ni * li) / den
    f_im = (ni * lr - nr * li) / den
    bb_re = f_re[..., None] * b_re - f_im[..., None] * b_im
    bb_im = f_re[..., None] * b_im + f_im[..., None] * b_re
    wb = jnp.concatenate([_block_diag(jnp.swapaxes(bb_re, 1, 2)),
                          _block_diag(jnp.swapaxes(bb_im, 1, 2))], axis=1)
    wc = jnp.concatenate([_block_diag(jnp.swapaxes(c_re, 1, 2)),
                          _block_diag(jnp.swapaxes(-c_im, 1, 2))], axis=0)
    pr = [ab_re.reshape(1, S5_N)]
    pi = [ab_im.reshape(1, S5_N)]
    for _ in range(int(math.log2(T_S5)) - 1):
        r0, i0 = pr[-1], pi[-1]
        pr.append(r0 * r0 - i0 * i0)
        pi.append(2.0 * r0 * i0)
    return wb.astype(MM), jnp.concatenate(pr, 0), jnp.concatenate(pi, 0), wc.astype(MM)


def _s5(proj, wb, apr, api, wc, d, wg, bg):
    b, s, _ = proj.shape
    w = BRANCH_W
    n = S5_N
    ns = apr.shape[0]
    return pl.pallas_call(
        _s5_kernel,
        grid=(b, s // T_S5),
        in_specs=[pl.BlockSpec((1, T_S5, w), lambda i, j: (i, j, 1)),
                  pl.BlockSpec((w, 2 * n), lambda i, j: (0, 0)),
                  pl.BlockSpec((ns, n), lambda i, j: (0, 0)),
                  pl.BlockSpec((ns, n), lambda i, j: (0, 0)),
                  pl.BlockSpec((2 * n, w), lambda i, j: (0, 0)),
                  pl.BlockSpec((1, w), lambda i, j: (0, 0)),
                  pl.BlockSpec((w, 2 * w), lambda i, j: (0, 0)),
                  pl.BlockSpec((1, 2 * w), lambda i, j: (0, 0))],
        out_specs=pl.BlockSpec((1, T_S5, w), lambda i, j: (i, j, 0)),
        out_shape=jax.ShapeDtypeStruct((b, s, w), F32),
        scratch_shapes=[pltpu.VMEM((2, n), F32)],
        compiler_params=_cparams("parallel", "arbitrary"),
        name="s5",
    )(proj, wb, apr, api, wc, d, wg, bg)


def _ret_kernel(q_ref, k_ref, v_ref, cos_ref, sin_ref, dm_ref, xi_ref, zt_ref, cd_ref, o_ref, rst):
    c = RET_CHUNK
    dh = RET_HD

    @pl.when(pl.program_id(1) == 0)
    def _():
        rst[...] = jnp.zeros((RET_HEADS, dh, dh), F32)

    for ci in range(T_RET // c):
        rows = slice(ci * c, (ci + 1) * c)
        cos = cos_ref[rows, :]
        sin = sin_ref[rows, :]
        for h in range(RET_HEADS):
            cols = slice(h * dh, (h + 1) * dh)
            q = q_ref[0, rows, cols]
            k = k_ref[0, rows, cols]
            v = v_ref[0, rows, cols]
            q = q * cos + pltpu.roll(q, dh // 2, 1) * sin
            k = (k * cos + pltpu.roll(k, dh // 2, 1) * sin) * (dh ** -0.5)
            qb = q.astype(MM)
            kb = k.astype(MM)
            vb = v.astype(MM)
            scores = _dot_nt(qb, kb) * dm_ref[h]
            inner = _dot(scores.astype(MM), vb)
            r_prev = rst[h]
            cross = _dot((q * xi_ref[h]).astype(MM), r_prev.astype(MM))
            kv = _dot_tn((k * zt_ref[h]).astype(MM), vb)
            rst[h] = cd_ref[h] * r_prev + kv
            o = inner + cross
            mu = jnp.mean(o, axis=-1, keepdims=True)
            var = jnp.mean(jnp.square(o - mu), axis=-1, keepdims=True)
            o_ref[0, rows, cols] = (o - mu) * lax.rsqrt(var + 1e-5)


def _ret_tables(s):
    dh = RET_HD
    c = RET_CHUNK
    half = dh // 2
    inv = RET_ROPE_BASE ** (-jnp.arange(half, dtype=F32) * 2.0 / dh)
    ang = jnp.arange(s).astype(F32)[:, None] * inv[None, :]
    cos = jnp.cos(ang)
    sin = jnp.sin(ang)
    cosf = jnp.concatenate([cos, cos], axis=1)
    sinf = jnp.concatenate([-sin, sin], axis=1)
    gamma = 1.0 - jnp.exp(jnp.linspace(math.log(1.0 / 32.0), math.log(1.0 / 512.0), RET_HEADS, dtype=F32))
    log_g = jnp.log(gamma)
    n = jnp.arange(c, dtype=F32)
    diff = n[:, None] - n[None, :]
    dmask = jnp.where(diff[None] >= 0, jnp.exp(jnp.maximum(diff, 0.0)[None] * log_g[:, None, None]), 0.0)
    xi = jnp.exp((n[None, :] + 1.0) * log_g[:, None])
    zeta = jnp.exp((c - 1.0 - n)[None, :] * log_g[:, None])
    cdec = jnp.exp(c * log_g)
    xi_b = jnp.broadcast_to(xi[:, :, None], (RET_HEADS, c, dh))
    zt_b = jnp.broadcast_to(zeta[:, :, None], (RET_HEADS, c, dh))
    cd_b = jnp.broadcast_to(cdec[:, None, None], (RET_HEADS, dh, dh))
    return cosf, sinf, dmask, xi_b, zt_b, cd_b


def _retention(proj, tabs):
    b, s, _ = proj.shape
    w = BRANCH_W
    cosf, sinf, dmask, xi_b, zt_b, cd_b = tabs
    c = RET_CHUNK
    dh = RET_HD
    full3 = lambda i, j: (0, 0, 0)
    return pl.pallas_call(
        _ret_kernel,
        grid=(b, s // T_RET),
        in_specs=[pl.BlockSpec((1, T_RET, w), lambda i, j: (i, j, 2)),
                  pl.BlockSpec((1, T_RET, w), lambda i, j: (i, j, 3)),
                  pl.BlockSpec((1, T_RET, w), lambda i, j: (i, j, 4)),
                  pl.BlockSpec((T_RET, dh), lambda i, j: (j, 0)),
                  pl.BlockSpec((T_RET, dh), lambda i, j: (j, 0)),
                  pl.BlockSpec((RET_HEADS, c, c), full3),
                  pl.BlockSpec((RET_HEADS, c, dh), full3),
                  pl.BlockSpec((RET_HEADS, c, dh), full3),
                  pl.BlockSpec((RET_HEADS, dh, dh), full3)],
        out_specs=pl.BlockSpec((1, T_RET, w), lambda i, j: (i, j, 0)),
        out_shape=jax.ShapeDtypeStruct((b, s, w), F32),
        scratch_shapes=[pltpu.VMEM((RET_HEADS, dh, dh), F32)],
        compiler_params=_cparams("parallel", "arbitrary"),
        name="retention",
    )(proj, proj, proj, cosf, sinf, dmask, xi_b, zt_b, cd_b)


def _rope_lanes(x, cos, s_up, s_dn):
    half = ROPE_DIM // 2
    n = x.shape[-1]
    return x * cos + pltpu.roll(x, half, 1) * s_up + pltpu.roll(x, n - half, 1) * s_dn


def _nsa_rope_kernel(q_ref, ks_ref, kw_ref, cos_ref, up_ref, dn_ref, qo_ref, kso_ref, kwo_ref):
    cos = cos_ref[...]
    up = up_ref[...]
    dn = dn_ref[...]
    scale = NSA_HD ** -0.5
    for cb in range(BRANCH_W // LANE):
        cols = slice(cb * LANE, (cb + 1) * LANE)
        qo_ref[0, :, cols] = _rope_lanes(q_ref[0, :, cols], cos, up, dn) * scale
    kso_ref[0] = _rope_lanes(ks_ref[0], cos, up, dn)
    kwo_ref[0] = _rope_lanes(kw_ref[0], cos, up, dn)


def _nsa_rope_tables(pos):
    half = ROPE_DIM // 2
    inv = ROPE_THETA ** (-jnp.arange(half, dtype=F32) * 2.0 / ROPE_DIM)
    ang = pos.astype(F32)[:, None] * inv[None, :]
    cos = jnp.cos(ang)
    sin = jnp.sin(ang)
    n = pos.shape[0]
    zeros = jnp.zeros((n, NSA_HD - ROPE_DIM), F32)
    zh = jnp.zeros((n, half), F32)
    cos_h = jnp.concatenate([cos, cos, jnp.ones((n, NSA_HD - ROPE_DIM), F32)], axis=1)
    up_h = jnp.concatenate([zh, sin, zeros], axis=1)
    dn_h = jnp.concatenate([-sin, zh, zeros], axis=1)
    tile2 = lambda a: jnp.concatenate([a, a], axis=1)
    return tile2(cos_h), tile2(up_h), tile2(dn_h)


def _nsa_rope(proj, tabs):
    b, s, _ = proj.shape
    t = 512
    w = BRANCH_W
    q_blk = sum(IN_SIZES[:3]) // w
    kv_blk = sum(IN_SIZES[:4]) // LANE
    tab_spec = pl.BlockSpec((t, LANE), lambda i, j: (j, 0))
    return pl.pallas_call(
        _nsa_rope_kernel,
        grid=(b, s // t),
        in_specs=[pl.BlockSpec((1, t, w), lambda i, j: (i, j, q_blk)),
                  pl.BlockSpec((1, t, LANE), lambda i, j: (i, j, kv_blk + 2)),
                  pl.BlockSpec((1, t, LANE), lambda i, j: (i, j, kv_blk + 4)),
                  tab_spec, tab_spec, tab_spec],
        out_specs=[pl.BlockSpec((1, t, w), lambda i, j: (i, j, 0)),
                   pl.BlockSpec((1, t, LANE), lambda i, j: (i, j, 0)),
                   pl.BlockSpec((1, t, LANE), lambda i, j: (i, j, 0))],
        out_shape=[jax.ShapeDtypeStruct((b, s, w), F32),
                   jax.ShapeDtypeStruct((b, s, LANE), F32),
                   jax.ShapeDtypeStruct((b, s, LANE), F32)],
        compiler_params=_cparams("parallel", "parallel"),
        name="nsa_rope",
    )(proj, proj, proj, *tabs)


def _cmp_kernel(hb_ref, pos_ref, w1_ref, w2_ref, cos_ref, up_ref, dn_ref, o_ref):
    nh = hb_ref.shape[3]
    hw = hb_ref.shape[4]
    hb = hb_ref[0, 0, 0].astype(MM)
    w1 = w1_ref[0]
    lo = _dot(hb, w1[:hw, :])
    hi = _dot(hb, w1[hw:, :])
    posb = _dot(pos_ref[0].astype(MM), w1)[0:1, :]
    hid = jax.nn.gelu(lo + pltpu.roll(hi, nh - 1, 0) + posb)
    out = _dot(hid.astype(MM), w2_ref[0])

    @pl.when(pl.program_id(0) == 0)
    def _():
        o_ref[0, 0, 0] = _rope_lanes(out, cos_ref[...], up_ref[...], dn_ref[...])

    @pl.when(pl.program_id(0) == 1)
    def _():
        o_ref[0, 0, 0] = out


def _nsa_compress(hb, pos, w1, w2, tabs):
    _, b, hkv, nh, hw = hb.shape
    full2 = lambda kd, i, h: (0, 0)
    return pl.pallas_call(
        _cmp_kernel,
        grid=(2, b, hkv),
        in_specs=[pl.BlockSpec((1, 1, 1, nh, hw), lambda kd, i, h: (kd, i, h, 0, 0)),
                  pl.BlockSpec((1, 8, 2 * hw), lambda kd, i, h: (kd, 0, 0)),
                  pl.BlockSpec((1, 2 * hw, CMP_HIDDEN), lambda kd, i, h: (kd, 0, 0)),
                  pl.BlockSpec((1, CMP_HIDDEN, LANE), lambda kd, i, h: (kd, 0, 0)),
                  pl.BlockSpec((nh, LANE), full2),
                  pl.BlockSpec((nh, LANE), full2),
                  pl.BlockSpec((nh, LANE), full2)],
        out_specs=pl.BlockSpec((1, 1, 1, nh, LANE), lambda kd, i, h: (kd, i, h, 0, 0)),
        out_shape=jax.ShapeDtypeStruct((2, b, hkv, nh, LANE), F32),
        compiler_params=_cparams("parallel", "parallel", "parallel"),
        name="nsa_compress",
    )(hb, pos, w1, w2, *tabs)


def _softmax_update(s, v, m, l, acc):
    m_new = jnp.maximum(m, jnp.max(s, axis=-1, keepdims=True))
    alpha = jnp.exp(m - m_new)
    p = jnp.exp(s - m_new)
    l = alpha * l + jnp.sum(p, axis=-1, keepdims=True)
    acc = alpha * acc + _dot(p.astype(MM), v)
    return m_new, l, acc


def _nsa_attn_kernel(q_ref, gt_ref, kc_ref, vc_ref, ovl_ref, ka_ref, vs_ref, kw_ref, vw_ref, o_ref):
    blk = pl.program_id(2)
    qb = Q_BLOCK
    rows = NSA_GQA * qb
    ncmp_pad = kc_ref.shape[2]
    nsel = ovl_ref.shape[1]
    qf = q_ref[0, 0, 0]
    qm = qf.astype(MM)
    t = blk * qb + (lax.broadcasted_iota(jnp.int32, (rows, 1), 0) & (qb - 1))

    s_c = _dot_nt(qf, kc_ref[0, 0])
    n_id = lax.broadcasted_iota(jnp.int32, (1, ncmp_pad), 1)
    m_c = (n_id * CMP_STRIDE + (CMP_LEN - 1) <= t) & (n_id < ncmp_pad - 1)
    smax = jnp.max(jnp.where(m_c, s_c, NEG), axis=-1, keepdims=True)
    e_c = jnp.where(m_c, jnp.exp(s_c - smax), 0.0)
    den = jnp.sum(e_c, axis=-1, keepdims=True)
    p_c = e_c / jnp.where(den > 0.0, den, 1.0)
    o_c = _dot(p_c, vc_ref[0, 0])[:, :NSA_HD]

    psum = p_c[0:qb] + p_c[qb:2 * qb] + p_c[2 * qb:3 * qb] + p_c[3 * qb:4 * qb]
    imp = jnp.dot(psum, ovl_ref[...], preferred_element_type=F32, precision=lax.Precision.HIGHEST)
    tq = blk * qb + lax.broadcasted_iota(jnp.int32, (qb, 1), 0)
    cur = tq // SLC_BLOCK
    jid = lax.broadcasted_iota(jnp.int32, (qb, nsel), 1)
    valid = jid <= cur
    forced = (jid == 0) | (jid == cur) | (jid == cur - 1)
    score = jnp.where(valid, jnp.where(forced, FORCE_SCORE, imp), -1.0)
    sel = jnp.zeros((qb, nsel), F32)
    for _ in range(SLC_TOPK):
        mx = jnp.max(score, axis=-1, keepdims=True)
        idx = jnp.min(jnp.where(score == mx, jid, nsel), axis=-1, keepdims=True)
        pick = jid == idx
        sel = jnp.where(pick & (mx >= 0.0), 1.0, sel)
        score = jnp.where(pick, -3e38, score)
    bias = jnp.where(sel > 0.0, 0.0, NEG).astype(MM)
    q_aug = jnp.concatenate([jnp.concatenate([bias] * NSA_GQA, axis=0), qm], axis=1)

    kpos0 = lax.broadcasted_iota(jnp.int32, (1, qb), 1)
    init = (jnp.full((rows, 1), NEG, F32), jnp.zeros((rows, 1), F32), jnp.zeros((rows, NSA_HD), F32))

    def sel_body(j, carry):
        off = pl.multiple_of(j * qb, qb)
        s = _dot_nt(q_aug, ka_ref[0, 0, pl.ds(off, qb), :])
        return _softmax_update(s, vs_ref[0, 0, pl.ds(off, qb), :], *carry)

    carry = lax.fori_loop(0, blk, sel_body, init)
    off = pl.multiple_of(blk * qb, qb)
    s = _dot_nt(q_aug, ka_ref[0, 0, pl.ds(off, qb), :])
    s = jnp.where(blk * qb + kpos0 <= t, s, NEG)
    _, l_s, acc_s = _softmax_update(s, vs_ref[0, 0, pl.ds(off, qb), :], *carry)
    o_s = acc_s / l_s

    carry = init
    for back in range(WIN // qb, -1, -1):
        c = blk - back
        off = pl.multiple_of(jnp.maximum(c, 0) * qb, qb)
        wpos = c * qb + kpos0
        s = _dot_nt(qm, kw_ref[0, 0, pl.ds(off, qb), :])
        s = jnp.where((wpos <= t) & (wpos > t - WIN) & (wpos >= 0), s, NEG)
        carry = _softmax_update(s, vw_ref[0, 0, pl.ds(off, qb), :], *carry)
    _, l_w, acc_w = carry
    o_w = acc_w / l_w

    g = jax.nn.sigmoid(gt_ref[0, 0, 0])
    o_ref[0, 0, 0] = g[:, 0:1] * o_c + g[:, 1:2] * o_s + g[:, 2:3] * o_w


def _nsa_attn(qp, gt, kc, vc, ovl, ka, vs, kw, vw):
    b, hkv, nqb, rows, _ = qp.shape
    s = ka.shape[2]
    ncmp_pad = kc.shape[2]
    per_head = lambda i, h, j: (i, h, 0, 0)
    per_blk = lambda i, h, j: (i, h, j, 0, 0)
    return pl.pallas_call(
        _nsa_attn_kernel,
        grid=(b, hkv, nqb),
        in_specs=[pl.BlockSpec((1, 1, 1, rows, LANE), per_blk),
                  pl.BlockSpec((1, 1, 1, rows, 8), per_blk),
                  pl.BlockSpec((1, 1, ncmp_pad, LANE), per_head),
                  pl.BlockSpec((1, 1, ncmp_pad, LANE), per_head),
                  pl.BlockSpec(ovl.shape, lambda i, h, j: (0, 0)),
                  pl.BlockSpec((1, 1, s, 2 * LANE), per_head),
                  pl.BlockSpec((1, 1, s, NSA_HD), per_head),
                  pl.BlockSpec((1, 1, s, LANE), per_head),
                  pl.BlockSpec((1, 1, s, NSA_HD), per_head)],
        out_specs=pl.BlockSpec((1, 1, 1, rows, NSA_HD), per_blk),
        out_shape=jax.ShapeDtypeStruct((b, hkv, nqb, rows, NSA_HD), F32),
        compiler_params=_cparams("parallel", "parallel", "arbitrary"),
        name="nsa_attn",
    )(qp, gt, kc, vc, ovl, ka, vs, kw, vw)


def _nsa(proj, pos_k, pos_v, ck_w1, ck_w2, cv_w1, cv_w2, tabs):
    b, s, _ = proj.shape
    hkv, g, dh, qb = NSA_KV_HEADS, NSA_GQA, NSA_HD, Q_BLOCK
    nqb = s // qb
    nhalf = s // CMP_STRIDE
    ncmp = (s - CMP_LEN) // CMP_STRIDE + 1
    nsel = s // SLC_BLOCK
    tok_tabs, cmp_tabs, overlap = tabs

    q_r, ks_r, kw_r = _nsa_rope(proj, tok_tabs)

    kv0 = sum(IN_SIZES[:4])
    kv = proj[:, :, kv0:kv0 + NSA_KV_W].reshape(b, s, 6, hkv, dh)
    heads = lambda a: jnp.transpose(a.reshape(b, s, hkv, dh), (0, 2, 1, 3))
    hb = jnp.stack([heads(kv[:, :, 0]), heads(kv[:, :, 1])]).reshape(2, b, hkv, nhalf, CMP_STRIDE * dh)
    pos = jnp.stack([pos_k, pos_v]).reshape(2, 1, CMP_LEN * dh)
    pos = jnp.broadcast_to(pos, (2, 8, CMP_LEN * dh))
    w1 = jnp.stack([ck_w1, cv_w1]).astype(MM)
    w2 = jnp.pad(jnp.stack([ck_w2, cv_w2]), ((0, 0), (0, 0), (0, LANE - dh))).astype(MM)
    cmp = _nsa_compress(hb, pos, w1, w2, cmp_tabs)
    kc, vc = cmp[0], cmp[1]

    pad_lanes = lambda a, n: jnp.pad(a, [(0, 0)] * (a.ndim - 1) + [(0, n)])
    qp = q_r.reshape(b, nqb, qb, hkv, g, dh).transpose(0, 3, 1, 4, 2, 5).reshape(b, hkv, nqb, g * qb, dh)
    qp = pad_lanes(qp, LANE - dh)
    bg = proj[:, :, BG_OFF:BG_OFF + 3 * NSA_HEADS].reshape(b, nqb, qb, 3, hkv, g)
    gt = pad_lanes(bg.transpose(0, 4, 1, 5, 2, 3).reshape(b, hkv, nqb, g * qb, 3), 5)
    assert nsel <= LANE
    onehot = (jnp.arange(s)[:, None] // SLC_BLOCK == jnp.arange(LANE)[None, :]).astype(MM)
    ks_h = heads(ks_r).astype(MM)
    ka = jnp.concatenate([jnp.broadcast_to(onehot, (b, hkv, s, LANE)), ks_h,
                          jnp.zeros((b, hkv, s, LANE - dh), MM)], axis=-1)
    assert ka.shape[-1] == 2 * LANE
    vs = heads(kv[:, :, 3]).astype(MM)
    kw = pad_lanes(heads(kw_r), LANE - dh).astype(MM)
    vw = heads(kv[:, :, 5]).astype(MM)
    o = _nsa_attn(qp, gt, kc, vc, overlap, ka, vs, kw, vw)
    return o.reshape(b, hkv, nqb, g, qb, dh).transpose(0, 2, 4, 1, 3, 5).reshape(b, s, hkv * g * dh)


def _nsa_tables(s):
    ncmp = (s - CMP_LEN) // CMP_STRIDE + 1
    nhalf = s // CMP_STRIDE
    nsel = s // SLC_BLOCK
    tok_tabs = _nsa_rope_tables(jnp.arange(s))
    cmp_tabs = _nsa_rope_tables(jnp.arange(nhalf) * CMP_STRIDE + CMP_LEN - 1)
    cmp_tok_blk = (jnp.arange(ncmp)[:, None] * CMP_STRIDE + jnp.arange(CMP_LEN)[None, :]) // SLC_BLOCK
    overlap = jnp.mean((cmp_tok_blk[..., None] == jnp.arange(nsel)).astype(F32), axis=1)
    overlap = jnp.pad(overlap, ((0, nhalf - ncmp), (0, LANE - nsel)))
    return tok_tabs, cmp_tabs, overlap


def _out_kernel(x_ref, g_ref, yl_ref, ys_ref, yr_ref, yn_ref, wg_ref, wm_ref, wb_ref, wo_ref,
                fg_ref, o_ref, *, final):
    x = x_ref[0]
    h = _rms(x, g_ref[...]).astype(MM)
    w = BRANCH_W
    d = D_MODEL
    merged = jnp.zeros(x.shape, F32)
    for n, y_ref in enumerate((yl_ref, ys_ref, yr_ref, yn_ref)):
        gate = jax.nn.silu(_dot(h, wg_ref[:, n * w:(n + 1) * w]))
        branch = _dot((y_ref[0] * gate).astype(MM), wb_ref[n])
        merged = merged + jax.nn.sigmoid(_dot(h, wm_ref[:, n * d:(n + 1) * d])) * branch
    out = x + _dot(merged.astype(MM), wo_ref[...])
    if final:
        out = _rms(out, fg_ref[...])
    o_ref[0] = out


def _out(x, g, ys, wg, wm, wb, wo, fg, final):
    b, s, d = x.shape
    w = BRANCH_W
    tm = TM_PROJ
    row = lambda i, j: (i, j, 0)
    c2 = lambda i, j: (0, 0)
    once = pl.Buffered(1)
    return pl.pallas_call(
        functools.partial(_out_kernel, final=final),
        grid=(b, s // tm),
        in_specs=[pl.BlockSpec((1, tm, d), row),
                  pl.BlockSpec((1, d), c2)]
                 + [pl.BlockSpec((1, tm, w), row)] * N_BRANCH
                 + [pl.BlockSpec((d, N_BRANCH * w), c2, pipeline_mode=once),
                    pl.BlockSpec((d, N_BRANCH * d), c2, pipeline_mode=once),
                    pl.BlockSpec((N_BRANCH, w, d), lambda i, j: (0, 0, 0), pipeline_mode=once),
                    pl.BlockSpec((d, d), c2, pipeline_mode=once),
                    pl.BlockSpec((1, d), c2)],
        out_specs=pl.BlockSpec((1, tm, d), row),
        out_shape=jax.ShapeDtypeStruct((b, s, d), F32),
        compiler_params=_cparams("parallel", "arbitrary"),
        name="gate_merge_out",
    )(x, g, *ys, wg, wm, wb, wo, fg)


def _layer(x, p, l, ret_tabs, nsa_tabs, final_g):
    w_in = p["w_in"][l]
    w_mix = jnp.concatenate([w_in[:, :MIX_W], w_in[:, BG_OFF:GATE_OFF],
                             jnp.zeros((D_MODEL, LANE - IN_SIZES[5]), F32)], axis=1).astype(MM)
    g = p["norm_g"][l].reshape(1, D_MODEL)
    proj = _inproj(x, g, w_mix)

    wab = jnp.concatenate([_block_diag(p["lru_wa"][l]), _block_diag(p["lru_wx"][l])], axis=1).astype(MM)
    bab = jnp.concatenate([p["lru_ba"][l], p["lru_bx"][l]]).reshape(1, 2 * BRANCH_W)
    y_lru = _lru(proj, p["lru_conv_w"][l], p["lru_conv_b"][l].reshape(1, -1), wab, bab,
                 p["lru_lambda"][l].reshape(1, -1))

    wb, apr, api, wc = _s5_tables(p["s5_lambda_re"][l], p["s5_lambda_im"][l], p["s5_b_re"][l],
                                  p["s5_b_im"][l], p["s5_c_re"][l], p["s5_c_im"][l], p["s5_log_dt"][l])
    y_s5 = _s5(proj, wb, apr, api, wc, p["s5_d"][l].reshape(1, -1), p["s5_w_glu"][l].astype(MM),
               p["s5_b_glu"][l].reshape(1, -1))

    y_ret = _retention(proj, ret_tabs)

    y_nsa = _nsa(proj, p["nsa_pos_k"][l], p["nsa_pos_v"][l], p["nsa_ck_w1"][l], p["nsa_ck_w2"][l],
                 p["nsa_cv_w1"][l], p["nsa_cv_w2"][l], nsa_tabs)

    wg = w_in[:, GATE_OFF:MERGE_OFF].astype(MM)
    wm = w_in[:, MERGE_OFF:].astype(MM)
    final = l == DEPTH - 1
    return _out(x, g, (y_lru, y_s5, y_ret, y_nsa), wg, wm, p["w_branch"][l].astype(MM),
                p["w_out"][l].astype(MM), final_g.reshape(1, D_MODEL), final)


def kernel(x, norm_g, w_in, lru_conv_w, lru_conv_b, lru_wa, lru_ba, lru_wx, lru_bx, lru_lambda,
           s5_lambda_re, s5_lambda_im, s5_b_re, s5_b_im, s5_c_re, s5_c_im, s5_log_dt, s5_d,
           s5_w_glu, s5_b_glu, nsa_pos_k, nsa_pos_v, nsa_ck_w1, nsa_ck_w2, nsa_cv_w1, nsa_cv_w2,
           w_branch, w_out, final_norm_g):
    p = dict(norm_g=norm_g, w_in=w_in, lru_conv_w=lru_conv_w, lru_conv_b=lru_conv_b, lru_wa=lru_wa,
             lru_ba=lru_ba, lru_wx=lru_wx, lru_bx=lru_bx, lru_lambda=lru_lambda,
             s5_lambda_re=s5_lambda_re, s5_lambda_im=s5_lambda_im, s5_b_re=s5_b_re, s5_b_im=s5_b_im,
             s5_c_re=s5_c_re, s5_c_im=s5_c_im, s5_log_dt=s5_log_dt, s5_d=s5_d, s5_w_glu=s5_w_glu,
             s5_b_glu=s5_b_glu, nsa_pos_k=nsa_pos_k, nsa_pos_v=nsa_pos_v, nsa_ck_w1=nsa_ck_w1,
             nsa_ck_w2=nsa_ck_w2, nsa_cv_w1=nsa_cv_w1, nsa_cv_w2=nsa_cv_w2, w_branch=w_branch,
             w_out=w_out)
    s = x.shape[1]
    ret_tabs = _ret_tables(s)
    nsa_tabs = _nsa_tables(s)
    for l in range(DEPTH):
        x = _layer(x, p, l, ret_tabs, nsa_tabs, final_norm_g)
    return x
```

```python
import functools
import math

import jax
import jax.numpy as jnp
from jax import lax
from jax.experimental import pallas as pl
from jax.experimental.pallas import tpu as pltpu

F32 = jnp.float32
MM = jnp.bfloat16

D_MODEL = 1024
DEPTH = 4
N_BRANCH = 4
BRANCH_W = 512
NORM_EPS = 1e-6
LRU_HEADS = 8
LRU_HD = BRANCH_W // LRU_HEADS
CONV_W = 4
LRU_C = 8.0
S5_GROUP = 16
S5_GROUPS = BRANCH_W // S5_GROUP
S5_STATE = 64
S5_N = S5_GROUPS * S5_STATE
RET_HEADS = 4
RET_HD = BRANCH_W // RET_HEADS
RET_CHUNK = 128
RET_ROPE_BASE = 10000.0
NSA_HEADS = 8
NSA_KV_HEADS = 2
NSA_HD = BRANCH_W // NSA_HEADS
NSA_GQA = NSA_HEADS // NSA_KV_HEADS
CMP_LEN = 32
CMP_STRIDE = 16
CMP_HIDDEN = 256
SLC_BLOCK = 64
SLC_TOPK = 16
WIN = 512
Q_BLOCK = 128
ROPE_THETA = 500000.0
ROPE_DIM = NSA_HD // 4
FORCE_SCORE = 1e4
NEG = -1e30
NSA_KV_W = 6 * NSA_KV_HEADS * NSA_HD
IN_SIZES = (BRANCH_W, BRANCH_W, 3 * BRANCH_W, NSA_HEADS * NSA_HD, NSA_KV_W, 3 * NSA_HEADS,
            N_BRANCH * BRANCH_W, N_BRANCH * D_MODEL)
LANE = 128
MIX_W = sum(IN_SIZES[:5])
BG_OFF = MIX_W
GATE_OFF = BG_OFF + IN_SIZES[5]
MERGE_OFF = GATE_OFF + IN_SIZES[6]
PROJ_W = MIX_W + LANE
VMEM_LIMIT = 56 * 1024 * 1024

TM_PROJ = 256
T_LRU = 256
T_S5 = 256
T_RET = 512
SEL_CHUNK = 512


def _cparams(*sem):
    return pltpu.CompilerParams(dimension_semantics=sem, vmem_limit_bytes=VMEM_LIMIT)


def _rms(x, g):
    ms = jnp.mean(x * x, axis=-1, keepdims=True)
    return (x * lax.rsqrt(ms + NORM_EPS)) * g


def _dot(a, b):
    return jnp.dot(a, b, preferred_element_type=F32)


def _dot_nt(a, b):
    return lax.dot_general(a, b, (((1,), (1,)), ((), ())), preferred_element_type=F32)


def _dot_tn(a, b):
    return lax.dot_general(a, b, (((0,), (0,)), ((), ())), preferred_element_type=F32)


def _inproj_kernel(x_ref, g_ref, w_ref, o_ref):
    h = _rms(x_ref[0], g_ref[...])
    o_ref[0] = _dot(h.astype(MM), w_ref[...])


def _inproj(x, g, w):
    b, s, d = x.shape
    n = w.shape[1]
    return pl.pallas_call(
        _inproj_kernel,
        grid=(b, s // TM_PROJ),
        in_specs=[pl.BlockSpec((1, TM_PROJ, d), lambda i, j: (i, j, 0)),
                  pl.BlockSpec((1, d), lambda i, j: (0, 0)),
                  pl.BlockSpec((d, n), lambda i, j: (0, 0))],
        out_specs=pl.BlockSpec((1, TM_PROJ, n), lambda i, j: (i, j, 0)),
        out_shape=jax.ShapeDtypeStruct((b, s, n), F32),
        compiler_params=_cparams("parallel", "arbitrary"),
        name="inproj",
    )(x, g, w)


def _lru_kernel(u_ref, cw_ref, cb_ref, wab_ref, bab_ref, lam_ref, o_ref, ubuf, hcar):
    t = T_LRU
    w = BRANCH_W

    @pl.when(pl.program_id(1) == 0)
    def _():
        ubuf[0:8, :] = jnp.zeros((8, w), F32)
        hcar[...] = jnp.zeros((1, w), F32)

    ubuf[8:8 + t, :] = u_ref[0]
    xc = cb_ref[...] + cw_ref[0:1, :] * ubuf[5:5 + t, :]
    for k in range(1, CONV_W):
        xc = xc + cw_ref[k:k + 1, :] * ubuf[5 + k:5 + k + t, :]
    ubuf[0:8, :] = ubuf[t:t + 8, :]

    ri = _dot(xc.astype(MM), wab_ref[...]) + bab_ref[...]
    r = jax.nn.sigmoid(ri[:, :w])
    gi = jax.nn.sigmoid(ri[:, w:])
    z = -lam_ref[...]
    softplus = jnp.maximum(z, 0.0) + jnp.log1p(jnp.exp(-jnp.abs(z)))
    a = jnp.exp(-LRU_C * r * softplus)
    bt = jnp.sqrt(1.0 - a * a) * gi * xc

    row = lax.broadcasted_iota(jnp.int32, (t, w), 0)
    k = 1
    while k < t:
        keep = row >= k
        a_sh = jnp.where(keep, pltpu.roll(a, k, 0), 1.0)
        b_sh = jnp.where(keep, pltpu.roll(bt, k, 0), 0.0)
        bt = a * b_sh + bt
        a = a * a_sh
        k *= 2
    h = bt + a * hcar[...]
    o_ref[0] = h
    hcar[...] = h[t - 1:t, :]


def _lru(proj, cw, cb, wab, bab, lam):
    b, s, _ = proj.shape
    w = BRANCH_W
    return pl.pallas_call(
        _lru_kernel,
        grid=(b, s // T_LRU),
        in_specs=[pl.BlockSpec((1, T_LRU, w), lambda i, j: (i, j, 0)),
                  pl.BlockSpec((CONV_W, w), lambda i, j: (0, 0)),
                  pl.BlockSpec((1, w), lambda i, j: (0, 0)),
                  pl.BlockSpec((w, 2 * w), lambda i, j: (0, 0)),
                  pl.BlockSpec((1, 2 * w), lambda i, j: (0, 0)),
                  pl.BlockSpec((1, w), lambda i, j: (0, 0))],
        out_specs=pl.BlockSpec((1, T_LRU, w), lambda i, j: (i, j, 0)),
        out_shape=jax.ShapeDtypeStruct((b, s, w), F32),
        scratch_shapes=[pltpu.VMEM((T_LRU + 8, w), F32), pltpu.VMEM((1, w), F32)],
        compiler_params=_cparams("parallel", "arbitrary"),
        name="rg_lru",
    )(proj, cw, cb, wab, bab, lam)


def _block_diag(blocks):
    n, r, c = blocks.shape
    eye = jnp.eye(n, dtype=blocks.dtype)
    return jnp.einsum("nrc,nm->nrmc", blocks, eye).reshape(n * r, n * c)


def _s5_kernel(u_ref, wb_ref, apr_ref, api_ref, wc_ref, d_ref, wg_ref, bg_ref, o_ref, car):
    t = T_S5
    n = S5_N
    w = BRANCH_W

    @pl.when(pl.program_id(1) == 0)
    def _():
        car[...] = jnp.zeros((2, n), F32)

    u = u_ref[0]
    x = _dot(u.astype(MM), wb_ref[...])
    xr = x[:, :n]
    xi = x[:, n:]
    row = lax.broadcasted_iota(jnp.int32, (t, n), 0)
    ar = apr_ref[0:1, :]
    ai = api_ref[0:1, :]
    cr = car[0:1, :]
    ci = car[1:2, :]
    first = row == 0
    xr = xr + jnp.where(first, ar * cr - ai * ci, 0.0)
    xi = xi + jnp.where(first, ar * ci + ai * cr, 0.0)
    k = 1
    step = 0
    while k < t:
        ar = apr_ref[step:step + 1, :]
        ai = api_ref[step:step + 1, :]
        keep = row >= k
        xr_s = jnp.where(keep, pltpu.roll(xr, k, 0), 0.0)
        xi_s = jnp.where(keep, pltpu.roll(xi, k, 0), 0.0)
        xr, xi = xr + (ar * xr_s - ai * xi_s), xi + (ar * xi_s + ai * xr_s)
        k *= 2
        step += 1
    car[0:1, :] = xr[t - 1:t, :]
    car[1:2, :] = xi[t - 1:t, :]
    y = _dot(xr.astype(MM), wc_ref[0:n, :]) + _dot(xi.astype(MM), wc_ref[n:2 * n, :])
    y = y + d_ref[...] * u
    z = jax.nn.gelu(y)
    g = _dot(z.astype(MM), wg_ref[...]) + bg_ref[...]
    o_ref[0] = g[:, :w] * jax.nn.sigmoid(g[:, w:])


def _s5_tables(lam_re, lam_im, b_re, b_im, c_re, c_im, log_dt):
    dt = jnp.exp(log_dt)[:, None]
    lr = jnp.minimum(lam_re, -1e-4)
    li = lam_im
    mag = jnp.exp(lr * dt)
    ab_re = mag * jnp.cos(li * dt)
    ab_im = mag * jnp.sin(li * dt)
    den = lr * lr + li * li
    nr = ab_re - 1.0
    ni = ab_im
    f_re = (nr * lr + ni * li) / den
    f_im = (ni * lr - nr * li) / den
    bb_re = f_re[..., None] * b_re - f_im[..., None] * b_im
    bb_im = f_re[..., None] * b_im + f_im[..., None] * b_re
    wb = jnp.concatenate([_block_diag(jnp.swapaxes(bb_re, 1, 2)),
                          _block_diag(jnp.swapaxes(bb_im, 1, 2))], axis=1)
    wc = jnp.concatenate([_block_diag(jnp.swapaxes(c_re, 1, 2)),
                          _block_diag(jnp.swapaxes(-c_im, 1, 2))], axis=0)
    pr = [ab_re.reshape(1, S5_N)]
    pi = [ab_im.reshape(1, S5_N)]
    for _ in range(int(math.log2(T_S5)) - 1):
        r0, i0 = pr[-1], pi[-1]
        pr.append(r0 * r0 - i0 * i0)
        pi.append(2.0 * r0 * i0)
    return wb.astype(MM), jnp.concatenate(pr, 0), jnp.concatenate(pi, 0), wc.astype(MM)


def _s5(proj, wb, apr, api, wc, d, wg, bg):
    b, s, _ = proj.shape
    w = BRANCH_W
    n = S5_N
    ns = apr.shape[0]
    return pl.pallas_call(
        _s5_kernel,
        grid=(b, s // T_S5),
        in_specs=[pl.BlockSpec((1, T_S5, w), lambda i, j: (i, j, 1)),
                  pl.BlockSpec((w, 2 * n), lambda i, j: (0, 0)),
                  pl.BlockSpec((ns, n), lambda i, j: (0, 0)),
                  pl.BlockSpec((ns, n), lambda i, j: (0, 0)),
                  pl.BlockSpec((2 * n, w), lambda i, j: (0, 0)),
                  pl.BlockSpec((1, w), lambda i, j: (0, 0)),
                  pl.BlockSpec((w, 2 * w), lambda i, j: (0, 0)),
                  pl.BlockSpec((1, 2 * w), lambda i, j: (0, 0))],
        out_specs=pl.BlockSpec((1, T_S5, w), lambda i, j: (i, j, 0)),
        out_shape=jax.ShapeDtypeStruct((b, s, w), F32),
        scratch_shapes=[pltpu.VMEM((2, n), F32)],
        compiler_params=_cparams("parallel", "arbitrary"),
        name="s5",
    )(proj, wb, apr, api, wc, d, wg, bg)


def _ret_kernel(q_ref, k_ref, v_ref, cos_ref, sin_ref, dm_ref, xi_ref, zt_ref, cd_ref, o_ref, rst):
    c = RET_CHUNK
    dh = RET_HD

    @pl.when(pl.program_id(1) == 0)
    def _():
        rst[...] = jnp.zeros((RET_HEADS, dh, dh), F32)

    for ci in range(T_RET // c):
        rows = slice(ci * c, (ci + 1) * c)
        cos = cos_ref[rows, :]
        sin = sin_ref[rows, :]
        for h in range(RET_HEADS):
            cols = slice(h * dh, (h + 1) * dh)
            q = q_ref[0, rows, cols]
            k = k_ref[0, rows, cols]
            v = v_ref[0, rows, cols]
            q = q * cos + pltpu.roll(q, dh // 2, 1) * sin
            k = (k * cos + pltpu.roll(k, dh // 2, 1) * sin) * (dh ** -0.5)
            qb = q.astype(MM)
            kb = k.astype(MM)
            vb = v.astype(MM)
            scores = _dot_nt(qb, kb) * dm_ref[h]
            inner = _dot(scores.astype(MM), vb)
            r_prev = rst[h]
            cross = _dot((q * xi_ref[h]).astype(MM), r_prev.astype(MM))
            kv = _dot_tn((k * zt_ref[h]).astype(MM), vb)
            rst[h] = cd_ref[h] * r_prev + kv
            o = inner + cross
            mu = jnp.mean(o, axis=-1, keepdims=True)
            var = jnp.mean(jnp.square(o - mu), axis=-1, keepdims=True)
            o_ref[0, rows, cols] = (o - mu) * lax.rsqrt(var + 1e-5)


def _ret_tables(s):
    dh = RET_HD
    c = RET_CHUNK
    half = dh // 2
    inv = RET_ROPE_BASE ** (-jnp.arange(half, dtype=F32) * 2.0 / dh)
    ang = jnp.arange(s).astype(F32)[:, None] * inv[None, :]
    cos = jnp.cos(ang)
    sin = jnp.sin(ang)
    cosf = jnp.concatenate([cos, cos], axis=1)
    sinf = jnp.concatenate([-sin, sin], axis=1)
    gamma = 1.0 - jnp.exp(jnp.linspace(math.log(1.0 / 32.0), math.log(1.0 / 512.0), RET_HEADS, dtype=F32))
    log_g = jnp.log(gamma)
    n = jnp.arange(c, dtype=F32)
    diff = n[:, None] - n[None, :]
    dmask = jnp.where(diff[None] >= 0, jnp.exp(jnp.maximum(diff, 0.0)[None] * log_g[:, None, None]), 0.0)
    xi = jnp.exp((n[None, :] + 1.0) * log_g[:, None])
    zeta = jnp.exp((c - 1.0 - n)[None, :] * log_g[:, None])
    cdec = jnp.exp(c * log_g)
    xi_b = jnp.broadcast_to(xi[:, :, None], (RET_HEADS, c, dh))
    zt_b = jnp.broadcast_to(zeta[:, :, None], (RET_HEADS, c, dh))
    cd_b = jnp.broadcast_to(cdec[:, None, None], (RET_HEADS, dh, dh))
    return cosf, sinf, dmask, xi_b, zt_b, cd_b


def _retention(proj, tabs):
    b, s, _ = proj.shape
    w = BRANCH_W
    cosf, sinf, dmask, xi_b, zt_b, cd_b = tabs
    c = RET_CHUNK
    dh = RET_HD
    full3 = lambda i, j: (0, 0, 0)
    return pl.pallas_call(
        _ret_kernel,
        grid=(b, s // T_RET),
        in_specs=[pl.BlockSpec((1, T_RET, w), lambda i, j: (i, j, 2)),
                  pl.BlockSpec((1, T_RET, w), lambda i, j: (i, j, 3)),
                  pl.BlockSpec((1, T_RET, w), lambda i, j: (i, j, 4)),
                  pl.BlockSpec((T_RET, dh), lambda i, j: (j, 0)),
                  pl.BlockSpec((T_RET, dh), lambda i, j: (j, 0)),
                  pl.BlockSpec((RET_HEADS, c, c), full3),
                  pl.BlockSpec((RET_HEADS, c, dh), full3),
                  pl.BlockSpec((RET_HEADS, c, dh), full3),
                  pl.BlockSpec((RET_HEADS, dh, dh), full3)],
        out_specs=pl.BlockSpec((1, T_RET, w), lambda i, j: (i, j, 0)),
        out_shape=jax.ShapeDtypeStruct((b, s, w), F32),
        scratch_shapes=[pltpu.VMEM((RET_HEADS, dh, dh), F32)],
        compiler_params=_cparams("parallel", "arbitrary"),
        name="retention",
    )(proj, proj, proj, cosf, sinf, dmask, xi_b, zt_b, cd_b)


def _rope_lanes(x, cos, s_up, s_dn):
    half = ROPE_DIM // 2
    n = x.shape[-1]
    return x * cos + pltpu.roll(x, half, 1) * s_up + pltpu.roll(x, n - half, 1) * s_dn


def _nsa_rope_kernel(q_ref, ks_ref, kw_ref, cos_ref, up_ref, dn_ref, qo_ref, kso_ref, kwo_ref):
    cos = cos_ref[...]
    up = up_ref[...]
    dn = dn_ref[...]
    scale = NSA_HD ** -0.5
    for cb in range(BRANCH_W // LANE):
        cols = slice(cb * LANE, (cb + 1) * LANE)
        qo_ref[0, :, cols] = _rope_lanes(q_ref[0, :, cols], cos, up, dn) * scale
    kso_ref[0] = _rope_lanes(ks_ref[0], cos, up, dn)
    kwo_ref[0] = _rope_lanes(kw_ref[0], cos, up, dn)


def _nsa_rope_tables(pos):
    half = ROPE_DIM // 2
    inv = ROPE_THETA ** (-jnp.arange(half, dtype=F32) * 2.0 / ROPE_DIM)
    ang = pos.astype(F32)[:, None] * inv[None, :]
    cos = jnp.cos(ang)
    sin = jnp.sin(ang)
    n = pos.shape[0]
    zeros = jnp.zeros((n, NSA_HD - ROPE_DIM), F32)
    zh = jnp.zeros((n, half), F32)
    cos_h = jnp.concatenate([cos, cos, jnp.ones((n, NSA_HD - ROPE_DIM), F32)], axis=1)
    up_h = jnp.concatenate([zh, sin, zeros], axis=1)
    dn_h = jnp.concatenate([-sin, zh, zeros], axis=1)
    tile2 = lambda a: jnp.concatenate([a, a], axis=1)
    return tile2(cos_h), tile2(up_h), tile2(dn_h)


def _nsa_rope(proj, tabs):
    b, s, _ = proj.shape
    t = 512
    w = BRANCH_W
    q_blk = sum(IN_SIZES[:3]) // w
    kv_blk = sum(IN_SIZES[:4]) // LANE
    tab_spec = pl.BlockSpec((t, LANE), lambda i, j: (j, 0))
    return pl.pallas_call(
        _nsa_rope_kernel,
        grid=(b, s // t),
        in_specs=[pl.BlockSpec((1, t, w), lambda i, j: (i, j, q_blk)),
                  pl.BlockSpec((1, t, LANE), lambda i, j: (i, j, kv_blk + 2)),
                  pl.BlockSpec((1, t, LANE), lambda i, j: (i, j, kv_blk + 4)),
                  tab_spec, tab_spec, tab_spec],
        out_specs=[pl.BlockSpec((1, t, w), lambda i, j: (i, j, 0)),
                   pl.BlockSpec((1, t, LANE), lambda i, j: (i, j, 0)),
                   pl.BlockSpec((1, t, LANE), lambda i, j: (i, j, 0))],
        out_shape=[jax.ShapeDtypeStruct((b, s, w), F32),
                   jax.ShapeDtypeStruct((b, s, LANE), F32),
                   jax.ShapeDtypeStruct((b, s, LANE), F32)],
        compiler_params=_cparams("parallel", "parallel"),
        name="nsa_rope",
    )(proj, proj, proj, *tabs)


def _cmp_kernel(hb_ref, pos_ref, w1_ref, w2_ref, cos_ref, up_ref, dn_ref, o_ref):
    nh = hb_ref.shape[3]
    hw = hb_ref.shape[4]
    hb = hb_ref[0, 0, 0].astype(MM)
    w1 = w1_ref[0]
    lo = _dot(hb, w1[:hw, :])
    hi = _dot(hb, w1[hw:, :])
    posb = _dot(pos_ref[0].astype(MM), w1)[0:1, :]
    hid = jax.nn.gelu(lo + pltpu.roll(hi, nh - 1, 0) + posb)
    out = _dot(hid.astype(MM), w2_ref[0])

    @pl.when(pl.program_id(0) == 0)
    def _():
        o_ref[0, 0, 0] = _rope_lanes(out, cos_ref[...], up_ref[...], dn_ref[...])

    @pl.when(pl.program_id(0) == 1)
    def _():
        o_ref[0, 0, 0] = out


def _nsa_compress(hb, pos, w1, w2, tabs):
    _, b, hkv, nh, hw = hb.shape
    full2 = lambda kd, i, h: (0, 0)
    return pl.pallas_call(
        _cmp_kernel,
        grid=(2, b, hkv),
        in_specs=[pl.BlockSpec((1, 1, 1, nh, hw), lambda kd, i, h: (kd, i, h, 0, 0)),
                  pl.BlockSpec((1, 8, 2 * hw), lambda kd, i, h: (kd, 0, 0)),
                  pl.BlockSpec((1, 2 * hw, CMP_HIDDEN), lambda kd, i, h: (kd, 0, 0)),
                  pl.BlockSpec((1, CMP_HIDDEN, LANE), lambda kd, i, h: (kd, 0, 0)),
                  pl.BlockSpec((nh, LANE), full2),
                  pl.BlockSpec((nh, LANE), full2),
                  pl.BlockSpec((nh, LANE), full2)],
        out_specs=pl.BlockSpec((1, 1, 1, nh, LANE), lambda kd, i, h: (kd, i, h, 0, 0)),
        out_shape=jax.ShapeDtypeStruct((2, b, hkv, nh, LANE), F32),
        compiler_params=_cparams("parallel", "parallel", "parallel"),
        name="nsa_compress",
    )(hb, pos, w1, w2, *tabs)


def _softmax_update(s, vt, m, l, acc):
    m_new = jnp.maximum(m, jnp.max(s, axis=0, keepdims=True))
    alpha = jnp.exp(m - m_new)
    p = jnp.exp(s - m_new)
    l = alpha * l + jnp.sum(p, axis=0, keepdims=True)
    acc = alpha * acc + _dot(vt, p.astype(MM))
    return m_new, l, acc


def _nsa_attn_kernel(q_ref, gt_ref, kc_ref, vc_ref, ovl_ref, ka_ref, vs_ref, kw_ref, vw_ref, o_ref):
    blk = pl.program_id(2)
    qb = Q_BLOCK
    cols = NSA_GQA * qb
    ncmp_pad = kc_ref.shape[2]
    nsel = ovl_ref.shape[0]
    qf = q_ref[0, 0, 0]
    qm = qf.astype(MM)
    t = blk * qb + (lax.broadcasted_iota(jnp.int32, (1, cols), 1) & (qb - 1))

    s_c = _dot(kc_ref[0, 0], qf)
    n_id = lax.broadcasted_iota(jnp.int32, (ncmp_pad, 1), 0)
    m_c = (n_id * CMP_STRIDE + (CMP_LEN - 1) <= t) & (n_id < ncmp_pad - 1)
    smax = jnp.max(jnp.where(m_c, s_c, NEG), axis=0, keepdims=True)
    e_c = jnp.where(m_c, jnp.exp(s_c - smax), 0.0)
    den = jnp.sum(e_c, axis=0, keepdims=True)
    p_c = e_c * (1.0 / jnp.where(den > 0.0, den, 1.0))
    o_c = _dot(vc_ref[0, 0], p_c)[:NSA_HD, :]

    psum = p_c[:, 0:qb] + p_c[:, qb:2 * qb] + p_c[:, 2 * qb:3 * qb] + p_c[:, 3 * qb:4 * qb]
    imp = jnp.dot(ovl_ref[...], psum, preferred_element_type=F32, precision=lax.Precision.HIGHEST)
    cur = (blk * qb + lax.broadcasted_iota(jnp.int32, (1, qb), 1)) // SLC_BLOCK
    jid = lax.broadcasted_iota(jnp.int32, (nsel, qb), 0)
    valid = jid <= cur
    forced = (jid == 0) | (jid == cur) | (jid == cur - 1)
    score = jnp.where(valid, jnp.where(forced, FORCE_SCORE, imp), -1.0)
    sel = jnp.zeros((nsel, qb), F32)
    for _ in range(SLC_TOPK):
        mx = jnp.max(score, axis=0, keepdims=True)
        idx = jnp.min(jnp.where(score == mx, jid, nsel), axis=0, keepdims=True)
        pick = jid == idx
        sel = jnp.where(pick & (mx >= 0.0), 1.0, sel)
        score = jnp.where(pick, -3e38, score)
    bias = jnp.where(sel > 0.0, 0.0, NEG).astype(MM)
    q_aug = jnp.concatenate([jnp.concatenate([bias] * NSA_GQA, axis=1), qm], axis=0)

    kb = SEL_CHUNK
    init = (jnp.full((1, cols), NEG, F32), jnp.zeros((1, cols), F32), jnp.zeros((NSA_HD, cols), F32))

    def sel_body(j, carry):
        off = pl.multiple_of(j * kb, kb)
        s = _dot(ka_ref[0, 0, pl.ds(off, kb), :], q_aug)
        return _softmax_update(s, vs_ref[0, 0, j], *carry)

    nfull = (blk * qb) // kb
    carry = lax.fori_loop(0, nfull, sel_body, init)
    off = pl.multiple_of(nfull * kb, kb)
    s = _dot(ka_ref[0, 0, pl.ds(off, kb), :], q_aug)
    s = jnp.where(off + lax.broadcasted_iota(jnp.int32, (kb, 1), 0) <= t, s, NEG)
    _, l_s, acc_s = _softmax_update(s, vs_ref[0, 0, nfull], *carry)
    o_s = acc_s * (1.0 / l_s)

    kpos0 = lax.broadcasted_iota(jnp.int32, (qb, 1), 0)
    s_w = []
    v_w = []
    for back in range(WIN // qb, -1, -1):
        c = blk - back
        cc = jnp.maximum(c, 0)
        off = pl.multiple_of(cc * qb, qb)
        wpos = c * qb + kpos0
        s = _dot(kw_ref[0, 0, pl.ds(off, qb), :], qm)
        s_w.append(jnp.where((wpos <= t) & (wpos > t - WIN) & (wpos >= 0), s, NEG))
        v_w.append(vw_ref[0, 0, cc])
    s_w = jnp.concatenate(s_w, axis=0)
    p_w = jnp.exp(s_w - jnp.max(s_w, axis=0, keepdims=True))
    l_w = jnp.sum(p_w, axis=0, keepdims=True)
    o_w = _dot(jnp.concatenate(v_w, axis=1), p_w.astype(MM)) * (1.0 / l_w)

    g = jax.nn.sigmoid(gt_ref[0, 0, 0])
    o_ref[0, 0, 0] = g[0:1, :] * o_c + g[1:2, :] * o_s + g[2:3, :] * o_w


def _nsa_attn(qt, gt, kc, vct, ovlt, ka, vst, kw, vwt):
    b, hkv, nqb, _, cols = qt.shape
    s = ka.shape[2]
    ncmp_pad = kc.shape[2]
    nkc = vst.shape[2]
    per_head = lambda i, h, j: (i, h, 0, 0)
    per_head5 = lambda i, h, j: (i, h, 0, 0, 0)
    per_blk = lambda i, h, j: (i, h, j, 0, 0)
    return pl.pallas_call(
        _nsa_attn_kernel,
        grid=(b, hkv, nqb),
        in_specs=[pl.BlockSpec((1, 1, 1, LANE, cols), per_blk),
                  pl.BlockSpec((1, 1, 1, 8, cols), per_blk),
                  pl.BlockSpec((1, 1, ncmp_pad, LANE), per_head),
                  pl.BlockSpec((1, 1, LANE, ncmp_pad), per_head),
                  pl.BlockSpec(ovlt.shape, lambda i, h, j: (0, 0)),
                  pl.BlockSpec((1, 1, s, 2 * LANE), per_head),
                  pl.BlockSpec((1, 1, s // SEL_CHUNK, NSA_HD, SEL_CHUNK), per_head5),
                  pl.BlockSpec((1, 1, s, LANE), per_head),
                  pl.BlockSpec((1, 1, s // Q_BLOCK, NSA_HD, Q_BLOCK), per_head5)],
        out_specs=pl.BlockSpec((1, 1, 1, NSA_HD, cols), per_blk),
        out_shape=jax.ShapeDtypeStruct((b, hkv, nqb, NSA_HD, cols), F32),
        compiler_params=_cparams("parallel", "parallel", "arbitrary"),
        name="nsa_attn",
    )(qt, gt, kc, vct, ovlt, ka, vst, kw, vwt)


def _nsa(proj, pos_k, pos_v, ck_w1, ck_w2, cv_w1, cv_w2, tabs):
    b, s, _ = proj.shape
    hkv, g, dh, qb = NSA_KV_HEADS, NSA_GQA, NSA_HD, Q_BLOCK
    nqb = s // qb
    nhalf = s // CMP_STRIDE
    ncmp = (s - CMP_LEN) // CMP_STRIDE + 1
    nsel = s // SLC_BLOCK
    tok_tabs, cmp_tabs, overlap = tabs

    q_r, ks_r, kw_r = _nsa_rope(proj, tok_tabs)

    kv0 = sum(IN_SIZES[:4])
    kv = proj[:, :, kv0:kv0 + NSA_KV_W].reshape(b, s, 6, hkv, dh)
    heads = lambda a: jnp.transpose(a.reshape(b, s, hkv, dh), (0, 2, 1, 3))
    hb = jnp.stack([heads(kv[:, :, 0]), heads(kv[:, :, 1])]).reshape(2, b, hkv, nhalf, CMP_STRIDE * dh)
    pos = jnp.stack([pos_k, pos_v]).reshape(2, 1, CMP_LEN * dh)
    pos = jnp.broadcast_to(pos, (2, 8, CMP_LEN * dh))
    w1 = jnp.stack([ck_w1, cv_w1]).astype(MM)
    w2 = jnp.pad(jnp.stack([ck_w2, cv_w2]), ((0, 0), (0, 0), (0, LANE - dh))).astype(MM)
    cmp = _nsa_compress(hb, pos, w1, w2, cmp_tabs)
    kc, vc = cmp[0], cmp[1]

    pad_lanes = lambda a, n: jnp.pad(a, [(0, 0)] * (a.ndim - 1) + [(0, n)])
    pad_rows = lambda a, n: jnp.pad(a, [(0, 0)] * (a.ndim - 2) + [(0, n), (0, 0)])
    qt = q_r.reshape(b, nqb, qb, hkv, g, dh).transpose(0, 3, 1, 5, 4, 2).reshape(b, hkv, nqb, dh, g * qb)
    qt = pad_rows(qt, LANE - dh)
    bg = proj[:, :, BG_OFF:BG_OFF + 3 * NSA_HEADS].reshape(b, nqb, qb, 3, hkv, g)
    gt = pad_rows(bg.transpose(0, 4, 1, 3, 5, 2).reshape(b, hkv, nqb, 3, g * qb), 5)
    assert nsel <= LANE
    onehot = (jnp.arange(s)[:, None] // SLC_BLOCK == jnp.arange(LANE)[None, :]).astype(MM)
    ks_h = heads(ks_r).astype(MM)
    ka = jnp.concatenate([jnp.broadcast_to(onehot, (b, hkv, s, LANE)), ks_h,
                          jnp.zeros((b, hkv, s, LANE - dh), MM)], axis=-1)
    assert ka.shape[-1] == 2 * LANE
    chunks_t = lambda a, n: a.reshape(b, hkv, s // n, n, dh).transpose(0, 1, 2, 4, 3).astype(MM)
    vst = chunks_t(heads(kv[:, :, 3]), SEL_CHUNK)
    kw = pad_lanes(heads(kw_r), LANE - dh).astype(MM)
    vwt = chunks_t(heads(kv[:, :, 5]), qb)
    o = _nsa_attn(qt, gt, kc, jnp.swapaxes(vc, 2, 3), overlap.T, ka, vst, kw, vwt)
    return o.reshape(b, hkv, nqb, dh, g, qb).transpose(0, 2, 5, 1, 4, 3).reshape(b, s, hkv * g * dh)


def _nsa_tables(s):
    ncmp = (s - CMP_LEN) // CMP_STRIDE + 1
    nhalf = s // CMP_STRIDE
    nsel = s // SLC_BLOCK
    tok_tabs = _nsa_rope_tables(jnp.arange(s))
    cmp_tabs = _nsa_rope_tables(jnp.arange(nhalf) * CMP_STRIDE + CMP_LEN - 1)
    cmp_tok_blk = (jnp.arange(ncmp)[:, None] * CMP_STRIDE + jnp.arange(CMP_LEN)[None, :]) // SLC_BLOCK
    overlap = jnp.mean((cmp_tok_blk[..., None] == jnp.arange(nsel)).astype(F32), axis=1)
    overlap = jnp.pad(overlap, ((0, nhalf - ncmp), (0, LANE - nsel)))
    return tok_tabs, cmp_tabs, overlap


def _out_kernel(x_ref, g_ref, yl_ref, ys_ref, yr_ref, yn_ref, wg_ref, wm_ref, wb_ref, wo_ref,
                fg_ref, o_ref, *, final):
    x = x_ref[0]
    h = _rms(x, g_ref[...]).astype(MM)
    w = BRANCH_W
    d = D_MODEL
    merged = jnp.zeros(x.shape, F32)
    for n, y_ref in enumerate((yl_ref, ys_ref, yr_ref, yn_ref)):
        gate = jax.nn.silu(_dot(h, wg_ref[:, n * w:(n + 1) * w]))
        branch = _dot((y_ref[0] * gate).astype(MM), wb_ref[n])
        merged = merged + jax.nn.sigmoid(_dot(h, wm_ref[:, n * d:(n + 1) * d])) * branch
    out = x + _dot(merged.astype(MM), wo_ref[...])
    if final:
        out = _rms(out, fg_ref[...])
    o_ref[0] = out


def _out(x, g, ys, wg, wm, wb, wo, fg, final):
    b, s, d = x.shape
    w = BRANCH_W
    tm = TM_PROJ
    row = lambda i, j: (i, j, 0)
    c2 = lambda i, j: (0, 0)
    once = pl.Buffered(1)
    return pl.pallas_call(
        functools.partial(_out_kernel, final=final),
        grid=(b, s // tm),
        in_specs=[pl.BlockSpec((1, tm, d), row),
                  pl.BlockSpec((1, d), c2)]
                 + [pl.BlockSpec((1, tm, w), row)] * N_BRANCH
                 + [pl.BlockSpec((d, N_BRANCH * w), c2, pipeline_mode=once),
                    pl.BlockSpec((d, N_BRANCH * d), c2, pipeline_mode=once),
                    pl.BlockSpec((N_BRANCH, w, d), lambda i, j: (0, 0, 0), pipeline_mode=once),
                    pl.BlockSpec((d, d), c2, pipeline_mode=once),
                    pl.BlockSpec((1, d), c2)],
        out_specs=pl.BlockSpec((1, tm, d), row),
        out_shape=jax.ShapeDtypeStruct((b, s, d), F32),
        compiler_params=_cparams("parallel", "arbitrary"),
        name="gate_merge_out",
    )(x, g, *ys, wg, wm, wb, wo, fg)


def _layer(x, p, l, ret_tabs, nsa_tabs, final_g):
    w_in = p["w_in"][l]
    w_mix = jnp.concatenate([w_in[:, :MIX_W], w_in[:, BG_OFF:GATE_OFF],
                             jnp.zeros((D_MODEL, LANE - IN_SIZES[5]), F32)], axis=1).astype(MM)
    g = p["norm_g"][l].reshape(1, D_MODEL)
    proj = _inproj(x, g, w_mix)

    wab = jnp.concatenate([_block_diag(p["lru_wa"][l]), _block_diag(p["lru_wx"][l])], axis=1).astype(MM)
    bab = jnp.concatenate([p["lru_ba"][l], p["lru_bx"][l]]).reshape(1, 2 * BRANCH_W)
    y_lru = _lru(proj, p["lru_conv_w"][l], p["lru_conv_b"][l].reshape(1, -1), wab, bab,
                 p["lru_lambda"][l].reshape(1, -1))

    wb, apr, api, wc = _s5_tables(p["s5_lambda_re"][l], p["s5_lambda_im"][l], p["s5_b_re"][l],
                                  p["s5_b_im"][l], p["s5_c_re"][l], p["s5_c_im"][l], p["s5_log_dt"][l])
    y_s5 = _s5(proj, wb, apr, api, wc, p["s5_d"][l].reshape(1, -1), p["s5_w_glu"][l].astype(MM),
               p["s5_b_glu"][l].reshape(1, -1))

    y_ret = _retention(proj, ret_tabs)

    y_nsa = _nsa(proj, p["nsa_pos_k"][l], p["nsa_pos_v"][l], p["nsa_ck_w1"][l], p["nsa_ck_w2"][l],
                 p["nsa_cv_w1"][l], p["nsa_cv_w2"][l], nsa_tabs)

    wg = w_in[:, GATE_OFF:MERGE_OFF].astype(MM)
    wm = w_in[:, MERGE_OFF:].astype(MM)
    final = l == DEPTH - 1
    return _out(x, g, (y_lru, y_s5, y_ret, y_nsa), wg, wm, p["w_branch"][l].astype(MM),
                p["w_out"][l].astype(MM), final_g.reshape(1, D_MODEL), final)


def kernel(x, norm_g, w_in, lru_conv_w, lru_conv_b, lru_wa, lru_ba, lru_wx, lru_bx, lru_lambda,
           s5_lambda_re, s5_lambda_im, s5_b_re, s5_b_im, s5_c_re, s5_c_im, s5_log_dt, s5_d,
           s5_w_glu, s5_b_glu, nsa_pos_k, nsa_pos_v, nsa_ck_w1, nsa_ck_w2, nsa_cv_w1, nsa_cv_w2,
           w_branch, w_out, final_norm_g):
    p = dict(norm_g=norm_g, w_in=w_in, lru_conv_w=lru_conv_w, lru_conv_b=lru_conv_b, lru_wa=lru_wa,
             lru_ba=lru_ba, lru_wx=lru_wx, lru_bx=lru_bx, lru_lambda=lru_lambda,
             s5_lambda_re=s5_lambda_re, s5_lambda_im=s5_lambda_im, s5_b_re=s5_b_re, s5_b_im=s5_b_im,
             s5_c_re=s5_c_re, s5_c_im=s5_c_im, s5_log_dt=s5_log_dt, s5_d=s5_d, s5_w_glu=s5_w_glu,
             s5_b_glu=s5_b_glu, nsa_pos_k=nsa_pos_k, nsa_pos_v=nsa_pos_v, nsa_ck_w1=nsa_ck_w1,
             nsa_ck_w2=nsa_ck_w2, nsa_cv_w1=nsa_cv_w1, nsa_cv_w2=nsa_cv_w2, w_branch=w_branch,
             w_out=w_out)
    s = x.shape[1]
    ret_tabs = _ret_tables(s)
    nsa_tabs = _nsa_tables(s)
    for l in range(DEPTH):
        x = _layer(x, p, l, ret_tabs, nsa_tabs, final_norm_g)
    return x
```

```python
import functools
import math

import jax
import jax.numpy as jnp
from jax import lax
from jax.experimental import pallas as pl
from jax.experimental.pallas import tpu as pltpu

F32 = jnp.float32
MM = jnp.bfloat16

D_MODEL = 1024
DEPTH = 4
N_BRANCH = 4
BRANCH_W = 512
NORM_EPS = 1e-6
LRU_HEADS = 8
LRU_HD = BRANCH_W // LRU_HEADS
CONV_W = 4
LRU_C = 8.0
S5_GROUP = 16
S5_GROUPS = BRANCH_W // S5_GROUP
S5_STATE = 64
S5_N = S5_GROUPS * S5_STATE
RET_HEADS = 4
RET_HD = BRANCH_W // RET_HEADS
RET_CHUNK = 128
RET_ROPE_BASE = 10000.0
NSA_HEADS = 8
NSA_KV_HEADS = 2
NSA_HD = BRANCH_W // NSA_HEADS
NSA_GQA = NSA_HEADS // NSA_KV_HEADS
CMP_LEN = 32
CMP_STRIDE = 16
CMP_HIDDEN = 256
SLC_BLOCK = 64
SLC_TOPK = 16
WIN = 512
Q_BLOCK = 128
ROPE_THETA = 500000.0
ROPE_DIM = NSA_HD // 4
FORCE_SCORE = 1e4
NEG = -1e30
NSA_KV_W = 6 * NSA_KV_HEADS * NSA_HD
IN_SIZES = (BRANCH_W, BRANCH_W, 3 * BRANCH_W, NSA_HEADS * NSA_HD, NSA_KV_W, 3 * NSA_HEADS,
            N_BRANCH * BRANCH_W, N_BRANCH * D_MODEL)
LANE = 128
MIX_W = sum(IN_SIZES[:5])
BG_OFF = MIX_W
GATE_OFF = BG_OFF + IN_SIZES[5]
MERGE_OFF = GATE_OFF + IN_SIZES[6]
PROJ_W = MIX_W + LANE
VMEM_LIMIT = 56 * 1024 * 1024

TM_PROJ = 256
T_LRU = 256
T_S5 = 256
T_RET = 512
SEL_CHUNK = 512
V_ROWS = NSA_HD + 16


def _cparams(*sem):
    return pltpu.CompilerParams(dimension_semantics=sem, vmem_limit_bytes=VMEM_LIMIT)


def _rms(x, g):
    ms = jnp.mean(x * x, axis=-1, keepdims=True)
    return (x * lax.rsqrt(ms + NORM_EPS)) * g


def _dot(a, b):
    return jnp.dot(a, b, preferred_element_type=F32)


def _dot_nt(a, b):
    return lax.dot_general(a, b, (((1,), (1,)), ((), ())), preferred_element_type=F32)


def _dot_tn(a, b):
    return lax.dot_general(a, b, (((0,), (0,)), ((), ())), preferred_element_type=F32)


def _inproj_kernel(x_ref, g_ref, w_ref, o_ref):
    h = _rms(x_ref[0], g_ref[...])
    o_ref[0] = _dot(h.astype(MM), w_ref[...])


def _inproj(x, g, w):
    b, s, d = x.shape
    n = w.shape[1]
    return pl.pallas_call(
        _inproj_kernel,
        grid=(b, s // TM_PROJ),
        in_specs=[pl.BlockSpec((1, TM_PROJ, d), lambda i, j: (i, j, 0)),
                  pl.BlockSpec((1, d), lambda i, j: (0, 0)),
                  pl.BlockSpec((d, n), lambda i, j: (0, 0))],
        out_specs=pl.BlockSpec((1, TM_PROJ, n), lambda i, j: (i, j, 0)),
        out_shape=jax.ShapeDtypeStruct((b, s, n), F32),
        compiler_params=_cparams("parallel", "arbitrary"),
        name="inproj",
    )(x, g, w)


def _lru_kernel(u_ref, cw_ref, cb_ref, wab_ref, bab_ref, lam_ref, o_ref, ubuf, hcar):
    t = T_LRU
    w = BRANCH_W

    @pl.when(pl.program_id(1) == 0)
    def _():
        ubuf[0:8, :] = jnp.zeros((8, w), F32)
        hcar[...] = jnp.zeros((1, w), F32)

    ubuf[8:8 + t, :] = u_ref[0]
    xc = cb_ref[...] + cw_ref[0:1, :] * ubuf[5:5 + t, :]
    for k in range(1, CONV_W):
        xc = xc + cw_ref[k:k + 1, :] * ubuf[5 + k:5 + k + t, :]
    ubuf[0:8, :] = ubuf[t:t + 8, :]

    ri = _dot(xc.astype(MM), wab_ref[...]) + bab_ref[...]
    r = jax.nn.sigmoid(ri[:, :w])
    gi = jax.nn.sigmoid(ri[:, w:])
    z = -lam_ref[...]
    softplus = jnp.maximum(z, 0.0) + jnp.log1p(jnp.exp(-jnp.abs(z)))
    a = jnp.exp(-LRU_C * r * softplus)
    bt = jnp.sqrt(1.0 - a * a) * gi * xc

    row = lax.broadcasted_iota(jnp.int32, (t, w), 0)
    k = 1
    while k < t:
        keep = row >= k
        a_sh = jnp.where(keep, pltpu.roll(a, k, 0), 1.0)
        b_sh = jnp.where(keep, pltpu.roll(bt, k, 0), 0.0)
        bt = a * b_sh + bt
        a = a * a_sh
        k *= 2
    h = bt + a * hcar[...]
    o_ref[0] = h
    hcar[...] = h[t - 1:t, :]


def _lru(proj, cw, cb, wab, bab, lam):
    b, s, _ = proj.shape
    w = BRANCH_W
    return pl.pallas_call(
        _lru_kernel,
        grid=(b, s // T_LRU),
        in_specs=[pl.BlockSpec((1, T_LRU, w), lambda i, j: (i, j, 0)),
                  pl.BlockSpec((CONV_W, w), lambda i, j: (0, 0)),
                  pl.BlockSpec((1, w), lambda i, j: (0, 0)),
                  pl.BlockSpec((w, 2 * w), lambda i, j: (0, 0)),
                  pl.BlockSpec((1, 2 * w), lambda i, j: (0, 0)),
                  pl.BlockSpec((1, w), lambda i, j: (0, 0))],
        out_specs=pl.BlockSpec((1, T_LRU, w), lambda i, j: (i, j, 0)),
        out_shape=jax.ShapeDtypeStruct((b, s, w), F32),
        scratch_shapes=[pltpu.VMEM((T_LRU + 8, w), F32), pltpu.VMEM((1, w), F32)],
        compiler_params=_cparams("parallel", "arbitrary"),
        name="rg_lru",
    )(proj, cw, cb, wab, bab, lam)


def _block_diag(blocks):
    n, r, c = blocks.shape
    eye = jnp.eye(n, dtype=blocks.dtype)
    return jnp.einsum("nrc,nm->nrmc", blocks, eye).reshape(n * r, n * c)


def _s5_kernel(u_ref, wb_ref, apr_ref, api_ref, wc_ref, d_ref, wg_ref, bg_ref, o_ref, car):
    t = T_S5
    n = S5_N
    w = BRANCH_W

    @pl.when(pl.program_id(1) == 0)
    def _():
        car[...] = jnp.zeros((2, n), F32)

    u = u_ref[0]
    x = _dot(u.astype(MM), wb_ref[...])
    xr = x[:, :n]
    xi = x[:, n:]
    row = lax.broadcasted_iota(jnp.int32, (t, n), 0)
    ar = apr_ref[0:1, :]
    ai = api_ref[0:1, :]
    cr = car[0:1, :]
    ci = car[1:2, :]
    first = row == 0
    xr = xr + jnp.where(first, ar * cr - ai * ci, 0.0)
    xi = xi + jnp.where(first, ar * ci + ai * cr, 0.0)
    k = 1
    step = 0
    while k < t:
        ar = apr_ref[step:step + 1, :]
        ai = api_ref[step:step + 1, :]
        keep = row >= k
        xr_s = jnp.where(keep, pltpu.roll(xr, k, 0), 0.0)
        xi_s = jnp.where(keep, pltpu.roll(xi, k, 0), 0.0)
        xr, xi = xr + (ar * xr_s - ai * xi_s), xi + (ar * xi_s + ai * xr_s)
        k *= 2
        step += 1
    car[0:1, :] = xr[t - 1:t, :]
    car[1:2, :] = xi[t - 1:t, :]
    y = _dot(xr.astype(MM), wc_ref[0:n, :]) + _dot(xi.astype(MM), wc_ref[n:2 * n, :])
    y = y + d_ref[...] * u
    z = jax.nn.gelu(y)
    g = _dot(z.astype(MM), wg_ref[...]) + bg_ref[...]
    o_ref[0] = g[:, :w] * jax.nn.sigmoid(g[:, w:])


def _s5_tables(lam_re, lam_im, b_re, b_im, c_re, c_im, log_dt):
    dt = jnp.exp(log_dt)[:, None]
    lr = jnp.minimum(lam_re, -1e-4)
    li = lam_im
    mag = jnp.exp(lr * dt)
    ab_re = mag * jnp.cos(li * dt)
    ab_im = mag * jnp.sin(li * dt)
    den = lr * lr + li * li
    nr = ab_re - 1.0
    ni = ab_im
    f_re = (nr * lr + ni * li) / den
    f_im = (ni * lr - nr * li) / den
    bb_re = f_re[..., None] * b_re - f_im[..., None] * b_im
    bb_im = f_re[..., None] * b_im + f_im[..., None] * b_re
    wb = jnp.concatenate([_block_diag(jnp.swapaxes(bb_re, 1, 2)),
                          _block_diag(jnp.swapaxes(bb_im, 1, 2))], axis=1)
    wc = jnp.concatenate([_block_diag(jnp.swapaxes(c_re, 1, 2)),
                          _block_diag(jnp.swapaxes(-c_im, 1, 2))], axis=0)
    pr = [ab_re.reshape(1, S5_N)]
    pi = [ab_im.reshape(1, S5_N)]
    for _ in range(int(math.log2(T_S5)) - 1):
        r0, i0 = pr[-1], pi[-1]
        pr.append(r0 * r0 - i0 * i0)
        pi.append(2.0 * r0 * i0)
    return wb.astype(MM), jnp.concatenate(pr, 0), jnp.concatenate(pi, 0), wc.astype(MM)


def _s5(proj, wb, apr, api, wc, d, wg, bg):
    b, s, _ = proj.shape
    w = BRANCH_W
    n = S5_N
    ns = apr.shape[0]
    return pl.pallas_call(
        _s5_kernel,
        grid=(b, s // T_S5),
        in_specs=[pl.BlockSpec((1, T_S5, w), lambda i, j: (i, j, 1)),
                  pl.BlockSpec((w, 2 * n), lambda i, j: (0, 0)),
                  pl.BlockSpec((ns, n), lambda i, j: (0, 0)),
                  pl.BlockSpec((ns, n), lambda i, j: (0, 0)),
                  pl.BlockSpec((2 * n, w), lambda i, j: (0, 0)),
                  pl.BlockSpec((1, w), lambda i, j: (0, 0)),
                  pl.BlockSpec((w, 2 * w), lambda i, j: (0, 0)),
                  pl.BlockSpec((1, 2 * w), lambda i, j: (0, 0))],
        out_specs=pl.BlockSpec((1, T_S5, w), lambda i, j: (i, j, 0)),
        out_shape=jax.ShapeDtypeStruct((b, s, w), F32),
        scratch_shapes=[pltpu.VMEM((2, n), F32)],
        compiler_params=_cparams("parallel", "arbitrary"),
        name="s5",
    )(proj, wb, apr, api, wc, d, wg, bg)


def _ret_kernel(q_ref, k_ref, v_ref, cos_ref, sin_ref, dm_ref, xi_ref, zt_ref, cd_ref, o_ref, rst):
    c = RET_CHUNK
    dh = RET_HD

    @pl.when(pl.program_id(1) == 0)
    def _():
        rst[...] = jnp.zeros((RET_HEADS, dh, dh), F32)

    for ci in range(T_RET // c):
        rows = slice(ci * c, (ci + 1) * c)
        cos = cos_ref[rows, :]
        sin = sin_ref[rows, :]
        for h in range(RET_HEADS):
            cols = slice(h * dh, (h + 1) * dh)
            q = q_ref[0, rows, cols]
            k = k_ref[0, rows, cols]
            v = v_ref[0, rows, cols]
            q = q * cos + pltpu.roll(q, dh // 2, 1) * sin
            k = (k * cos + pltpu.roll(k, dh // 2, 1) * sin) * (dh ** -0.5)
            qb = q.astype(MM)
            kb = k.astype(MM)
            vb = v.astype(MM)
            scores = _dot_nt(qb, kb) * dm_ref[h]
            inner = _dot(scores.astype(MM), vb)
            r_prev = rst[h]
            cross = _dot((q * xi_ref[h]).astype(MM), r_prev.astype(MM))
            kv = _dot_tn((k * zt_ref[h]).astype(MM), vb)
            rst[h] = cd_ref[h] * r_prev + kv
            o = inner + cross
            mu = jnp.mean(o, axis=-1, keepdims=True)
            var = jnp.mean(jnp.square(o - mu), axis=-1, keepdims=True)
            o_ref[0, rows, cols] = (o - mu) * lax.rsqrt(var + 1e-5)


def _ret_tables(s):
    dh = RET_HD
    c = RET_CHUNK
    half = dh // 2
    inv = RET_ROPE_BASE ** (-jnp.arange(half, dtype=F32) * 2.0 / dh)
    ang = jnp.arange(s).astype(F32)[:, None] * inv[None, :]
    cos = jnp.cos(ang)
    sin = jnp.sin(ang)
    cosf = jnp.concatenate([cos, cos], axis=1)
    sinf = jnp.concatenate([-sin, sin], axis=1)
    gamma = 1.0 - jnp.exp(jnp.linspace(math.log(1.0 / 32.0), math.log(1.0 / 512.0), RET_HEADS, dtype=F32))
    log_g = jnp.log(gamma)
    n = jnp.arange(c, dtype=F32)
    diff = n[:, None] - n[None, :]
    dmask = jnp.where(diff[None] >= 0, jnp.exp(jnp.maximum(diff, 0.0)[None] * log_g[:, None, None]), 0.0)
    xi = jnp.exp((n[None, :] + 1.0) * log_g[:, None])
    zeta = jnp.exp((c - 1.0 - n)[None, :] * log_g[:, None])
    cdec = jnp.exp(c * log_g)
    xi_b = jnp.broadcast_to(xi[:, :, None], (RET_HEADS, c, dh))
    zt_b = jnp.broadcast_to(zeta[:, :, None], (RET_HEADS, c, dh))
    cd_b = jnp.broadcast_to(cdec[:, None, None], (RET_HEADS, dh, dh))
    return cosf, sinf, dmask, xi_b, zt_b, cd_b


def _retention(proj, tabs):
    b, s, _ = proj.shape
    w = BRANCH_W
    cosf, sinf, dmask, xi_b, zt_b, cd_b = tabs
    c = RET_CHUNK
    dh = RET_HD
    full3 = lambda i, j: (0, 0, 0)
    return pl.pallas_call(
        _ret_kernel,
        grid=(b, s // T_RET),
        in_specs=[pl.BlockSpec((1, T_RET, w), lambda i, j: (i, j, 2)),
                  pl.BlockSpec((1, T_RET, w), lambda i, j: (i, j, 3)),
                  pl.BlockSpec((1, T_RET, w), lambda i, j: (i, j, 4)),
                  pl.BlockSpec((T_RET, dh), lambda i, j: (j, 0)),
                  pl.BlockSpec((T_RET, dh), lambda i, j: (j, 0)),
                  pl.BlockSpec((RET_HEADS, c, c), full3),
                  pl.BlockSpec((RET_HEADS, c, dh), full3),
                  pl.BlockSpec((RET_HEADS, c, dh), full3),
                  pl.BlockSpec((RET_HEADS, dh, dh), full3)],
        out_specs=pl.BlockSpec((1, T_RET, w), lambda i, j: (i, j, 0)),
        out_shape=jax.ShapeDtypeStruct((b, s, w), F32),
        scratch_shapes=[pltpu.VMEM((RET_HEADS, dh, dh), F32)],
        compiler_params=_cparams("parallel", "arbitrary"),
        name="retention",
    )(proj, proj, proj, cosf, sinf, dmask, xi_b, zt_b, cd_b)


def _rope_lanes(x, cos, s_up, s_dn):
    half = ROPE_DIM // 2
    n = x.shape[-1]
    return x * cos + pltpu.roll(x, half, 1) * s_up + pltpu.roll(x, n - half, 1) * s_dn


def _nsa_rope_kernel(q_ref, ks_ref, kw_ref, cos_ref, up_ref, dn_ref, qo_ref, kso_ref, kwo_ref):
    cos = cos_ref[...]
    up = up_ref[...]
    dn = dn_ref[...]
    scale = NSA_HD ** -0.5 * math.log2(math.e)
    for cb in range(BRANCH_W // LANE):
        cols = slice(cb * LANE, (cb + 1) * LANE)
        qo_ref[0, :, cols] = _rope_lanes(q_ref[0, :, cols], cos, up, dn) * scale
    kso_ref[0] = _rope_lanes(ks_ref[0], cos, up, dn)
    kwo_ref[0] = _rope_lanes(kw_ref[0], cos, up, dn)


def _nsa_rope_tables(pos):
    half = ROPE_DIM // 2
    inv = ROPE_THETA ** (-jnp.arange(half, dtype=F32) * 2.0 / ROPE_DIM)
    ang = pos.astype(F32)[:, None] * inv[None, :]
    cos = jnp.cos(ang)
    sin = jnp.sin(ang)
    n = pos.shape[0]
    zeros = jnp.zeros((n, NSA_HD - ROPE_DIM), F32)
    zh = jnp.zeros((n, half), F32)
    cos_h = jnp.concatenate([cos, cos, jnp.ones((n, NSA_HD - ROPE_DIM), F32)], axis=1)
    up_h = jnp.concatenate([zh, sin, zeros], axis=1)
    dn_h = jnp.concatenate([-sin, zh, zeros], axis=1)
    tile2 = lambda a: jnp.concatenate([a, a], axis=1)
    return tile2(cos_h), tile2(up_h), tile2(dn_h)


def _nsa_rope(proj, tabs):
    b, s, _ = proj.shape
    t = 512
    w = BRANCH_W
    q_blk = sum(IN_SIZES[:3]) // w
    kv_blk = sum(IN_SIZES[:4]) // LANE
    tab_spec = pl.BlockSpec((t, LANE), lambda i, j: (j, 0))
    return pl.pallas_call(
        _nsa_rope_kernel,
        grid=(b, s // t),
        in_specs=[pl.BlockSpec((1, t, w), lambda i, j: (i, j, q_blk)),
                  pl.BlockSpec((1, t, LANE), lambda i, j: (i, j, kv_blk + 2)),
                  pl.BlockSpec((1, t, LANE), lambda i, j: (i, j, kv_blk + 4)),
                  tab_spec, tab_spec, tab_spec],
        out_specs=[pl.BlockSpec((1, t, w), lambda i, j: (i, j, 0)),
                   pl.BlockSpec((1, t, LANE), lambda i, j: (i, j, 0)),
                   pl.BlockSpec((1, t, LANE), lambda i, j: (i, j, 0))],
        out_shape=[jax.ShapeDtypeStruct((b, s, w), F32),
                   jax.ShapeDtypeStruct((b, s, LANE), F32),
                   jax.ShapeDtypeStruct((b, s, LANE), F32)],
        compiler_params=_cparams("parallel", "parallel"),
        name="nsa_rope",
    )(proj, proj, proj, *tabs)


def _cmp_kernel(hb_ref, pos_ref, w1_ref, w2_ref, cos_ref, up_ref, dn_ref, o_ref):
    nh = hb_ref.shape[3]
    hw = hb_ref.shape[4]
    hb = hb_ref[0, 0, 0].astype(MM)
    w1 = w1_ref[0]
    lo = _dot(hb, w1[:hw, :])
    hi = _dot(hb, w1[hw:, :])
    posb = _dot(pos_ref[0].astype(MM), w1)[0:1, :]
    hid = jax.nn.gelu(lo + pltpu.roll(hi, nh - 1, 0) + posb)
    out = _dot(hid.astype(MM), w2_ref[0])

    @pl.when(pl.program_id(0) == 0)
    def _():
        o_ref[0, 0, 0] = _rope_lanes(out, cos_ref[...], up_ref[...], dn_ref[...])

    @pl.when(pl.program_id(0) == 1)
    def _():
        o_ref[0, 0, 0] = out


def _nsa_compress(hb, pos, w1, w2, tabs):
    _, b, hkv, nh, hw = hb.shape
    full2 = lambda kd, i, h: (0, 0)
    return pl.pallas_call(
        _cmp_kernel,
        grid=(2, b, hkv),
        in_specs=[pl.BlockSpec((1, 1, 1, nh, hw), lambda kd, i, h: (kd, i, h, 0, 0)),
                  pl.BlockSpec((1, 8, 2 * hw), lambda kd, i, h: (kd, 0, 0)),
                  pl.BlockSpec((1, 2 * hw, CMP_HIDDEN), lambda kd, i, h: (kd, 0, 0)),
                  pl.BlockSpec((1, CMP_HIDDEN, LANE), lambda kd, i, h: (kd, 0, 0)),
                  pl.BlockSpec((nh, LANE), full2),
                  pl.BlockSpec((nh, LANE), full2),
                  pl.BlockSpec((nh, LANE), full2)],
        out_specs=pl.BlockSpec((1, 1, 1, nh, LANE), lambda kd, i, h: (kd, i, h, 0, 0)),
        out_shape=jax.ShapeDtypeStruct((2, b, hkv, nh, LANE), F32),
        compiler_params=_cparams("parallel", "parallel", "parallel"),
        name="nsa_compress",
    )(hb, pos, w1, w2, *tabs)


def _softmax_update(s, vt, m, acc):
    m_new = jnp.maximum(m, jnp.max(s, axis=0, keepdims=True))
    alpha = jnp.exp2(m - m_new)
    p = jnp.exp2(s - m_new)
    acc = alpha * acc + _dot(vt, p.astype(MM))
    return m_new, acc


def _normalise(acc):
    return acc[:NSA_HD, :] * (1.0 / acc[NSA_HD:NSA_HD + 1, :])


def _nsa_attn_kernel(q_ref, gt_ref, kc_ref, vc_ref, ka_ref, vs_ref, kw_ref, vw_ref, o_ref, ps_ref):
    blk = pl.program_id(2)
    qb = Q_BLOCK
    cols = NSA_GQA * qb
    ncmp_pad = kc_ref.shape[2]
    nsel = LANE
    qf = q_ref[0, 0, 0]
    qm = qf.astype(MM)
    t = blk * qb + (lax.broadcasted_iota(jnp.int32, (1, cols), 1) & (qb - 1))

    s_c = _dot(kc_ref[0, 0], qf)
    n_id = lax.broadcasted_iota(jnp.int32, (ncmp_pad, 1), 0)
    m_c = (n_id * CMP_STRIDE + (CMP_LEN - 1) <= t) & (n_id < ncmp_pad - 1)
    smax = jnp.max(jnp.where(m_c, s_c, NEG), axis=0, keepdims=True)
    e_c = jnp.where(m_c, jnp.exp2(s_c - smax), 0.0)
    den = jnp.sum(e_c, axis=0, keepdims=True)
    p_c = e_c * (1.0 / jnp.where(den > 0.0, den, 1.0))
    o_c = _dot(vc_ref[0, 0], p_c.astype(MM))[:NSA_HD, :]

    r = SLC_BLOCK // CMP_STRIDE
    nrow = ncmp_pad // r
    ps_ref[...] = p_c[:, 0:qb] + p_c[:, qb:2 * qb] + p_c[:, 2 * qb:3 * qb] + p_c[:, 3 * qb:4 * qb]
    imp = ps_ref[pl.ds(0, nrow, stride=r), :]
    for k in range(1, r - 1):
        imp = imp + ps_ref[pl.ds(k, nrow, stride=r), :]
    edge = 0.5 * ps_ref[pl.ds(r - 1, nrow, stride=r), :]
    first = lax.broadcasted_iota(jnp.int32, (nrow, qb), 0) == 0
    imp = imp + edge + jnp.where(first, 0.0, pltpu.roll(edge, 1, 0))
    if nrow < nsel:
        imp = jnp.concatenate([imp, jnp.zeros((nsel - nrow, qb), F32)], axis=0)

    cur = (blk * qb + lax.broadcasted_iota(jnp.int32, (1, qb), 1)) // SLC_BLOCK
    jid = lax.broadcasted_iota(jnp.int32, (nsel, qb), 0)
    valid = jid <= cur
    forced = (jid == 0) | (jid == cur) | (jid == cur - 1)
    score = jnp.where(valid, jnp.where(forced, FORCE_SCORE, imp), -1.0)
    sel = jnp.zeros((nsel, qb), F32)
    for _ in range(SLC_TOPK):
        mx = jnp.max(score, axis=0, keepdims=True)
        idx = jnp.min(jnp.where(score == mx, jid, nsel), axis=0, keepdims=True)
        pick = jid == idx
        sel = jnp.where(pick & (mx >= 0.0), 1.0, sel)
        score = jnp.where(pick, -3e38, score)
    bias = jnp.where(sel > 0.0, 0.0, NEG).astype(MM)
    q_aug = jnp.concatenate([jnp.concatenate([bias] * NSA_GQA, axis=1), qm], axis=0)

    kb = SEL_CHUNK
    init = (jnp.full((1, cols), NEG, F32), jnp.zeros((V_ROWS, cols), F32))

    def scores(j):
        off = pl.multiple_of(j * kb, kb)
        return _dot(ka_ref[0, 0, pl.ds(off, kb), :], q_aug)

    def sel_body(j, carry):
        s_next = scores(j + 1)
        return (s_next,) + _softmax_update(carry[0], vs_ref[0, 0, j], *carry[1:])

    nfull = (blk * qb) // kb
    s, m_s, acc_s = lax.fori_loop(0, nfull, sel_body, (scores(0),) + init)
    s = jnp.where(nfull * kb + lax.broadcasted_iota(jnp.int32, (kb, 1), 0) <= t, s, NEG)
    o_s = _normalise(_softmax_update(s, vs_ref[0, 0, nfull], m_s, acc_s)[1])

    lower = lax.broadcasted_iota(jnp.int32, (qb, 1), 0) <= (t & (qb - 1))
    s_w = []
    v_w = []
    for back in range(WIN // qb, -1, -1):
        c = blk - back
        cc = jnp.maximum(c, 0)
        off = pl.multiple_of(cc * qb, qb)
        s = _dot(kw_ref[0, 0, pl.ds(off, qb), :], qm)
        if back == WIN // qb:
            s = jnp.where(lower | (c < 0), NEG, s)
        elif back == 0:
            s = jnp.where(lower, s, NEG)
        else:
            s = jnp.where(c < 0, NEG, s)
        s_w.append(s)
        v_w.append(vw_ref[0, 0, cc])
    s_w = jnp.concatenate(s_w, axis=0)
    p_w = jnp.exp2(s_w - jnp.max(s_w, axis=0, keepdims=True))
    o_w = _normalise(_dot(jnp.concatenate(v_w, axis=1), p_w.astype(MM)))

    g = jax.nn.sigmoid(gt_ref[0, 0, 0])
    o_ref[0, 0, 0] = g[0:1, :] * o_c + g[1:2, :] * o_s + g[2:3, :] * o_w


def _nsa_attn(qt, gt, kc, vct, ka, vst, kw, vwt):
    b, hkv, nqb, _, cols = qt.shape
    s = ka.shape[2]
    ncmp_pad = kc.shape[2]
    assert CMP_LEN == 2 * CMP_STRIDE and SLC_BLOCK % CMP_STRIDE == 0 and s // SLC_BLOCK <= LANE
    per_head = lambda i, h, j: (i, h, 0, 0)
    per_head5 = lambda i, h, j: (i, h, 0, 0, 0)
    per_blk = lambda i, h, j: (i, h, j, 0, 0)
    return pl.pallas_call(
        _nsa_attn_kernel,
        grid=(b, hkv, nqb),
        in_specs=[pl.BlockSpec((1, 1, 1, LANE, cols), per_blk),
                  pl.BlockSpec((1, 1, 1, 8, cols), per_blk),
                  pl.BlockSpec((1, 1, ncmp_pad, LANE), per_head),
                  pl.BlockSpec((1, 1, LANE, ncmp_pad), per_head),
                  pl.BlockSpec((1, 1, s, 2 * LANE), per_head),
                  pl.BlockSpec((1, 1, s // SEL_CHUNK, V_ROWS, SEL_CHUNK), per_head5),
                  pl.BlockSpec((1, 1, s, LANE), per_head),
                  pl.BlockSpec((1, 1, s // Q_BLOCK, V_ROWS, Q_BLOCK), per_head5)],
        out_specs=pl.BlockSpec((1, 1, 1, NSA_HD, cols), per_blk),
        out_shape=jax.ShapeDtypeStruct((b, hkv, nqb, NSA_HD, cols), F32),
        scratch_shapes=[pltpu.VMEM((ncmp_pad, Q_BLOCK), F32)],
        compiler_params=_cparams("parallel", "parallel", "arbitrary"),
        name="nsa_attn",
    )(qt, gt, kc, vct, ka, vst, kw, vwt)


def _nsa(proj, pos_k, pos_v, ck_w1, ck_w2, cv_w1, cv_w2, tabs):
    b, s, _ = proj.shape
    hkv, g, dh, qb = NSA_KV_HEADS, NSA_GQA, NSA_HD, Q_BLOCK
    nqb = s // qb
    nhalf = s // CMP_STRIDE
    ncmp = (s - CMP_LEN) // CMP_STRIDE + 1
    nsel = s // SLC_BLOCK
    tok_tabs, cmp_tabs = tabs

    q_r, ks_r, kw_r = _nsa_rope(proj, tok_tabs)

    kv0 = sum(IN_SIZES[:4])
    kv = proj[:, :, kv0:kv0 + NSA_KV_W].reshape(b, s, 6, hkv, dh)
    heads = lambda a: jnp.transpose(a.reshape(b, s, hkv, dh), (0, 2, 1, 3))
    hb = jnp.stack([heads(kv[:, :, 0]), heads(kv[:, :, 1])]).reshape(2, b, hkv, nhalf, CMP_STRIDE * dh)
    pos = jnp.stack([pos_k, pos_v]).reshape(2, 1, CMP_LEN * dh)
    pos = jnp.broadcast_to(pos, (2, 8, CMP_LEN * dh))
    w1 = jnp.stack([ck_w1, cv_w1]).astype(MM)
    w2 = jnp.pad(jnp.stack([ck_w2, cv_w2]), ((0, 0), (0, 0), (0, LANE - dh))).astype(MM)
    cmp = _nsa_compress(hb, pos, w1, w2, cmp_tabs)
    kc, vc = cmp[0], cmp[1]

    pad_lanes = lambda a, n: jnp.pad(a, [(0, 0)] * (a.ndim - 1) + [(0, n)])
    pad_rows = lambda a, n: jnp.pad(a, [(0, 0)] * (a.ndim - 2) + [(0, n), (0, 0)])
    qt = q_r.reshape(b, nqb, qb, hkv, g, dh).transpose(0, 3, 1, 5, 4, 2).reshape(b, hkv, nqb, dh, g * qb)
    qt = pad_rows(qt, LANE - dh)
    bg = proj[:, :, BG_OFF:BG_OFF + 3 * NSA_HEADS].reshape(b, nqb, qb, 3, hkv, g)
    gt = pad_rows(bg.transpose(0, 4, 1, 3, 5, 2).reshape(b, hkv, nqb, 3, g * qb), 5)
    assert nsel <= LANE
    onehot = (jnp.arange(s)[:, None] // SLC_BLOCK == jnp.arange(LANE)[None, :]).astype(MM)
    ks_h = heads(ks_r).astype(MM)
    ka = jnp.concatenate([jnp.broadcast_to(onehot, (b, hkv, s, LANE)), ks_h,
                          jnp.zeros((b, hkv, s, LANE - dh), MM)], axis=-1)
    assert ka.shape[-1] == 2 * LANE
    def chunks_t(a, n):
        at = a.reshape(b, hkv, s // n, n, dh).transpose(0, 1, 2, 4, 3)
        ones = jnp.ones((b, hkv, s // n, 1, n), F32)
        zeros = jnp.zeros((b, hkv, s // n, V_ROWS - dh - 1, n), F32)
        return jnp.concatenate([at, ones, zeros], axis=3).astype(MM)

    vst = chunks_t(heads(kv[:, :, 3]), SEL_CHUNK)
    kw = pad_lanes(heads(kw_r), LANE - dh).astype(MM)
    vwt = chunks_t(heads(kv[:, :, 5]), qb)
    o = _nsa_attn(qt, gt, kc, jnp.swapaxes(vc, 2, 3).astype(MM), ka, vst, kw, vwt)
    return o.reshape(b, hkv, nqb, dh, g, qb).transpose(0, 2, 5, 1, 4, 3).reshape(b, s, hkv * g * dh)


def _nsa_tables(s):
    nhalf = s // CMP_STRIDE
    tok_tabs = _nsa_rope_tables(jnp.arange(s))
    cmp_tabs = _nsa_rope_tables(jnp.arange(nhalf) * CMP_STRIDE + CMP_LEN - 1)
    return tok_tabs, cmp_tabs


def _out_kernel(x_ref, g_ref, yl_ref, ys_ref, yr_ref, yn_ref, wg_ref, wm_ref, wb_ref, wo_ref,
                fg_ref, o_ref, *, final):
    x = x_ref[0]
    h = _rms(x, g_ref[...]).astype(MM)
    w = BRANCH_W
    d = D_MODEL
    merged = jnp.zeros(x.shape, F32)
    for n, y_ref in enumerate((yl_ref, ys_ref, yr_ref, yn_ref)):
        gate = jax.nn.silu(_dot(h, wg_ref[:, n * w:(n + 1) * w]))
        branch = _dot((y_ref[0] * gate).astype(MM), wb_ref[n])
        merged = merged + jax.nn.sigmoid(_dot(h, wm_ref[:, n * d:(n + 1) * d])) * branch
    out = x + _dot(merged.astype(MM), wo_ref[...])
    if final:
        out = _rms(out, fg_ref[...])
    o_ref[0] = out


def _out(x, g, ys, wg, wm, wb, wo, fg, final):
    b, s, d = x.shape
    w = BRANCH_W
    tm = TM_PROJ
    row = lambda i, j: (i, j, 0)
    c2 = lambda i, j: (0, 0)
    once = pl.Buffered(1)
    return pl.pallas_call(
        functools.partial(_out_kernel, final=final),
        grid=(b, s // tm),
        in_specs=[pl.BlockSpec((1, tm, d), row),
                  pl.BlockSpec((1, d), c2)]
                 + [pl.BlockSpec((1, tm, w), row)] * N_BRANCH
                 + [pl.BlockSpec((d, N_BRANCH * w), c2, pipeline_mode=once),
                    pl.BlockSpec((d, N_BRANCH * d), c2, pipeline_mode=once),
                    pl.BlockSpec((N_BRANCH, w, d), lambda i, j: (0, 0, 0), pipeline_mode=once),
                    pl.BlockSpec((d, d), c2, pipeline_mode=once),
                    pl.BlockSpec((1, d), c2)],
        out_specs=pl.BlockSpec((1, tm, d), row),
        out_shape=jax.ShapeDtypeStruct((b, s, d), F32),
        compiler_params=_cparams("parallel", "arbitrary"),
        name="gate_merge_out",
    )(x, g, *ys, wg, wm, wb, wo, fg)


def _layer(x, p, l, ret_tabs, nsa_tabs, final_g):
    w_in = p["w_in"][l]
    w_mix = jnp.concatenate([w_in[:, :MIX_W], w_in[:, BG_OFF:GATE_OFF],
                             jnp.zeros((D_MODEL, LANE - IN_SIZES[5]), F32)], axis=1).astype(MM)
    g = p["norm_g"][l].reshape(1, D_MODEL)
    proj = _inproj(x, g, w_mix)

    wab = jnp.concatenate([_block_diag(p["lru_wa"][l]), _block_diag(p["lru_wx"][l])], axis=1).astype(MM)
    bab = jnp.concatenate([p["lru_ba"][l], p["lru_bx"][l]]).reshape(1, 2 * BRANCH_W)
    y_lru = _lru(proj, p["lru_conv_w"][l], p["lru_conv_b"][l].reshape(1, -1), wab, bab,
                 p["lru_lambda"][l].reshape(1, -1))

    wb, apr, api, wc = _s5_tables(p["s5_lambda_re"][l], p["s5_lambda_im"][l], p["s5_b_re"][l],
                                  p["s5_b_im"][l], p["s5_c_re"][l], p["s5_c_im"][l], p["s5_log_dt"][l])
    y_s5 = _s5(proj, wb, apr, api, wc, p["s5_d"][l].reshape(1, -1), p["s5_w_glu"][l].astype(MM),
               p["s5_b_glu"][l].reshape(1, -1))

    y_ret = _retention(proj, ret_tabs)

    y_nsa = _nsa(proj, p["nsa_pos_k"][l], p["nsa_pos_v"][l], p["nsa_ck_w1"][l], p["nsa_ck_w2"][l],
                 p["nsa_cv_w1"][l], p["nsa_cv_w2"][l], nsa_tabs)

    wg = w_in[:, GATE_OFF:MERGE_OFF].astype(MM)
    wm = w_in[:, MERGE_OFF:].astype(MM)
    final = l == DEPTH - 1
    return _out(x, g, (y_lru, y_s5, y_ret, y_nsa), wg, wm, p["w_branch"][l].astype(MM),
                p["w_out"][l].astype(MM), final_g.reshape(1, D_MODEL), final)


def kernel(x, norm_g, w_in, lru_conv_w, lru_conv_b, lru_wa, lru_ba, lru_wx, lru_bx, lru_lambda,
           s5_lambda_re, s5_lambda_im, s5_b_re, s5_b_im, s5_c_re, s5_c_im, s5_log_dt, s5_d,
           s5_w_glu, s5_b_glu, nsa_pos_k, nsa_pos_v, nsa_ck_w1, nsa_ck_w2, nsa_cv_w1, nsa_cv_w2,
           w_branch, w_out, final_norm_g):
    p = dict(norm_g=norm_g, w_in=w_in, lru_conv_w=lru_conv_w, lru_conv_b=lru_conv_b, lru_wa=lru_wa,
             lru_ba=lru_ba, lru_wx=lru_wx, lru_bx=lru_bx, lru_lambda=lru_lambda,
             s5_lambda_re=s5_lambda_re, s5_lambda_im=s5_lambda_im, s5_b_re=s5_b_re, s5_b_im=s5_b_im,
             s5_c_re=s5_c_re, s5_c_im=s5_c_im, s5_log_dt=s5_log_dt, s5_d=s5_d, s5_w_glu=s5_w_glu,
             s5_b_glu=s5_b_glu, nsa_pos_k=nsa_pos_k, nsa_pos_v=nsa_pos_v, nsa_ck_w1=nsa_ck_w1,
             nsa_ck_w2=nsa_ck_w2, nsa_cv_w1=nsa_cv_w1, nsa_cv_w2=nsa_cv_w2, w_branch=w_branch,
             w_out=w_out)
    s = x.shape[1]
    ret_tabs = _ret_tables(s)
    nsa_tabs = _nsa_tables(s)
    for l in range(DEPTH):
        x = _layer(x, p, l, ret_tabs, nsa_tabs, final_norm_g)
    return x
```

```python
import functools
import math

import jax
import jax.numpy as jnp
from jax import lax
from jax.experimental import pallas as pl
from jax.experimental.pallas import tpu as pltpu

F32 = jnp.float32
MM = jnp.bfloat16

D_MODEL = 1024
DEPTH = 4
N_BRANCH = 4
BRANCH_W = 512
NORM_EPS = 1e-6
LRU_HEADS = 8
LRU_HD = BRANCH_W // LRU_HEADS
CONV_W = 4
LRU_C = 8.0
S5_GROUP = 16
S5_GROUPS = BRANCH_W // S5_GROUP
S5_STATE = 64
S5_N = S5_GROUPS * S5_STATE
RET_HEADS = 4
RET_HD = BRANCH_W // RET_HEADS
RET_CHUNK = 128
RET_ROPE_BASE = 10000.0
NSA_HEADS = 8
NSA_KV_HEADS = 2
NSA_HD = BRANCH_W // NSA_HEADS
NSA_GQA = NSA_HEADS // NSA_KV_HEADS
CMP_LEN = 32
CMP_STRIDE = 16
CMP_HIDDEN = 256
SLC_BLOCK = 64
SLC_TOPK = 16
WIN = 512
Q_BLOCK = 128
ROPE_THETA = 500000.0
ROPE_DIM = NSA_HD // 4
FORCE_SCORE = 1e4
NEG = -1e30
NSA_KV_W = 6 * NSA_KV_HEADS * NSA_HD
IN_SIZES = (BRANCH_W, BRANCH_W, 3 * BRANCH_W, NSA_HEADS * NSA_HD, NSA_KV_W, 3 * NSA_HEADS,
            N_BRANCH * BRANCH_W, N_BRANCH * D_MODEL)
LANE = 128
MIX_W = sum(IN_SIZES[:5])
BG_OFF = MIX_W
GATE_OFF = BG_OFF + IN_SIZES[5]
MERGE_OFF = GATE_OFF + IN_SIZES[6]
PROJ_W = MIX_W + LANE
VMEM_LIMIT = 56 * 1024 * 1024

TM_PROJ = 256
T_LRU = 256
T_S5 = 256
T_RET = 512
SEL_CHUNK = 512
NSA_T = SEL_CHUNK
V_ROWS = NSA_HD + 16


def _cparams(*sem):
    return pltpu.CompilerParams(dimension_semantics=sem, vmem_limit_bytes=VMEM_LIMIT)


def _rms(x, g):
    ms = jnp.mean(x * x, axis=-1, keepdims=True)
    return (x * lax.rsqrt(ms + NORM_EPS)) * g


def _dot(a, b):
    return jnp.dot(a, b, preferred_element_type=F32)


def _dot_nt(a, b):
    return lax.dot_general(a, b, (((1,), (1,)), ((), ())), preferred_element_type=F32)


def _dot_tn(a, b):
    return lax.dot_general(a, b, (((0,), (0,)), ((), ())), preferred_element_type=F32)


def _inproj_kernel(x_ref, g_ref, w_ref, o_ref):
    h = _rms(x_ref[0], g_ref[...])
    o_ref[0] = _dot(h.astype(MM), w_ref[0])


def _inproj(x, g, w, l):
    b, s, d = x.shape
    n = w.shape[2]
    return pl.pallas_call(
        _inproj_kernel,
        grid=(b, s // TM_PROJ),
        in_specs=[pl.BlockSpec((1, TM_PROJ, d), lambda i, j: (i, j, 0)),
                  pl.BlockSpec((1, d), lambda i, j: (0, 0)),
                  pl.BlockSpec((1, d, n), lambda i, j: (l, 0, 0))],
        out_specs=pl.BlockSpec((1, TM_PROJ, n), lambda i, j: (i, j, 0)),
        out_shape=jax.ShapeDtypeStruct((b, s, n), F32),
        compiler_params=_cparams("parallel", "arbitrary"),
        name="inproj",
    )(x, g, w)


def _lru_kernel(u_ref, cw_ref, cb_ref, wab_ref, bab_ref, lam_ref, o_ref, ubuf, hcar):
    t = T_LRU
    w = BRANCH_W

    @pl.when(pl.program_id(1) == 0)
    def _():
        ubuf[0:8, :] = jnp.zeros((8, w), F32)
        hcar[...] = jnp.zeros((1, w), F32)

    ubuf[8:8 + t, :] = u_ref[0]
    xc = cb_ref[...] + cw_ref[0:1, :] * ubuf[5:5 + t, :]
    for k in range(1, CONV_W):
        xc = xc + cw_ref[k:k + 1, :] * ubuf[5 + k:5 + k + t, :]
    ubuf[0:8, :] = ubuf[t:t + 8, :]

    ri = _dot(xc.astype(MM), wab_ref[...]) + bab_ref[...]
    r = jax.nn.sigmoid(ri[:, :w])
    gi = jax.nn.sigmoid(ri[:, w:])
    z = -lam_ref[...]
    softplus = jnp.maximum(z, 0.0) + jnp.log1p(jnp.exp(-jnp.abs(z)))
    a = jnp.exp(-LRU_C * r * softplus)
    bt = jnp.sqrt(1.0 - a * a) * gi * xc

    row = lax.broadcasted_iota(jnp.int32, (t, w), 0)
    k = 1
    while k < t:
        keep = row >= k
        a_sh = jnp.where(keep, pltpu.roll(a, k, 0), 1.0)
        b_sh = jnp.where(keep, pltpu.roll(bt, k, 0), 0.0)
        bt = a * b_sh + bt
        a = a * a_sh
        k *= 2
    h = bt + a * hcar[...]
    o_ref[0] = h
    hcar[...] = h[t - 1:t, :]


def _lru(proj, cw, cb, wab, bab, lam):
    b, s, _ = proj.shape
    w = BRANCH_W
    return pl.pallas_call(
        _lru_kernel,
        grid=(b, s // T_LRU),
        in_specs=[pl.BlockSpec((1, T_LRU, w), lambda i, j: (i, j, 0)),
                  pl.BlockSpec((CONV_W, w), lambda i, j: (0, 0)),
                  pl.BlockSpec((1, w), lambda i, j: (0, 0)),
                  pl.BlockSpec((w, 2 * w), lambda i, j: (0, 0)),
                  pl.BlockSpec((1, 2 * w), lambda i, j: (0, 0)),
                  pl.BlockSpec((1, w), lambda i, j: (0, 0))],
        out_specs=pl.BlockSpec((1, T_LRU, w), lambda i, j: (i, j, 0)),
        out_shape=jax.ShapeDtypeStruct((b, s, w), F32),
        scratch_shapes=[pltpu.VMEM((T_LRU + 8, w), F32), pltpu.VMEM((1, w), F32)],
        compiler_params=_cparams("parallel", "arbitrary"),
        name="rg_lru",
    )(proj, cw, cb, wab, bab, lam)


def _block_diag(blocks):
    n, r, c = blocks.shape
    eye = jnp.eye(n, dtype=blocks.dtype)
    return jnp.einsum("nrc,nm->nrmc", blocks, eye).reshape(n * r, n * c)


def _s5_kernel(u_ref, wb_ref, apr_ref, api_ref, wc_ref, d_ref, wg_ref, bg_ref, o_ref, car):
    t = T_S5
    n = S5_N
    w = BRANCH_W

    @pl.when(pl.program_id(1) == 0)
    def _():
        car[...] = jnp.zeros((2, n), F32)

    u = u_ref[0]
    x = _dot(u.astype(MM), wb_ref[...])
    xr = x[:, :n]
    xi = x[:, n:]
    row = lax.broadcasted_iota(jnp.int32, (t, n), 0)
    ar = apr_ref[0:1, :]
    ai = api_ref[0:1, :]
    cr = car[0:1, :]
    ci = car[1:2, :]
    first = row == 0
    xr = xr + jnp.where(first, ar * cr - ai * ci, 0.0)
    xi = xi + jnp.where(first, ar * ci + ai * cr, 0.0)
    k = 1
    step = 0
    while k < t:
        ar = apr_ref[step:step + 1, :]
        ai = api_ref[step:step + 1, :]
        keep = row >= k
        xr_s = jnp.where(keep, pltpu.roll(xr, k, 0), 0.0)
        xi_s = jnp.where(keep, pltpu.roll(xi, k, 0), 0.0)
        xr, xi = xr + (ar * xr_s - ai * xi_s), xi + (ar * xi_s + ai * xr_s)
        k *= 2
        step += 1
    car[0:1, :] = xr[t - 1:t, :]
    car[1:2, :] = xi[t - 1:t, :]
    y = _dot(xr.astype(MM), wc_ref[0:n, :]) + _dot(xi.astype(MM), wc_ref[n:2 * n, :])
    y = y + d_ref[...] * u
    z = jax.nn.gelu(y)
    g = _dot(z.astype(MM), wg_ref[...]) + bg_ref[...]
    o_ref[0] = g[:, :w] * jax.nn.sigmoid(g[:, w:])


def _s5_tables(lam_re, lam_im, b_re, b_im, c_re, c_im, log_dt):
    dt = jnp.exp(log_dt)[:, None]
    lr = jnp.minimum(lam_re, -1e-4)
    li = lam_im
    mag = jnp.exp(lr * dt)
    ab_re = mag * jnp.cos(li * dt)
    ab_im = mag * jnp.sin(li * dt)
    den = lr * lr + li * li
    nr = ab_re - 1.0
    ni = ab_im
    f_re = (nr * lr + ni * li) / den
    f_im = (ni * lr - nr * li) / den
    bb_re = f_re[..., None] * b_re - f_im[..., None] * b_im
    bb_im = f_re[..., None] * b_im + f_im[..., None] * b_re
    wb = jnp.concatenate([_block_diag(jnp.swapaxes(bb_re, 1, 2)),
                          _block_diag(jnp.swapaxes(bb_im, 1, 2))], axis=1)
    wc = jnp.concatenate([_block_diag(jnp.swapaxes(c_re, 1, 2)),
                          _block_diag(jnp.swapaxes(-c_im, 1, 2))], axis=0)
    pr = [ab_re.reshape(1, S5_N)]
    pi = [ab_im.reshape(1, S5_N)]
    for _ in range(int(math.log2(T_S5)) - 1):
        r0, i0 = pr[-1], pi[-1]
        pr.append(r0 * r0 - i0 * i0)
        pi.append(2.0 * r0 * i0)
    return wb.astype(MM), jnp.concatenate(pr, 0), jnp.concatenate(pi, 0), wc.astype(MM)


def _s5(proj, wb, apr, api, wc, d, wg, bg):
    b, s, _ = proj.shape
    w = BRANCH_W
    n = S5_N
    ns = apr.shape[0]
    return pl.pallas_call(
        _s5_kernel,
        grid=(b, s // T_S5),
        in_specs=[pl.BlockSpec((1, T_S5, w), lambda i, j: (i, j, 1)),
                  pl.BlockSpec((w, 2 * n), lambda i, j: (0, 0)),
                  pl.BlockSpec((ns, n), lambda i, j: (0, 0)),
                  pl.BlockSpec((ns, n), lambda i, j: (0, 0)),
                  pl.BlockSpec((2 * n, w), lambda i, j: (0, 0)),
                  pl.BlockSpec((1, w), lambda i, j: (0, 0)),
                  pl.BlockSpec((w, 2 * w), lambda i, j: (0, 0)),
                  pl.BlockSpec((1, 2 * w), lambda i, j: (0, 0))],
        out_specs=pl.BlockSpec((1, T_S5, w), lambda i, j: (i, j, 0)),
        out_shape=jax.ShapeDtypeStruct((b, s, w), F32),
        scratch_shapes=[pltpu.VMEM((2, n), F32)],
        compiler_params=_cparams("parallel", "arbitrary"),
        name="s5",
    )(proj, wb, apr, api, wc, d, wg, bg)


def _ret_kernel(q_ref, k_ref, v_ref, cos_ref, sin_ref, dm_ref, xi_ref, zt_ref, cd_ref, o_ref, rst):
    c = RET_CHUNK
    dh = RET_HD

    @pl.when(pl.program_id(1) == 0)
    def _():
        rst[...] = jnp.zeros((RET_HEADS, dh, dh), F32)

    for ci in range(T_RET // c):
        rows = slice(ci * c, (ci + 1) * c)
        cos = cos_ref[rows, :]
        sin = sin_ref[rows, :]
        for h in range(RET_HEADS):
            cols = slice(h * dh, (h + 1) * dh)
            q = q_ref[0, rows, cols]
            k = k_ref[0, rows, cols]
            v = v_ref[0, rows, cols]
            q = q * cos + pltpu.roll(q, dh // 2, 1) * sin
            k = (k * cos + pltpu.roll(k, dh // 2, 1) * sin) * (dh ** -0.5)
            qb = q.astype(MM)
            kb = k.astype(MM)
            vb = v.astype(MM)
            scores = _dot_nt(qb, kb) * dm_ref[h]
            inner = _dot(scores.astype(MM), vb)
            r_prev = rst[h]
            cross = _dot((q * xi_ref[h]).astype(MM), r_prev.astype(MM))
            kv = _dot_tn((k * zt_ref[h]).astype(MM), vb)
            rst[h] = cd_ref[h] * r_prev + kv
            o = inner + cross
            mu = jnp.mean(o, axis=-1, keepdims=True)
            var = jnp.mean(jnp.square(o - mu), axis=-1, keepdims=True)
            o_ref[0, rows, cols] = (o - mu) * lax.rsqrt(var + 1e-5)


def _ret_tables(s):
    dh = RET_HD
    c = RET_CHUNK
    half = dh // 2
    inv = RET_ROPE_BASE ** (-jnp.arange(half, dtype=F32) * 2.0 / dh)
    ang = jnp.arange(s).astype(F32)[:, None] * inv[None, :]
    cos = jnp.cos(ang)
    sin = jnp.sin(ang)
    cosf = jnp.concatenate([cos, cos], axis=1)
    sinf = jnp.concatenate([-sin, sin], axis=1)
    gamma = 1.0 - jnp.exp(jnp.linspace(math.log(1.0 / 32.0), math.log(1.0 / 512.0), RET_HEADS, dtype=F32))
    log_g = jnp.log(gamma)
    n = jnp.arange(c, dtype=F32)
    diff = n[:, None] - n[None, :]
    dmask = jnp.where(diff[None] >= 0, jnp.exp(jnp.maximum(diff, 0.0)[None] * log_g[:, None, None]), 0.0)
    xi = jnp.exp((n[None, :] + 1.0) * log_g[:, None])
    zeta = jnp.exp((c - 1.0 - n)[None, :] * log_g[:, None])
    cdec = jnp.exp(c * log_g)
    xi_b = jnp.broadcast_to(xi[:, :, None], (RET_HEADS, c, dh))
    zt_b = jnp.broadcast_to(zeta[:, :, None], (RET_HEADS, c, dh))
    cd_b = jnp.broadcast_to(cdec[:, None, None], (RET_HEADS, dh, dh))
    return cosf, sinf, dmask, xi_b, zt_b, cd_b


def _retention(proj, tabs):
    b, s, _ = proj.shape
    w = BRANCH_W
    cosf, sinf, dmask, xi_b, zt_b, cd_b = tabs
    c = RET_CHUNK
    dh = RET_HD
    full3 = lambda i, j: (0, 0, 0)
    return pl.pallas_call(
        _ret_kernel,
        grid=(b, s // T_RET),
        in_specs=[pl.BlockSpec((1, T_RET, w), lambda i, j: (i, j, 2)),
                  pl.BlockSpec((1, T_RET, w), lambda i, j: (i, j, 3)),
                  pl.BlockSpec((1, T_RET, w), lambda i, j: (i, j, 4)),
                  pl.BlockSpec((T_RET, dh), lambda i, j: (j, 0)),
                  pl.BlockSpec((T_RET, dh), lambda i, j: (j, 0)),
                  pl.BlockSpec((RET_HEADS, c, c), full3),
                  pl.BlockSpec((RET_HEADS, c, dh), full3),
                  pl.BlockSpec((RET_HEADS, c, dh), full3),
                  pl.BlockSpec((RET_HEADS, dh, dh), full3)],
        out_specs=pl.BlockSpec((1, T_RET, w), lambda i, j: (i, j, 0)),
        out_shape=jax.ShapeDtypeStruct((b, s, w), F32),
        scratch_shapes=[pltpu.VMEM((RET_HEADS, dh, dh), F32)],
        compiler_params=_cparams("parallel", "arbitrary"),
        name="retention",
    )(proj, proj, proj, cosf, sinf, dmask, xi_b, zt_b, cd_b)


def _rope_lanes(x, cos, s_up, s_dn):
    half = ROPE_DIM // 2
    n = x.shape[-1]
    return x * cos + pltpu.roll(x, half, 1) * s_up + pltpu.roll(x, n - half, 1) * s_dn


def _nsa_prep_kernel(q_ref, ks_ref, vs_ref, kw_ref, vw_ref, bg_ref, cos_ref, up_ref, dn_ref,
                     qt_ref, gt_ref, ka_ref, kwo_ref, vst_ref, vwt_ref):
    t = NSA_T
    qb = Q_BLOCK
    dh = NSA_HD
    cos = cos_ref[...]
    up = up_ref[...]
    dn = dn_ref[...]
    scale = NSA_HD ** -0.5 * math.log2(math.e)
    lane = lax.broadcasted_iota(jnp.int32, (t, LANE), 1)
    low = lane < dh
    tok = pl.program_id(1) * t + lax.broadcasted_iota(jnp.int32, (t, LANE), 0)
    onehot = jnp.where(tok // SLC_BLOCK == lane, 1.0, 0.0).astype(MM)
    ks = _rope_lanes(ks_ref[0], cos, up, dn)
    kw = _rope_lanes(kw_ref[0], cos, up, dn)
    for h in range(NSA_KV_HEADS):
        ks_h = ks if h == 0 else pltpu.roll(ks, LANE - h * dh, 1)
        kw_h = kw if h == 0 else pltpu.roll(kw, LANE - h * dh, 1)
        ka_ref[0, h, :, 0:LANE] = onehot
        ka_ref[0, h, :, LANE:2 * LANE] = jnp.where(low, ks_h, 0.0).astype(MM)
        kwo_ref[0, h] = jnp.where(low, kw_h, 0.0).astype(MM)

    def ones_row(n):
        return jnp.where(lax.broadcasted_iota(jnp.int32, (V_ROWS - dh, n), 0) == 0, 1.0, 0.0).astype(MM)

    r8 = lax.broadcasted_iota(jnp.int32, (8, NSA_GQA * qb), 0)
    for i in range(t // qb):
        rows = slice(i * qb, (i + 1) * qb)
        vs_t = vs_ref[0, rows, :].T
        vw_t = vw_ref[0, rows, :].T
        bg_t = bg_ref[0, rows, :].T
        for h in range(NSA_KV_HEADS):
            vst_ref[0, h, 0, 0:dh, rows] = vs_t[h * dh:(h + 1) * dh].astype(MM)
            vwt_ref[0, h, i, 0:dh, :] = vw_t[h * dh:(h + 1) * dh].astype(MM)
            vwt_ref[0, h, i, dh:V_ROWS, :] = ones_row(qb)
            qt_ref[0, h, i, dh:LANE, :] = jnp.zeros((LANE - dh, NSA_GQA * qb), F32)
            gates = jnp.zeros((8, NSA_GQA * qb), F32)
            for br in range(3):
                base = br * NSA_HEADS + h * NSA_GQA
                row = jnp.concatenate([bg_t[base + g:base + g + 1] for g in range(NSA_GQA)], axis=1)
                gates = jnp.where(r8 == br, row, gates)
            gt_ref[0, h, i] = gates
        for cb in range(BRANCH_W // LANE):
            cols = slice(cb * LANE, (cb + 1) * LANE)
            x_t = (_rope_lanes(q_ref[0, rows, cols], cos[rows], up[rows], dn[rows]) * scale).T
            h = (2 * cb) // NSA_GQA
            g0 = (2 * cb) % NSA_GQA
            qt_ref[0, h, i, 0:dh, g0 * qb:(g0 + 1) * qb] = x_t[0:dh]
            qt_ref[0, h, i, 0:dh, (g0 + 1) * qb:(g0 + 2) * qb] = x_t[dh:2 * dh]
    for h in range(NSA_KV_HEADS):
        vst_ref[0, h, 0, dh:V_ROWS, :] = ones_row(t)


def _nsa_rope_tables(pos):
    half = ROPE_DIM // 2
    inv = ROPE_THETA ** (-jnp.arange(half, dtype=F32) * 2.0 / ROPE_DIM)
    ang = pos.astype(F32)[:, None] * inv[None, :]
    cos = jnp.cos(ang)
    sin = jnp.sin(ang)
    n = pos.shape[0]
    zeros = jnp.zeros((n, NSA_HD - ROPE_DIM), F32)
    zh = jnp.zeros((n, half), F32)
    cos_h = jnp.concatenate([cos, cos, jnp.ones((n, NSA_HD - ROPE_DIM), F32)], axis=1)
    up_h = jnp.concatenate([zh, sin, zeros], axis=1)
    dn_h = jnp.concatenate([-sin, zh, zeros], axis=1)
    tile2 = lambda a: jnp.concatenate([a, a], axis=1)
    return tile2(cos_h), tile2(up_h), tile2(dn_h)


def _nsa_prep(proj, tabs):
    b, s, _ = proj.shape
    t = NSA_T
    qb = Q_BLOCK
    hkv = NSA_KV_HEADS
    cols = NSA_GQA * qb
    assert t == SEL_CHUNK and s // SLC_BLOCK <= LANE
    q_blk = sum(IN_SIZES[:3]) // BRANCH_W
    kv_blk = sum(IN_SIZES[:4]) // LANE
    lane_blk = lambda n: pl.BlockSpec((1, t, LANE), lambda i, j: (i, j, n))
    tab_spec = pl.BlockSpec((t, LANE), lambda i, j: (j, 0))
    tile4 = lambda i, j: (i, 0, j, 0)
    tile5 = lambda i, j: (i, 0, j, 0, 0)
    return pl.pallas_call(
        _nsa_prep_kernel,
        grid=(b, s // t),
        in_specs=[pl.BlockSpec((1, t, BRANCH_W), lambda i, j: (i, j, q_blk)),
                  lane_blk(kv_blk + 2), lane_blk(kv_blk + 3), lane_blk(kv_blk + 4), lane_blk(kv_blk + 5),
                  lane_blk(BG_OFF // LANE), tab_spec, tab_spec, tab_spec],
        out_specs=[pl.BlockSpec((1, hkv, t // qb, LANE, cols), tile5),
                   pl.BlockSpec((1, hkv, t // qb, 8, cols), tile5),
                   pl.BlockSpec((1, hkv, t, 2 * LANE), tile4),
                   pl.BlockSpec((1, hkv, t, LANE), tile4),
                   pl.BlockSpec((1, hkv, 1, V_ROWS, t), tile5),
                   pl.BlockSpec((1, hkv, t // qb, V_ROWS, qb), tile5)],
        out_shape=[jax.ShapeDtypeStruct((b, hkv, s // qb, LANE, cols), F32),
                   jax.ShapeDtypeStruct((b, hkv, s // qb, 8, cols), F32),
                   jax.ShapeDtypeStruct((b, hkv, s, 2 * LANE), MM),
                   jax.ShapeDtypeStruct((b, hkv, s, LANE), MM),
                   jax.ShapeDtypeStruct((b, hkv, s // t, V_ROWS, t), MM),
                   jax.ShapeDtypeStruct((b, hkv, s // qb, V_ROWS, qb), MM)],
        compiler_params=_cparams("parallel", "parallel"),
        name="nsa_prep",
    )(proj, proj, proj, proj, proj, proj, *tabs)


def _cmp_kernel(xk_ref, xv_ref, pos_ref, w1_ref, w2_ref, cos_ref, up_ref, dn_ref, kc_ref, vct_ref):
    nh = kc_ref.shape[2]
    hid_w = NSA_KV_HEADS * CMP_HIDDEN
    for kind, x_ref in enumerate((xk_ref, xv_ref)):
        lo = jnp.zeros((nh, hid_w), F32)
        hi = jnp.zeros((nh, hid_w), F32)
        posb = jnp.zeros((8, hid_w), F32)
        for j in range(CMP_STRIDE):
            xj = x_ref[0, pl.ds(j, nh, stride=CMP_STRIDE), :].astype(MM)
            lo = lo + _dot(xj, w1_ref[kind, j])
            hi = hi + _dot(xj, w1_ref[kind, CMP_STRIDE + j])
        for j in range(CMP_LEN):
            posb = posb + _dot(pos_ref[kind, j].astype(MM), w1_ref[kind, j])
        hid = jax.nn.gelu(lo + pltpu.roll(hi, nh - 1, 0) + posb[0:1, :])
        for h in range(NSA_KV_HEADS):
            out = _dot(hid[:, h * CMP_HIDDEN:(h + 1) * CMP_HIDDEN].astype(MM), w2_ref[kind])
            if kind == 0:
                kc_ref[0, h] = _rope_lanes(out, cos_ref[...], up_ref[...], dn_ref[...])
            else:
                vct_ref[0, h] = out.T.astype(MM)


def _nsa_compress(proj, pos, w1, w2, tabs):
    b, s, _ = proj.shape
    nh = s // CMP_STRIDE
    hkv = NSA_KV_HEADS
    kv_blk = sum(IN_SIZES[:4]) // LANE
    once = pl.Buffered(1)
    c2 = lambda i: (0, 0)
    c3 = lambda i: (0, 0, 0)
    c4 = lambda i: (0, 0, 0, 0)
    return pl.pallas_call(
        _cmp_kernel,
        grid=(b,),
        in_specs=[pl.BlockSpec((1, s, LANE), lambda i: (i, 0, kv_blk)),
                  pl.BlockSpec((1, s, LANE), lambda i: (i, 0, kv_blk + 1)),
                  pl.BlockSpec(pos.shape, c4),
                  pl.BlockSpec(w1.shape, c4, pipeline_mode=once),
                  pl.BlockSpec(w2.shape, c3),
                  pl.BlockSpec((nh, LANE), c2), pl.BlockSpec((nh, LANE), c2), pl.BlockSpec((nh, LANE), c2)],
        out_specs=[pl.BlockSpec((1, hkv, nh, LANE), lambda i: (i, 0, 0, 0)),
                   pl.BlockSpec((1, hkv, LANE, nh), lambda i: (i, 0, 0, 0))],
        out_shape=[jax.ShapeDtypeStruct((b, hkv, nh, LANE), F32),
                   jax.ShapeDtypeStruct((b, hkv, LANE, nh), MM)],
        compiler_params=_cparams("parallel"),
        name="nsa_compress",
    )(proj, proj, pos, w1, w2, *tabs)


def _softmax_update(s, vt, m, acc):
    m_new = jnp.maximum(m, jnp.max(s, axis=0, keepdims=True))
    alpha = jnp.exp2(m - m_new)
    p = jnp.exp2(s - m_new)
    acc = alpha * acc + _dot(vt, p.astype(MM))
    return m_new, acc


def _normalise(acc):
    return acc[:NSA_HD, :] * (1.0 / acc[NSA_HD:NSA_HD + 1, :])


def _nsa_attn_kernel(q_ref, gt_ref, kc_ref, vc_ref, ka_ref, vs_ref, kw_ref, vw_ref, o_ref, ps_ref):
    blk = pl.program_id(2)
    qb = Q_BLOCK
    cols = NSA_GQA * qb
    ncmp_pad = kc_ref.shape[2]
    nsel = LANE
    qf = q_ref[0, 0, 0]
    qm = qf.astype(MM)
    t = blk * qb + (lax.broadcasted_iota(jnp.int32, (1, cols), 1) & (qb - 1))

    s_c = _dot(kc_ref[0, 0], qf)
    n_id = lax.broadcasted_iota(jnp.int32, (ncmp_pad, 1), 0)
    m_c = (n_id * CMP_STRIDE + (CMP_LEN - 1) <= t) & (n_id < ncmp_pad - 1)
    smax = jnp.max(jnp.where(m_c, s_c, NEG), axis=0, keepdims=True)
    e_c = jnp.where(m_c, jnp.exp2(s_c - smax), 0.0)
    den = jnp.sum(e_c, axis=0, keepdims=True)
    p_c = e_c * (1.0 / jnp.where(den > 0.0, den, 1.0))
    o_c = _dot(vc_ref[0, 0], p_c.astype(MM))[:NSA_HD, :]

    r = SLC_BLOCK // CMP_STRIDE
    nrow = ncmp_pad // r
    ps_ref[...] = p_c[:, 0:qb] + p_c[:, qb:2 * qb] + p_c[:, 2 * qb:3 * qb] + p_c[:, 3 * qb:4 * qb]
    imp = ps_ref[pl.ds(0, nrow, stride=r), :]
    for k in range(1, r - 1):
        imp = imp + ps_ref[pl.ds(k, nrow, stride=r), :]
    edge = 0.5 * ps_ref[pl.ds(r - 1, nrow, stride=r), :]
    first = lax.broadcasted_iota(jnp.int32, (nrow, qb), 0) == 0
    imp = imp + edge + jnp.where(first, 0.0, pltpu.roll(edge, 1, 0))
    if nrow < nsel:
        imp = jnp.concatenate([imp, jnp.zeros((nsel - nrow, qb), F32)], axis=0)

    cur = (blk * qb + lax.broadcasted_iota(jnp.int32, (1, qb), 1)) // SLC_BLOCK
    jid = lax.broadcasted_iota(jnp.int32, (nsel, qb), 0)
    valid = jid <= cur
    forced = (jid == 0) | (jid == cur) | (jid == cur - 1)
    score = jnp.where(valid, jnp.where(forced, FORCE_SCORE, imp), -1.0)
    sel = jnp.zeros((nsel, qb), F32)
    for _ in range(SLC_TOPK):
        mx = jnp.max(score, axis=0, keepdims=True)
        idx = jnp.min(jnp.where(score == mx, jid, nsel), axis=0, keepdims=True)
        pick = jid == idx
        sel = jnp.where(pick & (mx >= 0.0), 1.0, sel)
        score = jnp.where(pick, -3e38, score)
    bias = jnp.where(sel > 0.0, 0.0, NEG).astype(MM)
    q_aug = jnp.concatenate([jnp.concatenate([bias] * NSA_GQA, axis=1), qm], axis=0)

    kb = SEL_CHUNK
    init = (jnp.full((1, cols), NEG, F32), jnp.zeros((V_ROWS, cols), F32))

    def scores(j):
        off = pl.multiple_of(j * kb, kb)
        return _dot(ka_ref[0, 0, pl.ds(off, kb), :], q_aug)

    def sel_body(j, carry):
        s_next = scores(j + 1)
        return (s_next,) + _softmax_update(carry[0], vs_ref[0, 0, j], *carry[1:])

    nfull = (blk * qb) // kb
    s, m_s, acc_s = lax.fori_loop(0, nfull, sel_body, (scores(0),) + init)
    s = jnp.where(nfull * kb + lax.broadcasted_iota(jnp.int32, (kb, 1), 0) <= t, s, NEG)
    o_s = _normalise(_softmax_update(s, vs_ref[0, 0, nfull], m_s, acc_s)[1])

    lower = lax.broadcasted_iota(jnp.int32, (qb, 1), 0) <= (t & (qb - 1))
    s_w = []
    v_w = []
    for back in range(WIN // qb, -1, -1):
        c = blk - back
        cc = jnp.maximum(c, 0)
        off = pl.multiple_of(cc * qb, qb)
        s = _dot(kw_ref[0, 0, pl.ds(off, qb), :], qm)
        if back == WIN // qb:
            s = jnp.where(lower | (c < 0), NEG, s)
        elif back == 0:
            s = jnp.where(lower, s, NEG)
        else:
            s = jnp.where(c < 0, NEG, s)
        s_w.append(s)
        v_w.append(vw_ref[0, 0, cc])
    s_w = jnp.concatenate(s_w, axis=0)
    p_w = jnp.exp2(s_w - jnp.max(s_w, axis=0, keepdims=True))
    o_w = _normalise(_dot(jnp.concatenate(v_w, axis=1), p_w.astype(MM)))

    g = jax.nn.sigmoid(gt_ref[0, 0, 0])
    o = g[0:1, :] * o_c + g[1:2, :] * o_s + g[2:3, :] * o_w
    for gp in range(NSA_GQA // 2):
        pair = jnp.concatenate([o[:, 2 * gp * qb:(2 * gp + 1) * qb],
                                o[:, (2 * gp + 1) * qb:(2 * gp + 2) * qb]], axis=0)
        o_ref[0, :, gp * LANE:(gp + 1) * LANE] = pair.T


def _nsa_attn(qt, gt, kc, vct, ka, vst, kw, vwt):
    b, hkv, nqb, _, cols = qt.shape
    s = ka.shape[2]
    ncmp_pad = kc.shape[2]
    assert CMP_LEN == 2 * CMP_STRIDE and SLC_BLOCK % CMP_STRIDE == 0 and s // SLC_BLOCK <= LANE
    per_head = lambda i, h, j: (i, h, 0, 0)
    per_head5 = lambda i, h, j: (i, h, 0, 0, 0)
    per_blk = lambda i, h, j: (i, h, j, 0, 0)
    return pl.pallas_call(
        _nsa_attn_kernel,
        grid=(b, hkv, nqb),
        in_specs=[pl.BlockSpec((1, 1, 1, LANE, cols), per_blk),
                  pl.BlockSpec((1, 1, 1, 8, cols), per_blk),
                  pl.BlockSpec((1, 1, ncmp_pad, LANE), per_head),
                  pl.BlockSpec((1, 1, LANE, ncmp_pad), per_head),
                  pl.BlockSpec((1, 1, s, 2 * LANE), per_head),
                  pl.BlockSpec((1, 1, s // SEL_CHUNK, V_ROWS, SEL_CHUNK), per_head5),
                  pl.BlockSpec((1, 1, s, LANE), per_head),
                  pl.BlockSpec((1, 1, s // Q_BLOCK, V_ROWS, Q_BLOCK), per_head5)],
        out_specs=pl.BlockSpec((1, Q_BLOCK, NSA_GQA * NSA_HD), lambda i, h, j: (i, j, h)),
        out_shape=jax.ShapeDtypeStruct((b, s, hkv * NSA_GQA * NSA_HD), F32),
        scratch_shapes=[pltpu.VMEM((ncmp_pad, Q_BLOCK), F32)],
        compiler_params=_cparams("parallel", "parallel", "arbitrary"),
        name="nsa_attn",
    )(qt, gt, kc, vct, ka, vst, kw, vwt)


def _nsa_weights(pos_k, pos_v, ck_w1, ck_w2, cv_w1, cv_w2):
    dh = NSA_HD
    eye = jnp.eye(NSA_KV_HEADS, dtype=F32)
    slabs = lambda w: jnp.einsum("jrc,hg->jhrgc", w.reshape(CMP_LEN, dh, CMP_HIDDEN), eye).reshape(
        CMP_LEN, NSA_KV_HEADS * dh, NSA_KV_HEADS * CMP_HIDDEN)
    w1 = jnp.stack([slabs(ck_w1), slabs(cv_w1)]).astype(MM)
    w2 = jnp.pad(jnp.stack([ck_w2, cv_w2]), ((0, 0), (0, 0), (0, LANE - dh))).astype(MM)
    pos = jnp.stack([pos_k, pos_v])
    pos = jnp.broadcast_to(jnp.concatenate([pos] * NSA_KV_HEADS, axis=-1)[:, :, None, :],
                           (2, CMP_LEN, 8, NSA_KV_HEADS * dh))
    return pos, w1, w2


def _nsa(proj, weights, tabs):
    tok_tabs, cmp_tabs = tabs
    qt, gt, ka, kw, vst, vwt = _nsa_prep(proj, tok_tabs)
    kc, vct = _nsa_compress(proj, *weights, cmp_tabs)
    return _nsa_attn(qt, gt, kc, vct, ka, vst, kw, vwt)


def _nsa_tables(s):
    nhalf = s // CMP_STRIDE
    tok_tabs = _nsa_rope_tables(jnp.arange(s))
    cmp_tabs = _nsa_rope_tables(jnp.arange(nhalf) * CMP_STRIDE + CMP_LEN - 1)
    return tok_tabs, cmp_tabs


def _out_kernel(x_ref, g_ref, yl_ref, ys_ref, yr_ref, yn_ref, wgm_ref, wb_ref, wo_ref,
                fg_ref, o_ref, *, final):
    x = x_ref[0]
    h = _rms(x, g_ref[...]).astype(MM)
    w = BRANCH_W
    d = D_MODEL
    m0 = N_BRANCH * w
    merged = jnp.zeros(x.shape, F32)
    for n, y_ref in enumerate((yl_ref, ys_ref, yr_ref, yn_ref)):
        gate = jax.nn.silu(_dot(h, wgm_ref[0, :, n * w:(n + 1) * w]))
        branch = _dot((y_ref[0] * gate).astype(MM), wb_ref[n])
        merged = merged + jax.nn.sigmoid(_dot(h, wgm_ref[0, :, m0 + n * d:m0 + (n + 1) * d])) * branch
    out = x + _dot(merged.astype(MM), wo_ref[...])
    if final:
        out = _rms(out, fg_ref[...])
    o_ref[0] = out


def _out(x, g, ys, wgm, l, wb, wo, fg, final):
    b, s, d = x.shape
    w = BRANCH_W
    tm = TM_PROJ
    row = lambda i, j: (i, j, 0)
    c2 = lambda i, j: (0, 0)
    once = pl.Buffered(1)
    return pl.pallas_call(
        functools.partial(_out_kernel, final=final),
        grid=(b, s // tm),
        in_specs=[pl.BlockSpec((1, tm, d), row),
                  pl.BlockSpec((1, d), c2)]
                 + [pl.BlockSpec((1, tm, w), row)] * N_BRANCH
                 + [pl.BlockSpec((1, d, N_BRANCH * (w + d)), lambda i, j: (l, 0, 0), pipeline_mode=once),
                    pl.BlockSpec((N_BRANCH, w, d), lambda i, j: (0, 0, 0), pipeline_mode=once),
                    pl.BlockSpec((d, d), c2, pipeline_mode=once),
                    pl.BlockSpec((1, d), c2)],
        out_specs=pl.BlockSpec((1, tm, d), row),
        out_shape=jax.ShapeDtypeStruct((b, s, d), F32),
        compiler_params=_cparams("parallel", "arbitrary"),
        name="gate_merge_out",
    )(x, g, *ys, wgm, wb, wo, fg)


def _split_w_in(w_in):
    pad = jnp.zeros(w_in.shape[:2] + (PROJ_W - GATE_OFF,), w_in.dtype)
    w_mix = jnp.concatenate([w_in[:, :, :GATE_OFF], pad], axis=2).astype(MM)
    return w_mix, w_in[:, :, GATE_OFF:].astype(MM)


def _layer(x, p, l, w_mix, w_gm, ret_tabs, nsa_tabs, final_g):
    g = p["norm_g"][l].reshape(1, D_MODEL)
    proj = _inproj(x, g, w_mix, l)

    wab = jnp.concatenate([_block_diag(p["lru_wa"][l]), _block_diag(p["lru_wx"][l])], axis=1).astype(MM)
    bab = jnp.concatenate([p["lru_ba"][l], p["lru_bx"][l]]).reshape(1, 2 * BRANCH_W)
    y_lru = _lru(proj, p["lru_conv_w"][l], p["lru_conv_b"][l].reshape(1, -1), wab, bab,
                 p["lru_lambda"][l].reshape(1, -1))

    wb, apr, api, wc = _s5_tables(p["s5_lambda_re"][l], p["s5_lambda_im"][l], p["s5_b_re"][l],
                                  p["s5_b_im"][l], p["s5_c_re"][l], p["s5_c_im"][l], p["s5_log_dt"][l])
    y_s5 = _s5(proj, wb, apr, api, wc, p["s5_d"][l].reshape(1, -1), p["s5_w_glu"][l].astype(MM),
               p["s5_b_glu"][l].reshape(1, -1))

    y_ret = _retention(proj, ret_tabs)

    nsa_w = _nsa_weights(p["nsa_pos_k"][l], p["nsa_pos_v"][l], p["nsa_ck_w1"][l], p["nsa_ck_w2"][l],
                         p["nsa_cv_w1"][l], p["nsa_cv_w2"][l])
    y_nsa = _nsa(proj, nsa_w, nsa_tabs)

    final = l == DEPTH - 1
    return _out(x, g, (y_lru, y_s5, y_ret, y_nsa), w_gm, l, p["w_branch"][l].astype(MM),
                p["w_out"][l].astype(MM), final_g.reshape(1, D_MODEL), final)


def kernel(x, norm_g, w_in, lru_conv_w, lru_conv_b, lru_wa, lru_ba, lru_wx, lru_bx, lru_lambda,
           s5_lambda_re, s5_lambda_im, s5_b_re, s5_b_im, s5_c_re, s5_c_im, s5_log_dt, s5_d,
           s5_w_glu, s5_b_glu, nsa_pos_k, nsa_pos_v, nsa_ck_w1, nsa_ck_w2, nsa_cv_w1, nsa_cv_w2,
           w_branch, w_out, final_norm_g):
    p = dict(norm_g=norm_g, w_in=w_in, lru_conv_w=lru_conv_w, lru_conv_b=lru_conv_b, lru_wa=lru_wa,
             lru_ba=lru_ba, lru_wx=lru_wx, lru_bx=lru_bx, lru_lambda=lru_lambda,
             s5_lambda_re=s5_lambda_re, s5_lambda_im=s5_lambda_im, s5_b_re=s5_b_re, s5_b_im=s5_b_im,
             s5_c_re=s5_c_re, s5_c_im=s5_c_im, s5_log_dt=s5_log_dt, s5_d=s5_d, s5_w_glu=s5_w_glu,
             s5_b_glu=s5_b_glu, nsa_pos_k=nsa_pos_k, nsa_pos_v=nsa_pos_v, nsa_ck_w1=nsa_ck_w1,
             nsa_ck_w2=nsa_ck_w2, nsa_cv_w1=nsa_cv_w1, nsa_cv_w2=nsa_cv_w2, w_branch=w_branch,
             w_out=w_out)
    s = x.shape[1]
    ret_tabs = _ret_tables(s)
    nsa_tabs = _nsa_tables(s)
    w_mix, w_gm = _split_w_in(w_in)
    for l in range(DEPTH):
        x = _layer(x, p, l, w_mix, w_gm, ret_tabs, nsa_tabs, final_norm_g)
    return x
```

```python
import functools
import math

import jax
import jax.numpy as jnp
from jax import lax
from jax.experimental import pallas as pl
from jax.experimental.pallas import tpu as pltpu

F32 = jnp.float32
MM = jnp.bfloat16

D_MODEL = 1024
DEPTH = 4
N_BRANCH = 4
BRANCH_W = 512
NORM_EPS = 1e-6
LRU_HEADS = 8
LRU_HD = BRANCH_W // LRU_HEADS
CONV_W = 4
LRU_C = 8.0
S5_GROUP = 16
S5_GROUPS = BRANCH_W // S5_GROUP
S5_STATE = 64
S5_N = S5_GROUPS * S5_STATE
RET_HEADS = 4
RET_HD = BRANCH_W // RET_HEADS
RET_CHUNK = 128
RET_ROPE_BASE = 10000.0
NSA_HEADS = 8
NSA_KV_HEADS = 2
NSA_HD = BRANCH_W // NSA_HEADS
NSA_GQA = NSA_HEADS // NSA_KV_HEADS
CMP_LEN = 32
CMP_STRIDE = 16
CMP_HIDDEN = 256
SLC_BLOCK = 64
SLC_TOPK = 16
WIN = 512
Q_BLOCK = 128
ROPE_THETA = 500000.0
ROPE_DIM = NSA_HD // 4
FORCE_SCORE = 1e4
NEG = -1e30
NSA_KV_W = 6 * NSA_KV_HEADS * NSA_HD
IN_SIZES = (BRANCH_W, BRANCH_W, 3 * BRANCH_W, NSA_HEADS * NSA_HD, NSA_KV_W, 3 * NSA_HEADS,
            N_BRANCH * BRANCH_W, N_BRANCH * D_MODEL)
LANE = 128
MIX_W = sum(IN_SIZES[:5])
BG_OFF = MIX_W
GATE_OFF = BG_OFF + IN_SIZES[5]
MERGE_OFF = GATE_OFF + IN_SIZES[6]
PROJ_W = MIX_W + LANE
VMEM_LIMIT = 56 * 1024 * 1024

TM_PROJ = 256
T_LRU = 256
T_S5 = 256
T_RET = 512
SEL_CHUNK = 512
NSA_T = SEL_CHUNK
V_ROWS = NSA_HD + 16


def _cparams(*sem):
    return pltpu.CompilerParams(dimension_semantics=sem, vmem_limit_bytes=VMEM_LIMIT)


def _rms(x, g):
    ms = jnp.mean(x * x, axis=-1, keepdims=True)
    return (x * lax.rsqrt(ms + NORM_EPS)) * g


def _dot(a, b):
    return jnp.dot(a, b, preferred_element_type=F32)


def _dot_nt(a, b):
    return lax.dot_general(a, b, (((1,), (1,)), ((), ())), preferred_element_type=F32)


def _dot_tn(a, b):
    return lax.dot_general(a, b, (((0,), (0,)), ((), ())), preferred_element_type=F32)


def _inproj_kernel(x_ref, g_ref, w_ref, o_ref):
    h = _rms(x_ref[0], g_ref[...])
    o_ref[0] = _dot(h.astype(MM), w_ref[0])


def _inproj(x, g, w, l):
    b, s, d = x.shape
    n = w.shape[2]
    return pl.pallas_call(
        _inproj_kernel,
        grid=(b, s // TM_PROJ),
        in_specs=[pl.BlockSpec((1, TM_PROJ, d), lambda i, j: (i, j, 0)),
                  pl.BlockSpec((1, d), lambda i, j: (0, 0)),
                  pl.BlockSpec((1, d, n), lambda i, j: (l, 0, 0))],
        out_specs=pl.BlockSpec((1, TM_PROJ, n), lambda i, j: (i, j, 0)),
        out_shape=jax.ShapeDtypeStruct((b, s, n), F32),
        compiler_params=_cparams("parallel", "arbitrary"),
        name="inproj",
    )(x, g, w)


def _lru_kernel(u_ref, cw_ref, cb_ref, wab_ref, bab_ref, lam_ref, o_ref, ubuf, hcar):
    t = T_LRU
    w = BRANCH_W

    @pl.when(pl.program_id(1) == 0)
    def _():
        ubuf[0:8, :] = jnp.zeros((8, w), F32)
        hcar[...] = jnp.zeros((1, w), F32)

    ubuf[8:8 + t, :] = u_ref[0]
    xc = cb_ref[...] + cw_ref[0:1, :] * ubuf[5:5 + t, :]
    for k in range(1, CONV_W):
        xc = xc + cw_ref[k:k + 1, :] * ubuf[5 + k:5 + k + t, :]
    ubuf[0:8, :] = ubuf[t:t + 8, :]

    ri = _dot(xc.astype(MM), wab_ref[...]) + bab_ref[...]
    r = jax.nn.sigmoid(ri[:, :w])
    gi = jax.nn.sigmoid(ri[:, w:])
    z = -lam_ref[...]
    softplus = jnp.maximum(z, 0.0) + jnp.log1p(jnp.exp(-jnp.abs(z)))
    a = jnp.exp(-LRU_C * r * softplus)
    bt = jnp.sqrt(1.0 - a * a) * gi * xc

    row = lax.broadcasted_iota(jnp.int32, (t, w), 0)
    k = 1
    while k < t:
        keep = row >= k
        a_sh = jnp.where(keep, pltpu.roll(a, k, 0), 1.0)
        b_sh = jnp.where(keep, pltpu.roll(bt, k, 0), 0.0)
        bt = a * b_sh + bt
        a = a * a_sh
        k *= 2
    h = bt + a * hcar[...]
    o_ref[0] = h
    hcar[...] = h[t - 1:t, :]


def _lru(proj, cw, cb, wab, bab, lam):
    b, s, _ = proj.shape
    w = BRANCH_W
    return pl.pallas_call(
        _lru_kernel,
        grid=(b, s // T_LRU),
        in_specs=[pl.BlockSpec((1, T_LRU, w), lambda i, j: (i, j, 0)),
                  pl.BlockSpec((CONV_W, w), lambda i, j: (0, 0)),
                  pl.BlockSpec((1, w), lambda i, j: (0, 0)),
                  pl.BlockSpec((w, 2 * w), lambda i, j: (0, 0)),
                  pl.BlockSpec((1, 2 * w), lambda i, j: (0, 0)),
                  pl.BlockSpec((1, w), lambda i, j: (0, 0))],
        out_specs=pl.BlockSpec((1, T_LRU, w), lambda i, j: (i, j, 0)),
        out_shape=jax.ShapeDtypeStruct((b, s, w), F32),
        scratch_shapes=[pltpu.VMEM((T_LRU + 8, w), F32), pltpu.VMEM((1, w), F32)],
        compiler_params=_cparams("parallel", "arbitrary"),
        name="rg_lru",
    )(proj, cw, cb, wab, bab, lam)


def _block_diag(blocks):
    n, r, c = blocks.shape
    eye = jnp.eye(n, dtype=blocks.dtype)
    return jnp.einsum("nrc,nm->nrmc", blocks, eye).reshape(n * r, n * c)


def _s5_kernel(u_ref, wb_ref, apr_ref, api_ref, wc_ref, d_ref, wg_ref, bg_ref, o_ref, car):
    t = T_S5
    n = S5_N
    w = BRANCH_W

    @pl.when(pl.program_id(1) == 0)
    def _():
        car[...] = jnp.zeros((2, n), F32)

    u = u_ref[0]
    x = _dot(u.astype(MM), wb_ref[...])
    xr = x[:, :n]
    xi = x[:, n:]
    row = lax.broadcasted_iota(jnp.int32, (t, n), 0)
    ar = apr_ref[0:1, :]
    ai = api_ref[0:1, :]
    cr = car[0:1, :]
    ci = car[1:2, :]
    first = row == 0
    xr = xr + jnp.where(first, ar * cr - ai * ci, 0.0)
    xi = xi + jnp.where(first, ar * ci + ai * cr, 0.0)
    k = 1
    step = 0
    while k < t:
        ar = apr_ref[step:step + 1, :]
        ai = api_ref[step:step + 1, :]
        keep = row >= k
        xr_s = jnp.where(keep, pltpu.roll(xr, k, 0), 0.0)
        xi_s = jnp.where(keep, pltpu.roll(xi, k, 0), 0.0)
        xr, xi = xr + (ar * xr_s - ai * xi_s), xi + (ar * xi_s + ai * xr_s)
        k *= 2
        step += 1
    car[0:1, :] = xr[t - 1:t, :]
    car[1:2, :] = xi[t - 1:t, :]
    y = _dot(xr.astype(MM), wc_ref[0:n, :]) + _dot(xi.astype(MM), wc_ref[n:2 * n, :])
    y = y + d_ref[...] * u
    z = jax.nn.gelu(y)
    g = _dot(z.astype(MM), wg_ref[...]) + bg_ref[...]
    o_ref[0] = g[:, :w] * jax.nn.sigmoid(g[:, w:])


def _s5_tables(lam_re, lam_im, b_re, b_im, c_re, c_im, log_dt):
    dt = jnp.exp(log_dt)[:, None]
    lr = jnp.minimum(lam_re, -1e-4)
    li = lam_im
    mag = jnp.exp(lr * dt)
    ab_re = mag * jnp.cos(li * dt)
    ab_im = mag * jnp.sin(li * dt)
    den = lr * lr + li * li
    nr = ab_re - 1.0
    ni = ab_im
    f_re = (nr * lr + ni * li) / den
    f_im = (ni * lr - nr * li) / den
    bb_re = f_re[..., None] * b_re - f_im[..., None] * b_im
    bb_im = f_re[..., None] * b_im + f_im[..., None] * b_re
    wb = jnp.concatenate([_block_diag(jnp.swapaxes(bb_re, 1, 2)),
                          _block_diag(jnp.swapaxes(bb_im, 1, 2))], axis=1)
    wc = jnp.concatenate([_block_diag(jnp.swapaxes(c_re, 1, 2)),
                          _block_diag(jnp.swapaxes(-c_im, 1, 2))], axis=0)
    pr = [ab_re.reshape(1, S5_N)]
    pi = [ab_im.reshape(1, S5_N)]
    for _ in range(int(math.log2(T_S5)) - 1):
        r0, i0 = pr[-1], pi[-1]
        pr.append(r0 * r0 - i0 * i0)
        pi.append(2.0 * r0 * i0)
    return wb.astype(MM), jnp.concatenate(pr, 0), jnp.concatenate(pi, 0), wc.astype(MM)


def _s5(proj, wb, apr, api, wc, d, wg, bg):
    b, s, _ = proj.shape
    w = BRANCH_W
    n = S5_N
    ns = apr.shape[0]
    return pl.pallas_call(
        _s5_kernel,
        grid=(b, s // T_S5),
        in_specs=[pl.BlockSpec((1, T_S5, w), lambda i, j: (i, j, 1)),
                  pl.BlockSpec((w, 2 * n), lambda i, j: (0, 0)),
                  pl.BlockSpec((ns, n), lambda i, j: (0, 0)),
                  pl.BlockSpec((ns, n), lambda i, j: (0, 0)),
                  pl.BlockSpec((2 * n, w), lambda i, j: (0, 0)),
                  pl.BlockSpec((1, w), lambda i, j: (0, 0)),
                  pl.BlockSpec((w, 2 * w), lambda i, j: (0, 0)),
                  pl.BlockSpec((1, 2 * w), lambda i, j: (0, 0))],
        out_specs=pl.BlockSpec((1, T_S5, w), lambda i, j: (i, j, 0)),
        out_shape=jax.ShapeDtypeStruct((b, s, w), F32),
        scratch_shapes=[pltpu.VMEM((2, n), F32)],
        compiler_params=_cparams("parallel", "arbitrary"),
        name="s5",
    )(proj, wb, apr, api, wc, d, wg, bg)


def _ret_kernel(q_ref, k_ref, v_ref, cos_ref, sin_ref, dm_ref, xi_ref, zt_ref, cd_ref, o_ref, rst):
    c = RET_CHUNK
    dh = RET_HD

    @pl.when(pl.program_id(1) == 0)
    def _():
        rst[...] = jnp.zeros((RET_HEADS, dh, dh), F32)

    for ci in range(T_RET // c):
        rows = slice(ci * c, (ci + 1) * c)
        cos = cos_ref[rows, :]
        sin = sin_ref[rows, :]
        for h in range(RET_HEADS):
            cols = slice(h * dh, (h + 1) * dh)
            q = q_ref[0, rows, cols]
            k = k_ref[0, rows, cols]
            v = v_ref[0, rows, cols]
            q = q * cos + pltpu.roll(q, dh // 2, 1) * sin
            k = (k * cos + pltpu.roll(k, dh // 2, 1) * sin) * (dh ** -0.5)
            qb = q.astype(MM)
            kb = k.astype(MM)
            vb = v.astype(MM)
            scores = _dot_nt(qb, kb) * dm_ref[h]
            inner = _dot(scores.astype(MM), vb)
            r_prev = rst[h]
            cross = _dot((q * xi_ref[h]).astype(MM), r_prev.astype(MM))
            kv = _dot_tn((k * zt_ref[h]).astype(MM), vb)
            rst[h] = cd_ref[h] * r_prev + kv
            o = inner + cross
            mu = jnp.mean(o, axis=-1, keepdims=True)
            var = jnp.mean(jnp.square(o - mu), axis=-1, keepdims=True)
            o_ref[0, rows, cols] = (o - mu) * lax.rsqrt(var + 1e-5)


def _ret_tables(s):
    dh = RET_HD
    c = RET_CHUNK
    half = dh // 2
    inv = RET_ROPE_BASE ** (-jnp.arange(half, dtype=F32) * 2.0 / dh)
    ang = jnp.arange(s).astype(F32)[:, None] * inv[None, :]
    cos = jnp.cos(ang)
    sin = jnp.sin(ang)
    cosf = jnp.concatenate([cos, cos], axis=1)
    sinf = jnp.concatenate([-sin, sin], axis=1)
    gamma = 1.0 - jnp.exp(jnp.linspace(math.log(1.0 / 32.0), math.log(1.0 / 512.0), RET_HEADS, dtype=F32))
    log_g = jnp.log(gamma)
    n = jnp.arange(c, dtype=F32)
    diff = n[:, None] - n[None, :]
    dmask = jnp.where(diff[None] >= 0, jnp.exp(jnp.maximum(diff, 0.0)[None] * log_g[:, None, None]), 0.0)
    xi = jnp.exp((n[None, :] + 1.0) * log_g[:, None])
    zeta = jnp.exp((c - 1.0 - n)[None, :] * log_g[:, None])
    cdec = jnp.exp(c * log_g)
    xi_b = jnp.broadcast_to(xi[:, :, None], (RET_HEADS, c, dh))
    zt_b = jnp.broadcast_to(zeta[:, :, None], (RET_HEADS, c, dh))
    cd_b = jnp.broadcast_to(cdec[:, None, None], (RET_HEADS, dh, dh))
    return cosf, sinf, dmask, xi_b, zt_b, cd_b


def _retention(proj, tabs):
    b, s, _ = proj.shape
    w = BRANCH_W
    cosf, sinf, dmask, xi_b, zt_b, cd_b = tabs
    c = RET_CHUNK
    dh = RET_HD
    full3 = lambda i, j: (0, 0, 0)
    return pl.pallas_call(
        _ret_kernel,
        grid=(b, s // T_RET),
        in_specs=[pl.BlockSpec((1, T_RET, w), lambda i, j: (i, j, 2)),
                  pl.BlockSpec((1, T_RET, w), lambda i, j: (i, j, 3)),
                  pl.BlockSpec((1, T_RET, w), lambda i, j: (i, j, 4)),
                  pl.BlockSpec((T_RET, dh), lambda i, j: (j, 0)),
                  pl.BlockSpec((T_RET, dh), lambda i, j: (j, 0)),
                  pl.BlockSpec((RET_HEADS, c, c), full3),
                  pl.BlockSpec((RET_HEADS, c, dh), full3),
                  pl.BlockSpec((RET_HEADS, c, dh), full3),
                  pl.BlockSpec((RET_HEADS, dh, dh), full3)],
        out_specs=pl.BlockSpec((1, T_RET, w), lambda i, j: (i, j, 0)),
        out_shape=jax.ShapeDtypeStruct((b, s, w), F32),
        scratch_shapes=[pltpu.VMEM((RET_HEADS, dh, dh), F32)],
        compiler_params=_cparams("parallel", "arbitrary"),
        name="retention",
    )(proj, proj, proj, cosf, sinf, dmask, xi_b, zt_b, cd_b)


def _rope_lanes(x, cos, s_up, s_dn):
    half = ROPE_DIM // 2
    n = x.shape[-1]
    return x * cos + pltpu.roll(x, half, 1) * s_up + pltpu.roll(x, n - half, 1) * s_dn


def _nsa_prep_kernel(q_ref, ks_ref, vs_ref, kw_ref, vw_ref, bg_ref, cos_ref, up_ref, dn_ref,
                     qt_ref, gt_ref, ka_ref, kwo_ref, vst_ref, vwt_ref):
    t = NSA_T
    qb = Q_BLOCK
    dh = NSA_HD
    cos = cos_ref[...]
    up = up_ref[...]
    dn = dn_ref[...]
    scale = NSA_HD ** -0.5 * math.log2(math.e)
    lane = lax.broadcasted_iota(jnp.int32, (t, LANE), 1)
    low = lane < dh
    tok = pl.program_id(1) * t + lax.broadcasted_iota(jnp.int32, (t, LANE), 0)
    onehot = jnp.where(tok // SLC_BLOCK == lane, 1.0, 0.0).astype(MM)
    ks = _rope_lanes(ks_ref[0], cos, up, dn)
    kw = _rope_lanes(kw_ref[0], cos, up, dn)
    for h in range(NSA_KV_HEADS):
        ks_h = ks if h == 0 else pltpu.roll(ks, LANE - h * dh, 1)
        kw_h = kw if h == 0 else pltpu.roll(kw, LANE - h * dh, 1)
        ka_ref[0, h, :, 0:LANE] = onehot
        ka_ref[0, h, :, LANE:2 * LANE] = jnp.where(low, ks_h, 0.0).astype(MM)
        kwo_ref[0, h] = jnp.where(low, kw_h, 0.0).astype(MM)

    def ones_row(n):
        return jnp.where(lax.broadcasted_iota(jnp.int32, (V_ROWS - dh, n), 0) == 0, 1.0, 0.0).astype(MM)

    r8 = lax.broadcasted_iota(jnp.int32, (8, NSA_GQA * qb), 0)
    for i in range(t // qb):
        rows = slice(i * qb, (i + 1) * qb)
        vs_t = vs_ref[0, rows, :].T
        vw_t = vw_ref[0, rows, :].T
        bg_t = bg_ref[0, rows, :].T
        for h in range(NSA_KV_HEADS):
            vst_ref[0, h, 0, 0:dh, rows] = vs_t[h * dh:(h + 1) * dh].astype(MM)
            vwt_ref[0, h, i, 0:dh, :] = vw_t[h * dh:(h + 1) * dh].astype(MM)
            vwt_ref[0, h, i, dh:V_ROWS, :] = ones_row(qb)
            qt_ref[0, h, i, dh:LANE, :] = jnp.zeros((LANE - dh, NSA_GQA * qb), F32)
            gates = jnp.zeros((8, NSA_GQA * qb), F32)
            for br in range(3):
                base = br * NSA_HEADS + h * NSA_GQA
                row = jnp.concatenate([bg_t[base + g:base + g + 1] for g in range(NSA_GQA)], axis=1)
                gates = jnp.where(r8 == br, row, gates)
            gt_ref[0, h, i] = gates
        for cb in range(BRANCH_W // LANE):
            cols = slice(cb * LANE, (cb + 1) * LANE)
            x_t = (_rope_lanes(q_ref[0, rows, cols], cos[rows], up[rows], dn[rows]) * scale).T
            h = (2 * cb) // NSA_GQA
            g0 = (2 * cb) % NSA_GQA
            qt_ref[0, h, i, 0:dh, g0 * qb:(g0 + 1) * qb] = x_t[0:dh]
            qt_ref[0, h, i, 0:dh, (g0 + 1) * qb:(g0 + 2) * qb] = x_t[dh:2 * dh]
    for h in range(NSA_KV_HEADS):
        vst_ref[0, h, 0, dh:V_ROWS, :] = ones_row(t)


def _nsa_rope_tables(pos):
    half = ROPE_DIM // 2
    inv = ROPE_THETA ** (-jnp.arange(half, dtype=F32) * 2.0 / ROPE_DIM)
    ang = pos.astype(F32)[:, None] * inv[None, :]
    cos = jnp.cos(ang)
    sin = jnp.sin(ang)
    n = pos.shape[0]
    zeros = jnp.zeros((n, NSA_HD - ROPE_DIM), F32)
    zh = jnp.zeros((n, half), F32)
    cos_h = jnp.concatenate([cos, cos, jnp.ones((n, NSA_HD - ROPE_DIM), F32)], axis=1)
    up_h = jnp.concatenate([zh, sin, zeros], axis=1)
    dn_h = jnp.concatenate([-sin, zh, zeros], axis=1)
    tile2 = lambda a: jnp.concatenate([a, a], axis=1)
    return tile2(cos_h), tile2(up_h), tile2(dn_h)


def _nsa_prep(proj, tabs):
    b, s, _ = proj.shape
    t = NSA_T
    qb = Q_BLOCK
    hkv = NSA_KV_HEADS
    cols = NSA_GQA * qb
    assert t == SEL_CHUNK and s // SLC_BLOCK <= LANE
    q_blk = sum(IN_SIZES[:3]) // BRANCH_W
    kv_blk = sum(IN_SIZES[:4]) // LANE
    lane_blk = lambda n: pl.BlockSpec((1, t, LANE), lambda i, j: (i, j, n))
    tab_spec = pl.BlockSpec((t, LANE), lambda i, j: (j, 0))
    tile4 = lambda i, j: (i, 0, j, 0)
    tile5 = lambda i, j: (i, 0, j, 0, 0)
    return pl.pallas_call(
        _nsa_prep_kernel,
        grid=(b, s // t),
        in_specs=[pl.BlockSpec((1, t, BRANCH_W), lambda i, j: (i, j, q_blk)),
                  lane_blk(kv_blk + 2), lane_blk(kv_blk + 3), lane_blk(kv_blk + 4), lane_blk(kv_blk + 5),
                  lane_blk(BG_OFF // LANE), tab_spec, tab_spec, tab_spec],
        out_specs=[pl.BlockSpec((1, hkv, t // qb, LANE, cols), tile5),
                   pl.BlockSpec((1, hkv, t // qb, 8, cols), tile5),
                   pl.BlockSpec((1, hkv, t, 2 * LANE), tile4),
                   pl.BlockSpec((1, hkv, t, LANE), tile4),
                   pl.BlockSpec((1, hkv, 1, V_ROWS, t), tile5),
                   pl.BlockSpec((1, hkv, t // qb, V_ROWS, qb), tile5)],
        out_shape=[jax.ShapeDtypeStruct((b, hkv, s // qb, LANE, cols), F32),
                   jax.ShapeDtypeStruct((b, hkv, s // qb, 8, cols), F32),
                   jax.ShapeDtypeStruct((b, hkv, s, 2 * LANE), MM),
                   jax.ShapeDtypeStruct((b, hkv, s, LANE), MM),
                   jax.ShapeDtypeStruct((b, hkv, s // t, V_ROWS, t), MM),
                   jax.ShapeDtypeStruct((b, hkv, s // qb, V_ROWS, qb), MM)],
        compiler_params=_cparams("parallel", "parallel"),
        name="nsa_prep",
    )(proj, proj, proj, proj, proj, proj, *tabs)


def _cmp_kernel(xk_ref, xv_ref, pos_ref, w1_ref, w2_ref, cos_ref, up_ref, dn_ref, kc_ref, vct_ref):
    nh = kc_ref.shape[2]
    hid_w = NSA_KV_HEADS * CMP_HIDDEN
    for kind, x_ref in enumerate((xk_ref, xv_ref)):
        lo = jnp.zeros((nh, hid_w), F32)
        hi = jnp.zeros((nh, hid_w), F32)
        posb = jnp.zeros((8, hid_w), F32)
        for j in range(CMP_STRIDE):
            xj = x_ref[0, pl.ds(j, nh, stride=CMP_STRIDE), :].astype(MM)
            lo = lo + _dot(xj, w1_ref[kind, j])
            hi = hi + _dot(xj, w1_ref[kind, CMP_STRIDE + j])
        for j in range(CMP_LEN):
            posb = posb + _dot(pos_ref[kind, j].astype(MM), w1_ref[kind, j])
        hid = jax.nn.gelu(lo + pltpu.roll(hi, nh - 1, 0) + posb[0:1, :])
        for h in range(NSA_KV_HEADS):
            out = _dot(hid[:, h * CMP_HIDDEN:(h + 1) * CMP_HIDDEN].astype(MM), w2_ref[kind])
            if kind == 0:
                kc_ref[0, h] = _rope_lanes(out, cos_ref[...], up_ref[...], dn_ref[...])
            else:
                vct_ref[0, h] = out.T.astype(MM)


def _nsa_compress(proj, pos, w1, w2, tabs):
    b, s, _ = proj.shape
    nh = s // CMP_STRIDE
    hkv = NSA_KV_HEADS
    kv_blk = sum(IN_SIZES[:4]) // LANE
    once = pl.Buffered(1)
    c2 = lambda i: (0, 0)
    c3 = lambda i: (0, 0, 0)
    c4 = lambda i: (0, 0, 0, 0)
    return pl.pallas_call(
        _cmp_kernel,
        grid=(b,),
        in_specs=[pl.BlockSpec((1, s, LANE), lambda i: (i, 0, kv_blk)),
                  pl.BlockSpec((1, s, LANE), lambda i: (i, 0, kv_blk + 1)),
                  pl.BlockSpec(pos.shape, c4),
                  pl.BlockSpec(w1.shape, c4, pipeline_mode=once),
                  pl.BlockSpec(w2.shape, c3),
                  pl.BlockSpec((nh, LANE), c2), pl.BlockSpec((nh, LANE), c2), pl.BlockSpec((nh, LANE), c2)],
        out_specs=[pl.BlockSpec((1, hkv, nh, LANE), lambda i: (i, 0, 0, 0)),
                   pl.BlockSpec((1, hkv, LANE, nh), lambda i: (i, 0, 0, 0))],
        out_shape=[jax.ShapeDtypeStruct((b, hkv, nh, LANE), F32),
                   jax.ShapeDtypeStruct((b, hkv, LANE, nh), MM)],
        compiler_params=_cparams("parallel"),
        name="nsa_compress",
    )(proj, proj, pos, w1, w2, *tabs)


def _softmax_update(s, vt, m, acc):
    m_new = jnp.maximum(m, jnp.max(s, axis=0, keepdims=True))
    alpha = jnp.exp2(m - m_new)
    p = jnp.exp2(s - m_new)
    acc = alpha * acc + _dot(vt, p.astype(MM))
    return m_new, acc


def _normalise(acc):
    return acc[:NSA_HD, :] * (1.0 / acc[NSA_HD:NSA_HD + 1, :])


def _nsa_attn_kernel(q_ref, gt_ref, kc_ref, vc_ref, ka_ref, vs_ref, kw_ref, vw_ref, o_ref,
                     ps_ref, sa_ref, sb_ref):
    blk = pl.program_id(2)
    qb = Q_BLOCK
    cols = NSA_GQA * qb
    ncmp_pad = kc_ref.shape[2]
    nsel = LANE
    qf = q_ref[0, 0, 0]
    qm = qf.astype(MM)
    t = blk * qb + (lax.broadcasted_iota(jnp.int32, (1, cols), 1) & (qb - 1))
    lower = lax.broadcasted_iota(jnp.int32, (qb, 1), 0) <= (t & (qb - 1))

    s_w = []
    v_w = []
    for back in range(WIN // qb, -1, -1):
        c = blk - back
        cc = jnp.maximum(c, 0)
        off = pl.multiple_of(cc * qb, qb)
        s = _dot(kw_ref[0, 0, pl.ds(off, qb), :], qm)
        if back == WIN // qb:
            s = jnp.where(lower | (c < 0), NEG, s)
        elif back == 0:
            s = jnp.where(lower, s, NEG)
        else:
            s = jnp.where(c < 0, NEG, s)
        s_w.append(s)
        v_w.append(vw_ref[0, 0, cc])
    s_w = jnp.concatenate(s_w, axis=0)
    p_w = jnp.exp2(s_w - jnp.max(s_w, axis=0, keepdims=True))
    o_w = _normalise(_dot(jnp.concatenate(v_w, axis=1), p_w.astype(MM)))

    s_c = _dot(kc_ref[0, 0], qf)
    n_id = lax.broadcasted_iota(jnp.int32, (ncmp_pad, 1), 0)
    m_c = (n_id * CMP_STRIDE + (CMP_LEN - 1) <= t) & (n_id < ncmp_pad - 1)
    smax = jnp.max(jnp.where(m_c, s_c, NEG), axis=0, keepdims=True)
    e_c = jnp.where(m_c, jnp.exp2(s_c - smax), 0.0)
    den = jnp.sum(e_c, axis=0, keepdims=True)
    p_c = e_c * (1.0 / jnp.where(den > 0.0, den, 1.0))
    o_c = _dot(vc_ref[0, 0], p_c.astype(MM))[:NSA_HD, :]

    r = SLC_BLOCK // CMP_STRIDE
    nrow = ncmp_pad // r
    ps_ref[...] = p_c[:, 0:qb] + p_c[:, qb:2 * qb] + p_c[:, 2 * qb:3 * qb] + p_c[:, 3 * qb:4 * qb]
    imp = ps_ref[pl.ds(0, nrow, stride=r), :]
    for k in range(1, r - 1):
        imp = imp + ps_ref[pl.ds(k, nrow, stride=r), :]
    edge = 0.5 * ps_ref[pl.ds(r - 1, nrow, stride=r), :]
    first = lax.broadcasted_iota(jnp.int32, (nrow, qb), 0) == 0
    imp = imp + edge + jnp.where(first, 0.0, pltpu.roll(edge, 1, 0))
    if nrow < nsel:
        imp = jnp.concatenate([imp, jnp.zeros((nsel - nrow, qb), F32)], axis=0)

    cur = (blk * qb + lax.broadcasted_iota(jnp.int32, (1, qb), 1)) // SLC_BLOCK
    jid = lax.broadcasted_iota(jnp.int32, (nsel, qb), 0)
    valid = jid <= cur
    forced = (jid == 0) | (jid == cur) | (jid == cur - 1)
    score0 = jnp.where(valid, jnp.where(forced, FORCE_SCORE, imp), -1.0)
    score = score0
    for _ in range(SLC_TOPK):
        mx = jnp.max(score, axis=0, keepdims=True)
        idx = jnp.min(jnp.where(score == mx, jid, nsel), axis=0, keepdims=True)
        score = jnp.where(jid == idx, -3e38, score)
    bias = jnp.where((score < -1e38) & (score0 >= 0.0), 0.0, NEG).astype(MM)
    q_aug = jnp.concatenate([jnp.concatenate([bias] * NSA_GQA, axis=1), qm], axis=0)

    kb = SEL_CHUNK
    nchunk = ka_ref.shape[2] // kb
    diag = blk * qb

    def scores_to(s_ref, j):
        jc = jnp.minimum(j, nchunk - 1)
        off = pl.multiple_of(jc * kb, kb)
        s_ref[...] = _dot(ka_ref[0, 0, pl.ds(off, kb), :], q_aug)
        here = (diag >= j * kb) & (diag < (j + 1) * kb)
        row0 = pl.multiple_of(jnp.where(here, diag - j * kb, 0), qb)
        slab = s_ref[pl.ds(row0, qb), :]
        s_ref[pl.ds(row0, qb), :] = jnp.where(lower | jnp.logical_not(here), slab, NEG)

    def update_from(s_ref, j, m, acc):
        jc = jnp.minimum(j, nchunk - 1)
        return _softmax_update(s_ref[...], vs_ref[0, 0, jc], m, acc)

    def pair_body(i, carry):
        scores_to(sb_ref, 2 * i + 1)
        carry = update_from(sa_ref, 2 * i, *carry)
        scores_to(sa_ref, 2 * i + 2)
        return update_from(sb_ref, 2 * i + 1, *carry)

    nfull = diag // kb
    scores_to(sa_ref, 0)
    init = (jnp.full((1, cols), NEG, F32), jnp.zeros((V_ROWS, cols), F32))
    o_s = _normalise(lax.fori_loop(0, (nfull + 2) // 2, pair_body, init)[1])

    g = jax.nn.sigmoid(gt_ref[0, 0, 0])
    o = g[0:1, :] * o_c + g[1:2, :] * o_s + g[2:3, :] * o_w
    for gp in range(NSA_GQA // 2):
        pair = jnp.concatenate([o[:, 2 * gp * qb:(2 * gp + 1) * qb],
                                o[:, (2 * gp + 1) * qb:(2 * gp + 2) * qb]], axis=0)
        o_ref[0, :, gp * LANE:(gp + 1) * LANE] = pair.T


def _nsa_attn(qt, gt, kc, vct, ka, vst, kw, vwt):
    b, hkv, nqb, _, cols = qt.shape
    s = ka.shape[2]
    ncmp_pad = kc.shape[2]
    assert CMP_LEN == 2 * CMP_STRIDE and SLC_BLOCK % CMP_STRIDE == 0 and s // SLC_BLOCK <= LANE
    assert (s // SEL_CHUNK) % 2 == 0
    per_head = lambda i, h, j: (i, h, 0, 0)
    per_head5 = lambda i, h, j: (i, h, 0, 0, 0)
    per_blk = lambda i, h, j: (i, h, j, 0, 0)
    return pl.pallas_call(
        _nsa_attn_kernel,
        grid=(b, hkv, nqb),
        in_specs=[pl.BlockSpec((1, 1, 1, LANE, cols), per_blk),
                  pl.BlockSpec((1, 1, 1, 8, cols), per_blk),
                  pl.BlockSpec((1, 1, ncmp_pad, LANE), per_head),
                  pl.BlockSpec((1, 1, LANE, ncmp_pad), per_head),
                  pl.BlockSpec((1, 1, s, 2 * LANE), per_head),
                  pl.BlockSpec((1, 1, s // SEL_CHUNK, V_ROWS, SEL_CHUNK), per_head5),
                  pl.BlockSpec((1, 1, s, LANE), per_head),
                  pl.BlockSpec((1, 1, s // Q_BLOCK, V_ROWS, Q_BLOCK), per_head5)],
        out_specs=pl.BlockSpec((1, Q_BLOCK, NSA_GQA * NSA_HD), lambda i, h, j: (i, j, h)),
        out_shape=jax.ShapeDtypeStruct((b, s, hkv * NSA_GQA * NSA_HD), F32),
        scratch_shapes=[pltpu.VMEM((ncmp_pad, Q_BLOCK), F32),
                        pltpu.VMEM((SEL_CHUNK, cols), F32), pltpu.VMEM((SEL_CHUNK, cols), F32)],
        compiler_params=_cparams("parallel", "parallel", "arbitrary"),
        name="nsa_attn",
    )(qt, gt, kc, vct, ka, vst, kw, vwt)


def _nsa_weights(pos_k, pos_v, ck_w1, ck_w2, cv_w1, cv_w2):
    dh = NSA_HD
    eye = jnp.eye(NSA_KV_HEADS, dtype=F32)
    slabs = lambda w: jnp.einsum("jrc,hg->jhrgc", w.reshape(CMP_LEN, dh, CMP_HIDDEN), eye).reshape(
        CMP_LEN, NSA_KV_HEADS * dh, NSA_KV_HEADS * CMP_HIDDEN)
    w1 = jnp.stack([slabs(ck_w1), slabs(cv_w1)]).astype(MM)
    w2 = jnp.pad(jnp.stack([ck_w2, cv_w2]), ((0, 0), (0, 0), (0, LANE - dh))).astype(MM)
    pos = jnp.stack([pos_k, pos_v])
    pos = jnp.broadcast_to(jnp.concatenate([pos] * NSA_KV_HEADS, axis=-1)[:, :, None, :],
                           (2, CMP_LEN, 8, NSA_KV_HEADS * dh))
    return pos, w1, w2


def _nsa(proj, weights, tabs):
    tok_tabs, cmp_tabs = tabs
    qt, gt, ka, kw, vst, vwt = _nsa_prep(proj, tok_tabs)
    kc, vct = _nsa_compress(proj, *weights, cmp_tabs)
    return _nsa_attn(qt, gt, kc, vct, ka, vst, kw, vwt)


def _nsa_tables(s):
    nhalf = s // CMP_STRIDE
    tok_tabs = _nsa_rope_tables(jnp.arange(s))
    cmp_tabs = _nsa_rope_tables(jnp.arange(nhalf) * CMP_STRIDE + CMP_LEN - 1)
    return tok_tabs, cmp_tabs


def _out_kernel(x_ref, g_ref, yl_ref, ys_ref, yr_ref, yn_ref, wgm_ref, wb_ref, wo_ref,
                fg_ref, o_ref, *, final):
    x = x_ref[0]
    h = _rms(x, g_ref[...]).astype(MM)
    w = BRANCH_W
    d = D_MODEL
    m0 = N_BRANCH * w
    merged = jnp.zeros(x.shape, F32)
    for n, y_ref in enumerate((yl_ref, ys_ref, yr_ref, yn_ref)):
        gate = jax.nn.silu(_dot(h, wgm_ref[0, :, n * w:(n + 1) * w]))
        branch = _dot((y_ref[0] * gate).astype(MM), wb_ref[n])
        merged = merged + jax.nn.sigmoid(_dot(h, wgm_ref[0, :, m0 + n * d:m0 + (n + 1) * d])) * branch
    out = x + _dot(merged.astype(MM), wo_ref[...])
    if final:
        out = _rms(out, fg_ref[...])
    o_ref[0] = out


def _out(x, g, ys, wgm, l, wb, wo, fg, final):
    b, s, d = x.shape
    w = BRANCH_W
    tm = TM_PROJ
    row = lambda i, j: (i, j, 0)
    c2 = lambda i, j: (0, 0)
    once = pl.Buffered(1)
    return pl.pallas_call(
        functools.partial(_out_kernel, final=final),
        grid=(b, s // tm),
        in_specs=[pl.BlockSpec((1, tm, d), row),
                  pl.BlockSpec((1, d), c2)]
                 + [pl.BlockSpec((1, tm, w), row)] * N_BRANCH
                 + [pl.BlockSpec((1, d, N_BRANCH * (w + d)), lambda i, j: (l, 0, 0), pipeline_mode=once),
                    pl.BlockSpec((N_BRANCH, w, d), lambda i, j: (0, 0, 0), pipeline_mode=once),
                    pl.BlockSpec((d, d), c2, pipeline_mode=once),
                    pl.BlockSpec((1, d), c2)],
        out_specs=pl.BlockSpec((1, tm, d), row),
        out_shape=jax.ShapeDtypeStruct((b, s, d), F32),
        compiler_params=_cparams("parallel", "arbitrary"),
        name="gate_merge_out",
    )(x, g, *ys, wgm, wb, wo, fg)


def _split_w_in(w_in):
    pad = jnp.zeros(w_in.shape[:2] + (PROJ_W - GATE_OFF,), w_in.dtype)
    w_mix = jnp.concatenate([w_in[:, :, :GATE_OFF], pad], axis=2).astype(MM)
    return w_mix, w_in[:, :, GATE_OFF:].astype(MM)


def _layer(x, p, l, w_mix, w_gm, ret_tabs, nsa_tabs, final_g):
    g = p["norm_g"][l].reshape(1, D_MODEL)
    proj = _inproj(x, g, w_mix, l)

    wab = jnp.concatenate([_block_diag(p["lru_wa"][l]), _block_diag(p["lru_wx"][l])], axis=1).astype(MM)
    bab = jnp.concatenate([p["lru_ba"][l], p["lru_bx"][l]]).reshape(1, 2 * BRANCH_W)
    y_lru = _lru(proj, p["lru_conv_w"][l], p["lru_conv_b"][l].reshape(1, -1), wab, bab,
                 p["lru_lambda"][l].reshape(1, -1))

    wb, apr, api, wc = _s5_tables(p["s5_lambda_re"][l], p["s5_lambda_im"][l], p["s5_b_re"][l],
                                  p["s5_b_im"][l], p["s5_c_re"][l], p["s5_c_im"][l], p["s5_log_dt"][l])
    y_s5 = _s5(proj, wb, apr, api, wc, p["s5_d"][l].reshape(1, -1), p["s5_w_glu"][l].astype(MM),
               p["s5_b_glu"][l].reshape(1, -1))

    y_ret = _retention(proj, ret_tabs)

    nsa_w = _nsa_weights(p["nsa_pos_k"][l], p["nsa_pos_v"][l], p["nsa_ck_w1"][l], p["nsa_ck_w2"][l],
                         p["nsa_cv_w1"][l], p["nsa_cv_w2"][l])
    y_nsa = _nsa(proj, nsa_w, nsa_tabs)

    final = l == DEPTH - 1
    return _out(x, g, (y_lru, y_s5, y_ret, y_nsa), w_gm, l, p["w_branch"][l].astype(MM),
                p["w_out"][l].astype(MM), final_g.reshape(1, D_MODEL), final)


def kernel(x, norm_g, w_in, lru_conv_w, lru_conv_b, lru_wa, lru_ba, lru_wx, lru_bx, lru_lambda,
           s5_lambda_re, s5_lambda_im, s5_b_re, s5_b_im, s5_c_re, s5_c_im, s5_log_dt, s5_d,
           s5_w_glu, s5_b_glu, nsa_pos_k, nsa_pos_v, nsa_ck_w1, nsa_ck_w2, nsa_cv_w1, nsa_cv_w2,
           w_branch, w_out, final_norm_g):
    p = dict(norm_g=norm_g, w_in=w_in, lru_conv_w=lru_conv_w, lru_conv_b=lru_conv_b, lru_wa=lru_wa,
             lru_ba=lru_ba, lru_wx=lru_wx, lru_bx=lru_bx, lru_lambda=lru_lambda,
             s5_lambda_re=s5_lambda_re, s5_lambda_im=s5_lambda_im, s5_b_re=s5_b_re, s5_b_im=s5_b_im,
             s5_c_re=s5_c_re, s5_c_im=s5_c_im, s5_log_dt=s5_log_dt, s5_d=s5_d, s5_w_glu=s5_w_glu,
             s5_b_glu=s5_b_glu, nsa_pos_k=nsa_pos_k, nsa_pos_v=nsa_pos_v, nsa_ck_w1=nsa_ck_w1,
             nsa_ck_w2=nsa_ck_w2, nsa_cv_w1=nsa_cv_w1, nsa_cv_w2=nsa_cv_w2, w_branch=w_branch,
             w_out=w_out)
    s = x.shape[1]
    ret_tabs = _ret_tables(s)
    nsa_tabs = _nsa_tables(s)
    w_mix, w_gm = _split_w_in(w_in)
    for l in range(DEPTH):
        x = _layer(x, p, l, w_mix, w_gm, ret_tabs, nsa_tabs, final_norm_g)
    return x
```

```python
import functools
import math

import jax
import jax.numpy as jnp
from jax import lax
from jax.experimental import pallas as pl
from jax.experimental.pallas import tpu as pltpu

F32 = jnp.float32
MM = jnp.bfloat16

D_MODEL = 1024
DEPTH = 4
N_BRANCH = 4
BRANCH_W = 512
NORM_EPS = 1e-6
LRU_HEADS = 8
LRU_HD = BRANCH_W // LRU_HEADS
CONV_W = 4
LRU_C = 8.0
S5_GROUP = 16
S5_GROUPS = BRANCH_W // S5_GROUP
S5_STATE = 64
S5_N = S5_GROUPS * S5_STATE
RET_HEADS = 4
RET_HD = BRANCH_W // RET_HEADS
RET_CHUNK = 128
RET_ROPE_BASE = 10000.0
NSA_HEADS = 8
NSA_KV_HEADS = 2
NSA_HD = BRANCH_W // NSA_HEADS
NSA_GQA = NSA_HEADS // NSA_KV_HEADS
CMP_LEN = 32
CMP_STRIDE = 16
CMP_HIDDEN = 256
SLC_BLOCK = 64
SLC_TOPK = 16
WIN = 512
Q_BLOCK = 128
ROPE_THETA = 500000.0
ROPE_DIM = NSA_HD // 4
FORCE_SCORE = 1e4
NEG = -1e30
NSA_KV_W = 6 * NSA_KV_HEADS * NSA_HD
IN_SIZES = (BRANCH_W, BRANCH_W, 3 * BRANCH_W, NSA_HEADS * NSA_HD, NSA_KV_W, 3 * NSA_HEADS,
            N_BRANCH * BRANCH_W, N_BRANCH * D_MODEL)
LANE = 128
MIX_W = sum(IN_SIZES[:5])
BG_OFF = MIX_W
GATE_OFF = BG_OFF + IN_SIZES[5]
MERGE_OFF = GATE_OFF + IN_SIZES[6]
PROJ_W = MIX_W + LANE
VMEM_LIMIT = 56 * 1024 * 1024

TM_PROJ = 256
T_LRU = 256
T_S5 = 256
T_RET = 512
SEL_CHUNK = 512
NSA_T = SEL_CHUNK
V_ROWS = NSA_HD + 16


def _cparams(*sem):
    return pltpu.CompilerParams(dimension_semantics=sem, vmem_limit_bytes=VMEM_LIMIT)


def _rms(x, g):
    ms = jnp.mean(x * x, axis=-1, keepdims=True)
    return (x * lax.rsqrt(ms + NORM_EPS)) * g


def _dot(a, b):
    return jnp.dot(a, b, preferred_element_type=F32)


def _dot_nt(a, b):
    return lax.dot_general(a, b, (((1,), (1,)), ((), ())), preferred_element_type=F32)


def _dot_tn(a, b):
    return lax.dot_general(a, b, (((0,), (0,)), ((), ())), preferred_element_type=F32)


def _inproj_kernel(x_ref, g_ref, w_ref, o_ref):
    h = _rms(x_ref[0], g_ref[...])
    o_ref[0] = _dot(h.astype(MM), w_ref[0])


def _inproj(x, g, w, l):
    b, s, d = x.shape
    n = w.shape[2]
    return pl.pallas_call(
        _inproj_kernel,
        grid=(b, s // TM_PROJ),
        in_specs=[pl.BlockSpec((1, TM_PROJ, d), lambda i, j: (i, j, 0)),
                  pl.BlockSpec((1, d), lambda i, j: (0, 0)),
                  pl.BlockSpec((1, d, n), lambda i, j: (l, 0, 0))],
        out_specs=pl.BlockSpec((1, TM_PROJ, n), lambda i, j: (i, j, 0)),
        out_shape=jax.ShapeDtypeStruct((b, s, n), F32),
        compiler_params=_cparams("parallel", "arbitrary"),
        name="inproj",
    )(x, g, w)


def _lru_kernel(u_ref, cw_ref, cb_ref, wab_ref, bab_ref, lam_ref, o_ref, ubuf, hcar):
    t = T_LRU
    w = BRANCH_W

    @pl.when(pl.program_id(1) == 0)
    def _():
        ubuf[0:8, :] = jnp.zeros((8, w), F32)
        hcar[...] = jnp.zeros((1, w), F32)

    ubuf[8:8 + t, :] = u_ref[0]
    xc = cb_ref[...] + cw_ref[0:1, :] * ubuf[5:5 + t, :]
    for k in range(1, CONV_W):
        xc = xc + cw_ref[k:k + 1, :] * ubuf[5 + k:5 + k + t, :]
    ubuf[0:8, :] = ubuf[t:t + 8, :]

    ri = _dot(xc.astype(MM), wab_ref[...]) + bab_ref[...]
    r = jax.nn.sigmoid(ri[:, :w])
    gi = jax.nn.sigmoid(ri[:, w:])
    z = -lam_ref[...]
    softplus = jnp.maximum(z, 0.0) + jnp.log1p(jnp.exp(-jnp.abs(z)))
    a = jnp.exp(-LRU_C * r * softplus)
    bt = jnp.sqrt(1.0 - a * a) * gi * xc

    row = lax.broadcasted_iota(jnp.int32, (t, w), 0)
    k = 1
    while k < t:
        keep = row >= k
        a_sh = jnp.where(keep, pltpu.roll(a, k, 0), 1.0)
        b_sh = jnp.where(keep, pltpu.roll(bt, k, 0), 0.0)
        bt = a * b_sh + bt
        a = a * a_sh
        k *= 2
    h = bt + a * hcar[...]
    o_ref[0] = h
    hcar[...] = h[t - 1:t, :]


def _lru(proj, cw, cb, wab, bab, lam):
    b, s, _ = proj.shape
    w = BRANCH_W
    return pl.pallas_call(
        _lru_kernel,
        grid=(b, s // T_LRU),
        in_specs=[pl.BlockSpec((1, T_LRU, w), lambda i, j: (i, j, 0)),
                  pl.BlockSpec((CONV_W, w), lambda i, j: (0, 0)),
                  pl.BlockSpec((1, w), lambda i, j: (0, 0)),
                  pl.BlockSpec((w, 2 * w), lambda i, j: (0, 0)),
                  pl.BlockSpec((1, 2 * w), lambda i, j: (0, 0)),
                  pl.BlockSpec((1, w), lambda i, j: (0, 0))],
        out_specs=pl.BlockSpec((1, T_LRU, w), lambda i, j: (i, j, 0)),
        out_shape=jax.ShapeDtypeStruct((b, s, w), F32),
        scratch_shapes=[pltpu.VMEM((T_LRU + 8, w), F32), pltpu.VMEM((1, w), F32)],
        compiler_params=_cparams("parallel", "arbitrary"),
        name="rg_lru",
    )(proj, cw, cb, wab, bab, lam)


def _block_diag(blocks):
    n, r, c = blocks.shape
    eye = jnp.eye(n, dtype=blocks.dtype)
    return jnp.einsum("nrc,nm->nrmc", blocks, eye).reshape(n * r, n * c)


def _cmul_add(xr, xi, ar, ai, br, bi):
    return xr + (ar * br - ai * bi), xi + (ar * bi + ai * br)


def _s5_kernel(u_ref, wb_ref, apr_ref, api_ref, wc_ref, d_ref, wg_ref, bg_ref, o_ref,
               car, lr_ref, li_ref, zr_ref, zi_ref):
    t = T_S5
    n = S5_N
    w = BRANCH_W
    sub = 8
    grp = t // sub

    @pl.when(pl.program_id(1) == 0)
    def _():
        car[...] = jnp.zeros((2, n), F32)

    u = u_ref[0]
    x = _dot(u.astype(MM), wb_ref[...])
    first = lax.broadcasted_iota(jnp.int32, (t, n), 0) == 0
    cr, ci = _cmul_add(0.0, 0.0, apr_ref[0:1, :], api_ref[0:1, :], car[0:1, :], car[1:2, :])
    xr = (x[:, :n] + jnp.where(first, cr, 0.0)).reshape(grp, sub, n)
    xi = (x[:, n:] + jnp.where(first, ci, 0.0)).reshape(grp, sub, n)

    i8 = lax.broadcasted_iota(jnp.int32, (sub, n), 0)
    k = 1
    while k < sub:
        ar = jnp.where(i8 >= k, apr_ref[k - 1:k, :], 0.0)[None]
        ai = jnp.where(i8 >= k, api_ref[k - 1:k, :], 0.0)[None]
        xr, xi = _cmul_add(xr, xi, ar, ai, pltpu.roll(xr, k, 1), pltpu.roll(xi, k, 1))
        k *= 2
    nlb = n // LANE
    xr = xr.reshape(t, n)
    xi = xi.reshape(t, n)
    for cb in range(nlb):
        lr_ref[cb] = xr[:, cb * LANE:(cb + 1) * LANE]
        li_ref[cb] = xi[:, cb * LANE:(cb + 1) * LANE]
    last = pl.ds(sub - 1, grp, stride=sub)
    fr = jnp.concatenate([lr_ref[cb, last, :] for cb in range(nlb)], axis=1)
    fi = jnp.concatenate([li_ref[cb, last, :] for cb in range(nlb)], axis=1)
    rowg = lax.broadcasted_iota(jnp.int32, (grp, n), 0)
    k = 1
    step = sub - 1
    while k < grp:
        keep = rowg >= k
        sr = jnp.where(keep, pltpu.roll(fr, k, 0), 0.0)
        si = jnp.where(keep, pltpu.roll(fi, k, 0), 0.0)
        fr, fi = _cmul_add(fr, fi, apr_ref[step:step + 1, :], api_ref[step:step + 1, :], sr, si)
        k *= 2
        step += 1
    car[0:1, :] = fr[grp - 1:grp, :]
    car[1:2, :] = fi[grp - 1:grp, :]

    enter = rowg >= 1
    er = jnp.where(enter, pltpu.roll(fr, 1, 0), 0.0)
    ei = jnp.where(enter, pltpu.roll(fi, 1, 0), 0.0)
    head = pl.ds(0, grp, stride=sub)
    for cb in range(nlb):
        zr_ref[cb] = jnp.zeros((t, LANE), F32)
        zi_ref[cb] = jnp.zeros((t, LANE), F32)
        zr_ref[cb, head, :] = er[:, cb * LANE:(cb + 1) * LANE]
        zi_ref[cb, head, :] = ei[:, cb * LANE:(cb + 1) * LANE]
    zr = jnp.concatenate([zr_ref[cb] for cb in range(nlb)], axis=1).reshape(grp, sub, n)
    zi = jnp.concatenate([zi_ref[cb] for cb in range(nlb)], axis=1).reshape(grp, sub, n)
    k = 1
    while k < sub:
        zr = zr + pltpu.roll(zr, k, 1)
        zi = zi + pltpu.roll(zi, k, 1)
        k *= 2
    hr, hi = _cmul_add(xr.reshape(grp, sub, n), xi.reshape(grp, sub, n),
                       apr_ref[0:sub, :][None], api_ref[0:sub, :][None], zr, zi)
    y = _dot(hr.reshape(t, n).astype(MM), wc_ref[0:n, :]) + _dot(hi.reshape(t, n).astype(MM), wc_ref[n:2 * n, :])
    y = y + d_ref[...] * u
    z = jax.nn.gelu(y)
    g = _dot(z.astype(MM), wg_ref[...]) + bg_ref[...]
    o_ref[0] = g[:, :w] * jax.nn.sigmoid(g[:, w:])


def _s5_tables(lam_re, lam_im, b_re, b_im, c_re, c_im, log_dt):
    dt = jnp.exp(log_dt)[:, None]
    lr = jnp.minimum(lam_re, -1e-4)
    li = lam_im
    mag = jnp.exp(lr * dt)
    ab_re = mag * jnp.cos(li * dt)
    ab_im = mag * jnp.sin(li * dt)
    den = lr * lr + li * li
    nr = ab_re - 1.0
    ni = ab_im
    f_re = (nr * lr + ni * li) / den
    f_im = (ni * lr - nr * li) / den
    bb_re = f_re[..., None] * b_re - f_im[..., None] * b_im
    bb_im = f_re[..., None] * b_im + f_im[..., None] * b_re
    wb = jnp.concatenate([_block_diag(jnp.swapaxes(bb_re, 1, 2)),
                          _block_diag(jnp.swapaxes(bb_im, 1, 2))], axis=1)
    wc = jnp.concatenate([_block_diag(jnp.swapaxes(c_re, 1, 2)),
                          _block_diag(jnp.swapaxes(-c_im, 1, 2))], axis=0)
    a_re = ab_re.reshape(1, S5_N)
    a_im = ab_im.reshape(1, S5_N)
    pr, pi = [a_re], [a_im]
    for _ in range(7):
        pr.append(pr[-1] * a_re - pi[-1] * a_im)
        pi.append(pr[-2] * a_im + pi[-1] * a_re)
    for _ in range(int(math.log2(T_S5 // 8)) - 1):
        r0, i0 = pr[-1], pi[-1]
        pr.append(r0 * r0 - i0 * i0)
        pi.append(2.0 * r0 * i0)
    pad = [jnp.zeros_like(a_re)] * (16 - len(pr))
    return wb.astype(MM), jnp.concatenate(pr + pad, 0), jnp.concatenate(pi + pad, 0), wc.astype(MM)


def _s5(proj, wb, apr, api, wc, d, wg, bg):
    b, s, _ = proj.shape
    w = BRANCH_W
    n = S5_N
    ns = apr.shape[0]
    return pl.pallas_call(
        _s5_kernel,
        grid=(b, s // T_S5),
        in_specs=[pl.BlockSpec((1, T_S5, w), lambda i, j: (i, j, 1)),
                  pl.BlockSpec((w, 2 * n), lambda i, j: (0, 0)),
                  pl.BlockSpec((ns, n), lambda i, j: (0, 0)),
                  pl.BlockSpec((ns, n), lambda i, j: (0, 0)),
                  pl.BlockSpec((2 * n, w), lambda i, j: (0, 0)),
                  pl.BlockSpec((1, w), lambda i, j: (0, 0)),
                  pl.BlockSpec((w, 2 * w), lambda i, j: (0, 0)),
                  pl.BlockSpec((1, 2 * w), lambda i, j: (0, 0))],
        out_specs=pl.BlockSpec((1, T_S5, w), lambda i, j: (i, j, 0)),
        out_shape=jax.ShapeDtypeStruct((b, s, w), F32),
        scratch_shapes=[pltpu.VMEM((2, n), F32)] + [pltpu.VMEM((n // LANE, T_S5, LANE), F32)] * 4,
        compiler_params=_cparams("parallel", "arbitrary"),
        name="s5",
    )(proj, wb, apr, api, wc, d, wg, bg)


def _ret_kernel(q_ref, k_ref, v_ref, cos_ref, sin_ref, dm_ref, xi_ref, zt_ref, cd_ref, o_ref, rst):
    c = RET_CHUNK
    dh = RET_HD

    @pl.when(pl.program_id(1) == 0)
    def _():
        rst[...] = jnp.zeros((RET_HEADS, dh, dh), F32)

    for ci in range(T_RET // c):
        rows = slice(ci * c, (ci + 1) * c)
        cos = cos_ref[rows, :]
        sin = sin_ref[rows, :]
        for h in range(RET_HEADS):
            cols = slice(h * dh, (h + 1) * dh)
            q = q_ref[0, rows, cols]
            k = k_ref[0, rows, cols]
            v = v_ref[0, rows, cols]
            q = q * cos + pltpu.roll(q, dh // 2, 1) * sin
            k = (k * cos + pltpu.roll(k, dh // 2, 1) * sin) * (dh ** -0.5)
            qb = q.astype(MM)
            kb = k.astype(MM)
            vb = v.astype(MM)
            scores = _dot_nt(qb, kb) * dm_ref[h]
            inner = _dot(scores.astype(MM), vb)
            r_prev = rst[h]
            cross = _dot((q * xi_ref[h]).astype(MM), r_prev.astype(MM))
            kv = _dot_tn((k * zt_ref[h]).astype(MM), vb)
            rst[h] = cd_ref[h] * r_prev + kv
            o = inner + cross
            mu = jnp.mean(o, axis=-1, keepdims=True)
            var = jnp.mean(jnp.square(o - mu), axis=-1, keepdims=True)
            o_ref[0, rows, cols] = (o - mu) * lax.rsqrt(var + 1e-5)


def _ret_tables(s):
    dh = RET_HD
    c = RET_CHUNK
    half = dh // 2
    inv = RET_ROPE_BASE ** (-jnp.arange(half, dtype=F32) * 2.0 / dh)
    ang = jnp.arange(s).astype(F32)[:, None] * inv[None, :]
    cos = jnp.cos(ang)
    sin = jnp.sin(ang)
    cosf = jnp.concatenate([cos, cos], axis=1)
    sinf = jnp.concatenate([-sin, sin], axis=1)
    gamma = 1.0 - jnp.exp(jnp.linspace(math.log(1.0 / 32.0), math.log(1.0 / 512.0), RET_HEADS, dtype=F32))
    log_g = jnp.log(gamma)
    n = jnp.arange(c, dtype=F32)
    diff = n[:, None] - n[None, :]
    dmask = jnp.where(diff[None] >= 0, jnp.exp(jnp.maximum(diff, 0.0)[None] * log_g[:, None, None]), 0.0)
    xi = jnp.exp((n[None, :] + 1.0) * log_g[:, None])
    zeta = jnp.exp((c - 1.0 - n)[None, :] * log_g[:, None])
    cdec = jnp.exp(c * log_g)
    xi_b = jnp.broadcast_to(xi[:, :, None], (RET_HEADS, c, dh))
    zt_b = jnp.broadcast_to(zeta[:, :, None], (RET_HEADS, c, dh))
    cd_b = jnp.broadcast_to(cdec[:, None, None], (RET_HEADS, dh, dh))
    return cosf, sinf, dmask, xi_b, zt_b, cd_b


def _retention(proj, tabs):
    b, s, _ = proj.shape
    w = BRANCH_W
    cosf, sinf, dmask, xi_b, zt_b, cd_b = tabs
    c = RET_CHUNK
    dh = RET_HD
    full3 = lambda i, j: (0, 0, 0)
    return pl.pallas_call(
        _ret_kernel,
        grid=(b, s // T_RET),
        in_specs=[pl.BlockSpec((1, T_RET, w), lambda i, j: (i, j, 2)),
                  pl.BlockSpec((1, T_RET, w), lambda i, j: (i, j, 3)),
                  pl.BlockSpec((1, T_RET, w), lambda i, j: (i, j, 4)),
                  pl.BlockSpec((T_RET, dh), lambda i, j: (j, 0)),
                  pl.BlockSpec((T_RET, dh), lambda i, j: (j, 0)),
                  pl.BlockSpec((RET_HEADS, c, c), full3),
                  pl.BlockSpec((RET_HEADS, c, dh), full3),
                  pl.BlockSpec((RET_HEADS, c, dh), full3),
                  pl.BlockSpec((RET_HEADS, dh, dh), full3)],
        out_specs=pl.BlockSpec((1, T_RET, w), lambda i, j: (i, j, 0)),
        out_shape=jax.ShapeDtypeStruct((b, s, w), F32),
        scratch_shapes=[pltpu.VMEM((RET_HEADS, dh, dh), F32)],
        compiler_params=_cparams("parallel", "arbitrary"),
        name="retention",
    )(proj, proj, proj, cosf, sinf, dmask, xi_b, zt_b, cd_b)


def _rope_lanes(x, cos, s_up, s_dn):
    half = ROPE_DIM // 2
    n = x.shape[-1]
    return x * cos + pltpu.roll(x, half, 1) * s_up + pltpu.roll(x, n - half, 1) * s_dn


def _nsa_prep_kernel(q_ref, ks_ref, vs_ref, kw_ref, vw_ref, bg_ref, cos_ref, up_ref, dn_ref,
                     qt_ref, gt_ref, ka_ref, kwo_ref, vst_ref, vsd_ref, vwt_ref):
    t = NSA_T
    qb = Q_BLOCK
    dh = NSA_HD
    cos = cos_ref[...]
    up = up_ref[...]
    dn = dn_ref[...]
    scale = NSA_HD ** -0.5 * math.log2(math.e)
    lane = lax.broadcasted_iota(jnp.int32, (t, LANE), 1)
    low = lane < dh
    tok = pl.program_id(1) * t + lax.broadcasted_iota(jnp.int32, (t, LANE), 0)
    onehot = jnp.where(tok // SLC_BLOCK == lane, 1.0, 0.0).astype(MM)
    ks = _rope_lanes(ks_ref[0], cos, up, dn)
    kw = _rope_lanes(kw_ref[0], cos, up, dn)
    for h in range(NSA_KV_HEADS):
        ks_h = ks if h == 0 else pltpu.roll(ks, LANE - h * dh, 1)
        kw_h = kw if h == 0 else pltpu.roll(kw, LANE - h * dh, 1)
        ka_ref[0, h, :, 0:LANE] = onehot
        ka_ref[0, h, :, LANE:2 * LANE] = jnp.where(low, ks_h, 0.0).astype(MM)
        kwo_ref[0, h] = jnp.where(low, kw_h, 0.0).astype(MM)

    def ones_row(n):
        return jnp.where(lax.broadcasted_iota(jnp.int32, (V_ROWS - dh, n), 0) == 0, 1.0, 0.0).astype(MM)

    r8 = lax.broadcasted_iota(jnp.int32, (8, NSA_GQA * qb), 0)
    for i in range(t // qb):
        rows = slice(i * qb, (i + 1) * qb)
        vs_t = vs_ref[0, rows, :].T
        vw_t = vw_ref[0, rows, :].T
        bg_t = bg_ref[0, rows, :].T
        for h in range(NSA_KV_HEADS):
            vst_ref[0, h, 0, 0:dh, rows] = vs_t[h * dh:(h + 1) * dh].astype(MM)
            vsd_ref[0, h, i, 0:dh, :] = vs_t[h * dh:(h + 1) * dh].astype(MM)
            vsd_ref[0, h, i, dh:V_ROWS, :] = ones_row(qb)
            vwt_ref[0, h, i, 0:dh, :] = vw_t[h * dh:(h + 1) * dh].astype(MM)
            vwt_ref[0, h, i, dh:V_ROWS, :] = ones_row(qb)
            qt_ref[0, h, i, dh:LANE, :] = jnp.zeros((LANE - dh, NSA_GQA * qb), F32)
            gates = jnp.zeros((8, NSA_GQA * qb), F32)
            for br in range(3):
                base = br * NSA_HEADS + h * NSA_GQA
                row = jnp.concatenate([bg_t[base + g:base + g + 1] for g in range(NSA_GQA)], axis=1)
                gates = jnp.where(r8 == br, row, gates)
            gt_ref[0, h, i] = gates
        for cb in range(BRANCH_W // LANE):
            cols = slice(cb * LANE, (cb + 1) * LANE)
            x_t = (_rope_lanes(q_ref[0, rows, cols], cos[rows], up[rows], dn[rows]) * scale).T
            h = (2 * cb) // NSA_GQA
            g0 = (2 * cb) % NSA_GQA
            qt_ref[0, h, i, 0:dh, g0 * qb:(g0 + 1) * qb] = x_t[0:dh]
            qt_ref[0, h, i, 0:dh, (g0 + 1) * qb:(g0 + 2) * qb] = x_t[dh:2 * dh]
    for h in range(NSA_KV_HEADS):
        vst_ref[0, h, 0, dh:V_ROWS, :] = ones_row(t)


def _nsa_rope_tables(pos):
    half = ROPE_DIM // 2
    inv = ROPE_THETA ** (-jnp.arange(half, dtype=F32) * 2.0 / ROPE_DIM)
    ang = pos.astype(F32)[:, None] * inv[None, :]
    cos = jnp.cos(ang)
    sin = jnp.sin(ang)
    n = pos.shape[0]
    zeros = jnp.zeros((n, NSA_HD - ROPE_DIM), F32)
    zh = jnp.zeros((n, half), F32)
    cos_h = jnp.concatenate([cos, cos, jnp.ones((n, NSA_HD - ROPE_DIM), F32)], axis=1)
    up_h = jnp.concatenate([zh, sin, zeros], axis=1)
    dn_h = jnp.concatenate([-sin, zh, zeros], axis=1)
    tile2 = lambda a: jnp.concatenate([a, a], axis=1)
    return tile2(cos_h), tile2(up_h), tile2(dn_h)


def _nsa_prep(proj, tabs):
    b, s, _ = proj.shape
    t = NSA_T
    qb = Q_BLOCK
    hkv = NSA_KV_HEADS
    cols = NSA_GQA * qb
    assert t == SEL_CHUNK and s // SLC_BLOCK <= LANE
    q_blk = sum(IN_SIZES[:3]) // BRANCH_W
    kv_blk = sum(IN_SIZES[:4]) // LANE
    lane_blk = lambda n: pl.BlockSpec((1, t, LANE), lambda i, j: (i, j, n))
    tab_spec = pl.BlockSpec((t, LANE), lambda i, j: (j, 0))
    tile4 = lambda i, j: (i, 0, j, 0)
    tile5 = lambda i, j: (i, 0, j, 0, 0)
    return pl.pallas_call(
        _nsa_prep_kernel,
        grid=(b, s // t),
        in_specs=[pl.BlockSpec((1, t, BRANCH_W), lambda i, j: (i, j, q_blk)),
                  lane_blk(kv_blk + 2), lane_blk(kv_blk + 3), lane_blk(kv_blk + 4), lane_blk(kv_blk + 5),
                  lane_blk(BG_OFF // LANE), tab_spec, tab_spec, tab_spec],
        out_specs=[pl.BlockSpec((1, hkv, t // qb, LANE, cols), tile5),
                   pl.BlockSpec((1, hkv, t // qb, 8, cols), tile5),
                   pl.BlockSpec((1, hkv, t, 2 * LANE), tile4),
                   pl.BlockSpec((1, hkv, t, LANE), tile4),
                   pl.BlockSpec((1, hkv, 1, V_ROWS, t), tile5),
                   pl.BlockSpec((1, hkv, t // qb, V_ROWS, qb), tile5),
                   pl.BlockSpec((1, hkv, t // qb, V_ROWS, qb), tile5)],
        out_shape=[jax.ShapeDtypeStruct((b, hkv, s // qb, LANE, cols), F32),
                   jax.ShapeDtypeStruct((b, hkv, s // qb, 8, cols), F32),
                   jax.ShapeDtypeStruct((b, hkv, s, 2 * LANE), MM),
                   jax.ShapeDtypeStruct((b, hkv, s, LANE), MM),
                   jax.ShapeDtypeStruct((b, hkv, s // t, V_ROWS, t), MM),
                   jax.ShapeDtypeStruct((b, hkv, s // qb, V_ROWS, qb), MM),
                   jax.ShapeDtypeStruct((b, hkv, s // qb, V_ROWS, qb), MM)],
        compiler_params=_cparams("parallel", "parallel"),
        name="nsa_prep",
    )(proj, proj, proj, proj, proj, proj, *tabs)


def _cmp_kernel(xk_ref, xv_ref, pos_ref, w1_ref, w2_ref, cos_ref, up_ref, dn_ref, kc_ref, vct_ref):
    nh = kc_ref.shape[2]
    hid_w = NSA_KV_HEADS * CMP_HIDDEN
    for kind, x_ref in enumerate((xk_ref, xv_ref)):
        lo = jnp.zeros((nh, hid_w), F32)
        hi = jnp.zeros((nh, hid_w), F32)
        posb = jnp.zeros((8, hid_w), F32)
        for j in range(CMP_STRIDE):
            xj = x_ref[0, pl.ds(j, nh, stride=CMP_STRIDE), :].astype(MM)
            lo = lo + _dot(xj, w1_ref[kind, j])
            hi = hi + _dot(xj, w1_ref[kind, CMP_STRIDE + j])
        for j in range(CMP_LEN):
            posb = posb + _dot(pos_ref[kind, j].astype(MM), w1_ref[kind, j])
        hid = jax.nn.gelu(lo + pltpu.roll(hi, nh - 1, 0) + posb[0:1, :])
        for h in range(NSA_KV_HEADS):
            out = _dot(hid[:, h * CMP_HIDDEN:(h + 1) * CMP_HIDDEN].astype(MM), w2_ref[kind])
            if kind == 0:
                kc_ref[0, h] = _rope_lanes(out, cos_ref[...], up_ref[...], dn_ref[...])
            else:
                vct_ref[0, h] = out.T.astype(MM)


def _nsa_compress(proj, pos, w1, w2, tabs):
    b, s, _ = proj.shape
    nh = s // CMP_STRIDE
    hkv = NSA_KV_HEADS
    kv_blk = sum(IN_SIZES[:4]) // LANE
    once = pl.Buffered(1)
    c2 = lambda i: (0, 0)
    c3 = lambda i: (0, 0, 0)
    c4 = lambda i: (0, 0, 0, 0)
    return pl.pallas_call(
        _cmp_kernel,
        grid=(b,),
        in_specs=[pl.BlockSpec((1, s, LANE), lambda i: (i, 0, kv_blk)),
                  pl.BlockSpec((1, s, LANE), lambda i: (i, 0, kv_blk + 1)),
                  pl.BlockSpec(pos.shape, c4),
                  pl.BlockSpec(w1.shape, c4, pipeline_mode=once),
                  pl.BlockSpec(w2.shape, c3),
                  pl.BlockSpec((nh, LANE), c2), pl.BlockSpec((nh, LANE), c2), pl.BlockSpec((nh, LANE), c2)],
        out_specs=[pl.BlockSpec((1, hkv, nh, LANE), lambda i: (i, 0, 0, 0)),
                   pl.BlockSpec((1, hkv, LANE, nh), lambda i: (i, 0, 0, 0))],
        out_shape=[jax.ShapeDtypeStruct((b, hkv, nh, LANE), F32),
                   jax.ShapeDtypeStruct((b, hkv, LANE, nh), MM)],
        compiler_params=_cparams("parallel"),
        name="nsa_compress",
    )(proj, proj, pos, w1, w2, *tabs)


def _softmax_update(s, vt, m, acc):
    m_new = jnp.maximum(m, jnp.max(s, axis=0, keepdims=True))
    alpha = jnp.exp2(m - m_new)
    p = jnp.exp2(s - m_new)
    acc = alpha * acc + _dot(vt, p.astype(MM))
    return m_new, acc


def _normalise(acc):
    return acc[:NSA_HD, :] * (1.0 / acc[NSA_HD:NSA_HD + 1, :])


def _nsa_attn_kernel(q_ref, gt_ref, kc_ref, vc_ref, ka_ref, vs_ref, vd_ref, kw_ref, vw_ref, o_ref,
                     ps_ref, sa_ref, sb_ref):
    blk = pl.program_id(2)
    qb = Q_BLOCK
    cols = NSA_GQA * qb
    ncmp_pad = kc_ref.shape[2]
    nsel = LANE
    qf = q_ref[0, 0, 0]
    qm = qf.astype(MM)
    t = blk * qb + (lax.broadcasted_iota(jnp.int32, (1, cols), 1) & (qb - 1))
    lower = lax.broadcasted_iota(jnp.int32, (qb, 1), 0) <= (t & (qb - 1))

    s_w = []
    v_w = []
    for back in range(WIN // qb, -1, -1):
        c = blk - back
        cc = jnp.maximum(c, 0)
        off = pl.multiple_of(cc * qb, qb)
        s = _dot(kw_ref[0, 0, pl.ds(off, qb), :], qm)
        if back == WIN // qb:
            s = jnp.where(lower | (c < 0), NEG, s)
        elif back == 0:
            s = jnp.where(lower, s, NEG)
        else:
            s = jnp.where(c < 0, NEG, s)
        s_w.append(s)
        v_w.append(vw_ref[0, 0, cc])
    s_w = jnp.concatenate(s_w, axis=0)
    p_w = jnp.exp2(s_w - jnp.max(s_w, axis=0, keepdims=True))
    o_w = _normalise(_dot(jnp.concatenate(v_w, axis=1), p_w.astype(MM)))

    s_c = _dot(kc_ref[0, 0], qf)
    n_id = lax.broadcasted_iota(jnp.int32, (ncmp_pad, 1), 0)
    m_c = (n_id * CMP_STRIDE + (CMP_LEN - 1) <= t) & (n_id < ncmp_pad - 1)
    smax = jnp.max(jnp.where(m_c, s_c, NEG), axis=0, keepdims=True)
    e_c = jnp.where(m_c, jnp.exp2(s_c - smax), 0.0)
    den = jnp.sum(e_c, axis=0, keepdims=True)
    p_c = e_c * (1.0 / jnp.where(den > 0.0, den, 1.0))
    o_c = _dot(vc_ref[0, 0], p_c.astype(MM))[:NSA_HD, :]

    r = SLC_BLOCK // CMP_STRIDE
    nrow = ncmp_pad // r
    ps_ref[...] = p_c[:, 0:qb] + p_c[:, qb:2 * qb] + p_c[:, 2 * qb:3 * qb] + p_c[:, 3 * qb:4 * qb]
    imp = ps_ref[pl.ds(0, nrow, stride=r), :]
    for k in range(1, r - 1):
        imp = imp + ps_ref[pl.ds(k, nrow, stride=r), :]
    edge = 0.5 * ps_ref[pl.ds(r - 1, nrow, stride=r), :]
    first = lax.broadcasted_iota(jnp.int32, (nrow, qb), 0) == 0
    imp = imp + edge + jnp.where(first, 0.0, pltpu.roll(edge, 1, 0))
    if nrow < nsel:
        imp = jnp.concatenate([imp, jnp.zeros((nsel - nrow, qb), F32)], axis=0)

    cur = (blk * qb + lax.broadcasted_iota(jnp.int32, (1, qb), 1)) // SLC_BLOCK
    jid = lax.broadcasted_iota(jnp.int32, (nsel, qb), 0)
    valid = jid <= cur
    forced = (jid == 0) | (jid == cur) | (jid == cur - 1)
    score0 = jnp.where(valid, jnp.where(forced, FORCE_SCORE, imp), -1.0)
    score = score0
    for _ in range(SLC_TOPK):
        mx = jnp.max(score, axis=0, keepdims=True)
        idx = jnp.min(jnp.where(score == mx, jid, nsel), axis=0, keepdims=True)
        score = jnp.where(jid == idx, -3e38, score)
    own = 2 * blk * (qb // (2 * SLC_BLOCK))
    bias = jnp.where((score < -1e38) & (score0 >= 0.0) & (jid < own), 0.0, NEG).astype(MM)
    q_aug = jnp.concatenate([jnp.concatenate([bias] * NSA_GQA, axis=1), qm], axis=0)

    kb = SEL_CHUNK
    nchunk = ka_ref.shape[2] // kb
    diag = blk * qb

    def scores_to(s_ref, j):
        off = pl.multiple_of(jnp.minimum(j, nchunk - 1) * kb, kb)
        s_ref[...] = _dot(ka_ref[0, 0, pl.ds(off, kb), :], q_aug)

    def update_from(s_ref, j, m, acc):
        return _softmax_update(s_ref[...], vs_ref[0, 0, jnp.minimum(j, nchunk - 1)], m, acc)

    def pair_body(i, carry):
        scores_to(sb_ref, 2 * i + 1)
        carry = update_from(sa_ref, 2 * i, *carry)
        scores_to(sa_ref, 2 * i + 2)
        return update_from(sb_ref, 2 * i + 1, *carry)

    nneed = (diag + kb - 1) // kb
    scores_to(sa_ref, 0)
    s_d = _dot(ka_ref[0, 0, pl.ds(pl.multiple_of(diag, qb), qb), LANE:2 * LANE], qm)
    init = _softmax_update(jnp.where(lower, s_d, NEG), vd_ref[0, 0, blk],
                           jnp.full((1, cols), NEG, F32), jnp.zeros((V_ROWS, cols), F32))
    o_s = _normalise(lax.fori_loop(0, (nneed + 1) // 2, pair_body, init)[1])

    g = jax.nn.sigmoid(gt_ref[0, 0, 0])
    o = g[0:1, :] * o_c + g[1:2, :] * o_s + g[2:3, :] * o_w
    for gp in range(NSA_GQA // 2):
        pair = jnp.concatenate([o[:, 2 * gp * qb:(2 * gp + 1) * qb],
                                o[:, (2 * gp + 1) * qb:(2 * gp + 2) * qb]], axis=0)
        o_ref[0, :, gp * LANE:(gp + 1) * LANE] = pair.T


def _nsa_attn(qt, gt, kc, vct, ka, vst, vsd, kw, vwt):
    b, hkv, nqb, _, cols = qt.shape
    s = ka.shape[2]
    ncmp_pad = kc.shape[2]
    assert CMP_LEN == 2 * CMP_STRIDE and SLC_BLOCK % CMP_STRIDE == 0 and s // SLC_BLOCK <= LANE
    assert (s // SEL_CHUNK) % 2 == 0
    per_head = lambda i, h, j: (i, h, 0, 0)
    per_head5 = lambda i, h, j: (i, h, 0, 0, 0)
    per_blk = lambda i, h, j: (i, h, j, 0, 0)
    return pl.pallas_call(
        _nsa_attn_kernel,
        grid=(b, hkv, nqb),
        in_specs=[pl.BlockSpec((1, 1, 1, LANE, cols), per_blk),
                  pl.BlockSpec((1, 1, 1, 8, cols), per_blk),
                  pl.BlockSpec((1, 1, ncmp_pad, LANE), per_head),
                  pl.BlockSpec((1, 1, LANE, ncmp_pad), per_head),
                  pl.BlockSpec((1, 1, s, 2 * LANE), per_head),
                  pl.BlockSpec((1, 1, s // SEL_CHUNK, V_ROWS, SEL_CHUNK), per_head5),
                  pl.BlockSpec((1, 1, s // Q_BLOCK, V_ROWS, Q_BLOCK), per_head5),
                  pl.BlockSpec((1, 1, s, LANE), per_head),
                  pl.BlockSpec((1, 1, s // Q_BLOCK, V_ROWS, Q_BLOCK), per_head5)],
        out_specs=pl.BlockSpec((1, Q_BLOCK, NSA_GQA * NSA_HD), lambda i, h, j: (i, j, h)),
        out_shape=jax.ShapeDtypeStruct((b, s, hkv * NSA_GQA * NSA_HD), F32),
        scratch_shapes=[pltpu.VMEM((ncmp_pad, Q_BLOCK), F32),
                        pltpu.VMEM((SEL_CHUNK, cols), F32), pltpu.VMEM((SEL_CHUNK, cols), F32)],
        compiler_params=_cparams("parallel", "parallel", "arbitrary"),
        name="nsa_attn",
    )(qt, gt, kc, vct, ka, vst, vsd, kw, vwt)


def _nsa_weights(pos_k, pos_v, ck_w1, ck_w2, cv_w1, cv_w2):
    dh = NSA_HD
    eye = jnp.eye(NSA_KV_HEADS, dtype=F32)
    slabs = lambda w: jnp.einsum("jrc,hg->jhrgc", w.reshape(CMP_LEN, dh, CMP_HIDDEN), eye).reshape(
        CMP_LEN, NSA_KV_HEADS * dh, NSA_KV_HEADS * CMP_HIDDEN)
    w1 = jnp.stack([slabs(ck_w1), slabs(cv_w1)]).astype(MM)
    w2 = jnp.pad(jnp.stack([ck_w2, cv_w2]), ((0, 0), (0, 0), (0, LANE - dh))).astype(MM)
    pos = jnp.stack([pos_k, pos_v])
    pos = jnp.broadcast_to(jnp.concatenate([pos] * NSA_KV_HEADS, axis=-1)[:, :, None, :],
                           (2, CMP_LEN, 8, NSA_KV_HEADS * dh))
    return pos, w1, w2


def _nsa(proj, weights, tabs):
    tok_tabs, cmp_tabs = tabs
    qt, gt, ka, kw, vst, vsd, vwt = _nsa_prep(proj, tok_tabs)
    kc, vct = _nsa_compress(proj, *weights, cmp_tabs)
    return _nsa_attn(qt, gt, kc, vct, ka, vst, vsd, kw, vwt)


def _nsa_tables(s):
    nhalf = s // CMP_STRIDE
    tok_tabs = _nsa_rope_tables(jnp.arange(s))
    cmp_tabs = _nsa_rope_tables(jnp.arange(nhalf) * CMP_STRIDE + CMP_LEN - 1)
    return tok_tabs, cmp_tabs


def _out_kernel(x_ref, g_ref, yl_ref, ys_ref, yr_ref, yn_ref, wgm_ref, wb_ref, wo_ref,
                fg_ref, o_ref, *, final):
    x = x_ref[0]
    h = _rms(x, g_ref[...]).astype(MM)
    w = BRANCH_W
    d = D_MODEL
    m0 = N_BRANCH * w
    merged = jnp.zeros(x.shape, F32)
    for n, y_ref in enumerate((yl_ref, ys_ref, yr_ref, yn_ref)):
        gate = jax.nn.silu(_dot(h, wgm_ref[0, :, n * w:(n + 1) * w]))
        branch = _dot((y_ref[0] * gate).astype(MM), wb_ref[n])
        merged = merged + jax.nn.sigmoid(_dot(h, wgm_ref[0, :, m0 + n * d:m0 + (n + 1) * d])) * branch
    out = x + _dot(merged.astype(MM), wo_ref[...])
    if final:
        out = _rms(out, fg_ref[...])
    o_ref[0] = out


def _out(x, g, ys, wgm, l, wb, wo, fg, final):
    b, s, d = x.shape
    w = BRANCH_W
    tm = TM_PROJ
    row = lambda i, j: (i, j, 0)
    c2 = lambda i, j: (0, 0)
    once = pl.Buffered(1)
    return pl.pallas_call(
        functools.partial(_out_kernel, final=final),
        grid=(b, s // tm),
        in_specs=[pl.BlockSpec((1, tm, d), row),
                  pl.BlockSpec((1, d), c2)]
                 + [pl.BlockSpec((1, tm, w), row)] * N_BRANCH
                 + [pl.BlockSpec((1, d, N_BRANCH * (w + d)), lambda i, j: (l, 0, 0), pipeline_mode=once),
                    pl.BlockSpec((N_BRANCH, w, d), lambda i, j: (0, 0, 0), pipeline_mode=once),
                    pl.BlockSpec((d, d), c2, pipeline_mode=once),
                    pl.BlockSpec((1, d), c2)],
        out_specs=pl.BlockSpec((1, tm, d), row),
        out_shape=jax.ShapeDtypeStruct((b, s, d), F32),
        compiler_params=_cparams("parallel", "arbitrary"),
        name="gate_merge_out",
    )(x, g, *ys, wgm, wb, wo, fg)


def _split_w_in(w_in):
    pad = jnp.zeros(w_in.shape[:2] + (PROJ_W - GATE_OFF,), w_in.dtype)
    w_mix = jnp.concatenate([w_in[:, :, :GATE_OFF], pad], axis=2).astype(MM)
    return w_mix, w_in[:, :, GATE_OFF:].astype(MM)


def _layer(x, p, l, w_mix, w_gm, ret_tabs, nsa_tabs, final_g):
    g = p["norm_g"][l].reshape(1, D_MODEL)
    proj = _inproj(x, g, w_mix, l)

    wab = jnp.concatenate([_block_diag(p["lru_wa"][l]), _block_diag(p["lru_wx"][l])], axis=1).astype(MM)
    bab = jnp.concatenate([p["lru_ba"][l], p["lru_bx"][l]]).reshape(1, 2 * BRANCH_W)
    y_lru = _lru(proj, p["lru_conv_w"][l], p["lru_conv_b"][l].reshape(1, -1), wab, bab,
                 p["lru_lambda"][l].reshape(1, -1))

    wb, apr, api, wc = _s5_tables(p["s5_lambda_re"][l], p["s5_lambda_im"][l], p["s5_b_re"][l],
                                  p["s5_b_im"][l], p["s5_c_re"][l], p["s5_c_im"][l], p["s5_log_dt"][l])
    y_s5 = _s5(proj, wb, apr, api, wc, p["s5_d"][l].reshape(1, -1), p["s5_w_glu"][l].astype(MM),
               p["s5_b_glu"][l].reshape(1, -1))

    y_ret = _retention(proj, ret_tabs)

    nsa_w = _nsa_weights(p["nsa_pos_k"][l], p["nsa_pos_v"][l], p["nsa_ck_w1"][l], p["nsa_ck_w2"][l],
                         p["nsa_cv_w1"][l], p["nsa_cv_w2"][l])
    y_nsa = _nsa(proj, nsa_w, nsa_tabs)

    final = l == DEPTH - 1
    return _out(x, g, (y_lru, y_s5, y_ret, y_nsa), w_gm, l, p["w_branch"][l].astype(MM),
                p["w_out"][l].astype(MM), final_g.reshape(1, D_MODEL), final)


def kernel(x, norm_g, w_in, lru_conv_w, lru_conv_b, lru_wa, lru_ba, lru_wx, lru_bx, lru_lambda,
           s5_lambda_re, s5_lambda_im, s5_b_re, s5_b_im, s5_c_re, s5_c_im, s5_log_dt, s5_d,
           s5_w_glu, s5_b_glu, nsa_pos_k, nsa_pos_v, nsa_ck_w1, nsa_ck_w2, nsa_cv_w1, nsa_cv_w2,
           w_branch, w_out, final_norm_g):
    p = dict(norm_g=norm_g, w_in=w_in, lru_conv_w=lru_conv_w, lru_conv_b=lru_conv_b, lru_wa=lru_wa,
             lru_ba=lru_ba, lru_wx=lru_wx, lru_bx=lru_bx, lru_lambda=lru_lambda,
             s5_lambda_re=s5_lambda_re, s5_lambda_im=s5_lambda_im, s5_b_re=s5_b_re, s5_b_im=s5_b_im,
             s5_c_re=s5_c_re, s5_c_im=s5_c_im, s5_log_dt=s5_log_dt, s5_d=s5_d, s5_w_glu=s5_w_glu,
             s5_b_glu=s5_b_glu, nsa_pos_k=nsa_pos_k, nsa_pos_v=nsa_pos_v, nsa_ck_w1=nsa_ck_w1,
             nsa_ck_w2=nsa_ck_w2, nsa_cv_w1=nsa_cv_w1, nsa_cv_w2=nsa_cv_w2, w_branch=w_branch,
             w_out=w_out)
    s = x.shape[1]
    ret_tabs = _ret_tables(s)
    nsa_tabs = _nsa_tables(s)
    w_mix, w_gm = _split_w_in(w_in)
    for l in range(DEPTH):
        x = _layer(x, p, l, w_mix, w_gm, ret_tabs, nsa_tabs, final_norm_g)
    return x
```

```python
import functools
import math

import jax
import jax.numpy as jnp
from jax import lax
from jax.experimental import pallas as pl
from jax.experimental.pallas import tpu as pltpu

F32 = jnp.float32
MM = jnp.bfloat16

D_MODEL = 1024
DEPTH = 4
N_BRANCH = 4
BRANCH_W = 512
NORM_EPS = 1e-6
LRU_HEADS = 8
LRU_HD = BRANCH_W // LRU_HEADS
CONV_W = 4
LRU_C = 8.0
S5_GROUP = 16
S5_GROUPS = BRANCH_W // S5_GROUP
S5_STATE = 64
S5_N = S5_GROUPS * S5_STATE
RET_HEADS = 4
RET_HD = BRANCH_W // RET_HEADS
RET_CHUNK = 128
RET_ROPE_BASE = 10000.0
NSA_HEADS = 8
NSA_KV_HEADS = 2
NSA_HD = BRANCH_W // NSA_HEADS
NSA_GQA = NSA_HEADS // NSA_KV_HEADS
CMP_LEN = 32
CMP_STRIDE = 16
CMP_HIDDEN = 256
SLC_BLOCK = 64
SLC_TOPK = 16
WIN = 512
Q_BLOCK = 128
ROPE_THETA = 500000.0
ROPE_DIM = NSA_HD // 4
FORCE_SCORE = 1e4
NEG = -1e30
NSA_KV_W = 6 * NSA_KV_HEADS * NSA_HD
IN_SIZES = (BRANCH_W, BRANCH_W, 3 * BRANCH_W, NSA_HEADS * NSA_HD, NSA_KV_W, 3 * NSA_HEADS,
            N_BRANCH * BRANCH_W, N_BRANCH * D_MODEL)
LANE = 128
MIX_W = sum(IN_SIZES[:5])
BG_OFF = MIX_W
GATE_OFF = BG_OFF + IN_SIZES[5]
MERGE_OFF = GATE_OFF + IN_SIZES[6]
PROJ_W = MIX_W + LANE
VMEM_LIMIT = 56 * 1024 * 1024

TM_PROJ = 256
T_LRU = 256
T_S5 = 256
T_RET = 512
S5_TILE_GROUP = 4
SEL_CHUNK = 512
NSA_T = SEL_CHUNK
V_ROWS = NSA_HD + 16


def _cparams(*sem):
    return pltpu.CompilerParams(dimension_semantics=sem, vmem_limit_bytes=VMEM_LIMIT)


def _rms(x, g):
    ms = jnp.mean(x * x, axis=-1, keepdims=True)
    return (x * lax.rsqrt(ms + NORM_EPS)) * g


def _dot(a, b):
    return jnp.dot(a, b, preferred_element_type=F32)


def _dot_nt(a, b):
    return lax.dot_general(a, b, (((1,), (1,)), ((), ())), preferred_element_type=F32)


def _dot_tn(a, b):
    return lax.dot_general(a, b, (((0,), (0,)), ((), ())), preferred_element_type=F32)


def _inproj_kernel(x_ref, g_ref, w_ref, o_ref):
    h = _rms(x_ref[0], g_ref[...])
    o_ref[0] = _dot(h.astype(MM), w_ref[0])


def _inproj(x, g, w, l):
    b, s, d = x.shape
    n = w.shape[2]
    return pl.pallas_call(
        _inproj_kernel,
        grid=(b, s // TM_PROJ),
        in_specs=[pl.BlockSpec((1, TM_PROJ, d), lambda i, j: (i, j, 0)),
                  pl.BlockSpec((1, d), lambda i, j: (0, 0)),
                  pl.BlockSpec((1, d, n), lambda i, j: (l, 0, 0))],
        out_specs=pl.BlockSpec((1, TM_PROJ, n), lambda i, j: (i, j, 0)),
        out_shape=jax.ShapeDtypeStruct((b, s, n), F32),
        compiler_params=_cparams("parallel", "arbitrary"),
        name="inproj",
    )(x, g, w)


def _lru_kernel(u_ref, cw_ref, cb_ref, wab_ref, bab_ref, lam_ref, o_ref, ubuf, hcar):
    t = T_LRU
    w = BRANCH_W

    @pl.when(pl.program_id(1) == 0)
    def _():
        ubuf[0:8, :] = jnp.zeros((8, w), F32)
        hcar[...] = jnp.zeros((1, w), F32)

    ubuf[8:8 + t, :] = u_ref[0]
    xc = cb_ref[...] + cw_ref[0:1, :] * ubuf[5:5 + t, :]
    for k in range(1, CONV_W):
        xc = xc + cw_ref[k:k + 1, :] * ubuf[5 + k:5 + k + t, :]
    ubuf[0:8, :] = ubuf[t:t + 8, :]

    ri = _dot(xc.astype(MM), wab_ref[...]) + bab_ref[...]
    r = jax.nn.sigmoid(ri[:, :w])
    gi = jax.nn.sigmoid(ri[:, w:])
    z = -lam_ref[...]
    softplus = jnp.maximum(z, 0.0) + jnp.log1p(jnp.exp(-jnp.abs(z)))
    a = jnp.exp(-LRU_C * r * softplus)
    bt = jnp.sqrt(1.0 - a * a) * gi * xc

    row = lax.broadcasted_iota(jnp.int32, (t, w), 0)
    k = 1
    while k < t:
        keep = row >= k
        a_sh = jnp.where(keep, pltpu.roll(a, k, 0), 1.0)
        b_sh = jnp.where(keep, pltpu.roll(bt, k, 0), 0.0)
        bt = a * b_sh + bt
        a = a * a_sh
        k *= 2
    h = bt + a * hcar[...]
    o_ref[0] = h
    hcar[...] = h[t - 1:t, :]


def _lru(proj, cw, cb, wab, bab, lam):
    b, s, _ = proj.shape
    w = BRANCH_W
    return pl.pallas_call(
        _lru_kernel,
        grid=(b, s // T_LRU),
        in_specs=[pl.BlockSpec((1, T_LRU, w), lambda i, j: (i, j, 0)),
                  pl.BlockSpec((CONV_W, w), lambda i, j: (0, 0)),
                  pl.BlockSpec((1, w), lambda i, j: (0, 0)),
                  pl.BlockSpec((w, 2 * w), lambda i, j: (0, 0)),
                  pl.BlockSpec((1, 2 * w), lambda i, j: (0, 0)),
                  pl.BlockSpec((1, w), lambda i, j: (0, 0))],
        out_specs=pl.BlockSpec((1, T_LRU, w), lambda i, j: (i, j, 0)),
        out_shape=jax.ShapeDtypeStruct((b, s, w), F32),
        scratch_shapes=[pltpu.VMEM((T_LRU + 8, w), F32), pltpu.VMEM((1, w), F32)],
        compiler_params=_cparams("parallel", "arbitrary"),
        name="rg_lru",
    )(proj, cw, cb, wab, bab, lam)


def _block_diag(blocks):
    n, r, c = blocks.shape
    eye = jnp.eye(n, dtype=blocks.dtype)
    return jnp.einsum("nrc,nm->nrmc", blocks, eye).reshape(n * r, n * c)


def _cmul_add(xr, xi, ar, ai, br, bi):
    return xr + (ar * br - ai * bi), xi + (ar * bi + ai * br)


def _s5_kernel(u_ref, wb_ref, apr_ref, api_ref, wc_ref, d_ref, wg_ref, bg_ref, o_ref,
               car, lr_ref, li_ref, y_ref):
    t = T_S5
    n = S5_N
    w = BRANCH_W
    sub = 8
    grp = t // sub
    mxu = 256

    @pl.when(pl.program_id(1) == 0)
    def _():
        car[...] = jnp.zeros((2, n), F32)

    u = u_ref[0]
    ub = u.astype(MM)
    y_ref[...] = d_ref[...] * u
    first = lax.broadcasted_iota(jnp.int32, (t, LANE), 0) == 0
    i8 = lax.broadcasted_iota(jnp.int32, (sub, LANE), 0)
    rowg = lax.broadcasted_iota(jnp.int32, (grp, LANE), 0)
    rep = lax.broadcasted_iota(jnp.int32, (t, grp), 0) // sub == lax.broadcasted_iota(jnp.int32, (t, grp), 1)
    rep = jnp.where(rep, 1.0, 0.0).astype(MM)
    rep3 = jnp.concatenate([rep, rep, rep], axis=1)
    last = pl.ds(sub - 1, grp, stride=sub)

    ntile = n // LANE
    for c0 in range(0, ntile, S5_TILE_GROUP):
        tiles = range(c0, c0 + S5_TILE_GROUP)
        lanes = [slice(cb * LANE, (cb + 1) * LANE) for cb in tiles]
        ios = [slice(io * mxu, (io + 1) * mxu) for io in ((cb * LANE // S5_STATE * S5_GROUP) // mxu for cb in tiles)]
        aprs = [apr_ref[:, ln] for ln in lanes]
        apis = [api_ref[:, ln] for ln in lanes]
        xs = [_dot(ub[:, io], wb_ref[cb]) for cb, io in zip(tiles, ios)]
        loc = []
        for x, apr, api, ln in zip(xs, aprs, apis, lanes):
            cr, ci = _cmul_add(0.0, 0.0, apr[0:1], api[0:1], car[0:1, ln], car[1:2, ln])
            xr = (x[:, :LANE] + jnp.where(first, cr, 0.0)).reshape(grp, sub, LANE)
            xi = (x[:, LANE:] + jnp.where(first, ci, 0.0)).reshape(grp, sub, LANE)
            k = 1
            while k < sub:
                ar = jnp.where(i8 >= k, apr[k - 1:k], 0.0)[None]
                ai = jnp.where(i8 >= k, api[k - 1:k], 0.0)[None]
                xr, xi = _cmul_add(xr, xi, ar, ai, pltpu.roll(xr, k, 1), pltpu.roll(xi, k, 1))
                k *= 2
            loc.append((xr, xi))
        for j, (xr, xi) in enumerate(loc):
            lr_ref[j] = xr.reshape(t, LANE)
            li_ref[j] = xi.reshape(t, LANE)
        ends = [(lr_ref[j, last, :], li_ref[j, last, :]) for j in range(len(loc))]
        k = 1
        step = sub - 1
        while k < grp:
            keep = rowg >= k
            nxt = []
            for (fr, fi), apr, api in zip(ends, aprs, apis):
                sr = jnp.where(keep, pltpu.roll(fr, k, 0), 0.0)
                si = jnp.where(keep, pltpu.roll(fi, k, 0), 0.0)
                nxt.append(_cmul_add(fr, fi, apr[step:step + 1], api[step:step + 1], sr, si))
            ends = nxt
            k *= 2
            step += 1
        enter = rowg >= 1
        zs = []
        for (fr, fi), ln in zip(ends, lanes):
            car[0:1, ln] = fr[grp - 1:grp, :]
            car[1:2, ln] = fi[grp - 1:grp, :]
            e = jnp.concatenate([jnp.where(enter, pltpu.roll(fr, 1, 0), 0.0),
                                 jnp.where(enter, pltpu.roll(fi, 1, 0), 0.0)], axis=1)
            e_hi = e.astype(MM)
            r1 = e - e_hi.astype(F32)
            e_mid = r1.astype(MM)
            e_lo = (r1 - e_mid.astype(F32)).astype(MM)
            zs.append(_dot(rep3, jnp.concatenate([e_hi, e_mid, e_lo], axis=0)))
        for cb, io, (xr, xi), z, apr, api in zip(tiles, ios, loc, zs, aprs, apis):
            hr, hi = _cmul_add(xr, xi, apr[0:sub][None], api[0:sub][None],
                               z[:, :LANE].reshape(grp, sub, LANE), z[:, LANE:].reshape(grp, sub, LANE))
            h = jnp.concatenate([hr.reshape(t, LANE), hi.reshape(t, LANE)], axis=1).astype(MM)
            y_ref[:, io] += _dot(h, wc_ref[cb])

    z = jax.nn.gelu(y_ref[...])
    g = _dot(z.astype(MM), wg_ref[...]) + bg_ref[...]
    o_ref[0] = g[:, :w] * jax.nn.sigmoid(g[:, w:])


def _s5_tables(lam_re, lam_im, b_re, b_im, c_re, c_im, log_dt):
    dt = jnp.exp(log_dt)[:, None]
    lr = jnp.minimum(lam_re, -1e-4)
    li = lam_im
    mag = jnp.exp(lr * dt)
    ab_re = mag * jnp.cos(li * dt)
    ab_im = mag * jnp.sin(li * dt)
    den = lr * lr + li * li
    nr = ab_re - 1.0
    ni = ab_im
    f_re = (nr * lr + ni * li) / den
    f_im = (ni * lr - nr * li) / den
    bb_re = f_re[..., None] * b_re - f_im[..., None] * b_im
    bb_im = f_re[..., None] * b_im + f_im[..., None] * b_re
    wb_re = _block_diag(jnp.swapaxes(bb_re, 1, 2))
    wb_im = _block_diag(jnp.swapaxes(bb_im, 1, 2))
    wc_re = _block_diag(jnp.swapaxes(c_re, 1, 2))
    wc_im = _block_diag(jnp.swapaxes(-c_im, 1, 2))
    wb, wc = [], []
    for cb in range(S5_N // LANE):
        lanes = slice(cb * LANE, (cb + 1) * LANE)
        io = (cb * LANE // S5_STATE * S5_GROUP) // 256
        ios = slice(io * 256, (io + 1) * 256)
        wb.append(jnp.concatenate([wb_re[ios, lanes], wb_im[ios, lanes]], axis=1))
        wc.append(jnp.concatenate([wc_re[lanes, ios], wc_im[lanes, ios]], axis=0))
    wb = jnp.stack(wb)
    wc = jnp.stack(wc)
    a_re = ab_re.reshape(1, S5_N)
    a_im = ab_im.reshape(1, S5_N)
    pr, pi = [a_re], [a_im]
    for _ in range(7):
        pr.append(pr[-1] * a_re - pi[-1] * a_im)
        pi.append(pr[-2] * a_im + pi[-1] * a_re)
    for _ in range(int(math.log2(T_S5 // 8)) - 1):
        r0, i0 = pr[-1], pi[-1]
        pr.append(r0 * r0 - i0 * i0)
        pi.append(2.0 * r0 * i0)
    pad = [jnp.zeros_like(a_re)] * (16 - len(pr))
    return wb.astype(MM), jnp.concatenate(pr + pad, 0), jnp.concatenate(pi + pad, 0), wc.astype(MM)


def _s5(proj, wb, apr, api, wc, d, wg, bg):
    b, s, _ = proj.shape
    w = BRANCH_W
    n = S5_N
    c2 = lambda i, j: (0, 0)
    c3 = lambda i, j: (0, 0, 0)
    return pl.pallas_call(
        _s5_kernel,
        grid=(b, s // T_S5),
        in_specs=[pl.BlockSpec((1, T_S5, w), lambda i, j: (i, j, 1)),
                  pl.BlockSpec(wb.shape, c3),
                  pl.BlockSpec(apr.shape, c2),
                  pl.BlockSpec(api.shape, c2),
                  pl.BlockSpec(wc.shape, c3),
                  pl.BlockSpec((1, w), c2),
                  pl.BlockSpec((w, 2 * w), c2),
                  pl.BlockSpec((1, 2 * w), c2)],
        out_specs=pl.BlockSpec((1, T_S5, w), lambda i, j: (i, j, 0)),
        out_shape=jax.ShapeDtypeStruct((b, s, w), F32),
        scratch_shapes=[pltpu.VMEM((2, n), F32), pltpu.VMEM((S5_TILE_GROUP, T_S5, LANE), F32),
                        pltpu.VMEM((S5_TILE_GROUP, T_S5, LANE), F32), pltpu.VMEM((T_S5, w), F32)],
        compiler_params=_cparams("parallel", "arbitrary"),
        name="s5",
    )(proj, wb, apr, api, wc, d, wg, bg)


def _ret_kernel(q_ref, k_ref, v_ref, cos_ref, sin_ref, dm_ref, xi_ref, zt_ref, cd_ref, o_ref, rst):
    c = RET_CHUNK
    dh = RET_HD

    @pl.when(pl.program_id(1) == 0)
    def _():
        rst[...] = jnp.zeros((RET_HEADS, dh, dh), F32)

    for ci in range(T_RET // c):
        rows = slice(ci * c, (ci + 1) * c)
        cos = cos_ref[rows, :]
        sin = sin_ref[rows, :]
        for h in range(RET_HEADS):
            cols = slice(h * dh, (h + 1) * dh)
            q = q_ref[0, rows, cols]
            k = k_ref[0, rows, cols]
            v = v_ref[0, rows, cols]
            q = q * cos + pltpu.roll(q, dh // 2, 1) * sin
            k = (k * cos + pltpu.roll(k, dh // 2, 1) * sin) * (dh ** -0.5)
            qb = q.astype(MM)
            kb = k.astype(MM)
            vb = v.astype(MM)
            scores = _dot_nt(qb, kb) * dm_ref[h]
            inner = _dot(scores.astype(MM), vb)
            r_prev = rst[h]
            cross = _dot((q * xi_ref[h]).astype(MM), r_prev.astype(MM))
            kv = _dot_tn((k * zt_ref[h]).astype(MM), vb)
            rst[h] = cd_ref[h] * r_prev + kv
            o = inner + cross
            mu = jnp.mean(o, axis=-1, keepdims=True)
            var = jnp.mean(jnp.square(o - mu), axis=-1, keepdims=True)
            o_ref[0, rows, cols] = (o - mu) * lax.rsqrt(var + 1e-5)


def _ret_tables(s):
    dh = RET_HD
    c = RET_CHUNK
    half = dh // 2
    inv = RET_ROPE_BASE ** (-jnp.arange(half, dtype=F32) * 2.0 / dh)
    ang = jnp.arange(s).astype(F32)[:, None] * inv[None, :]
    cos = jnp.cos(ang)
    sin = jnp.sin(ang)
    cosf = jnp.concatenate([cos, cos], axis=1)
    sinf = jnp.concatenate([-sin, sin], axis=1)
    gamma = 1.0 - jnp.exp(jnp.linspace(math.log(1.0 / 32.0), math.log(1.0 / 512.0), RET_HEADS, dtype=F32))
    log_g = jnp.log(gamma)
    n = jnp.arange(c, dtype=F32)
    diff = n[:, None] - n[None, :]
    dmask = jnp.where(diff[None] >= 0, jnp.exp(jnp.maximum(diff, 0.0)[None] * log_g[:, None, None]), 0.0)
    xi = jnp.exp((n[None, :] + 1.0) * log_g[:, None])
    zeta = jnp.exp((c - 1.0 - n)[None, :] * log_g[:, None])
    cdec = jnp.exp(c * log_g)
    xi_b = jnp.broadcast_to(xi[:, :, None], (RET_HEADS, c, dh))
    zt_b = jnp.broadcast_to(zeta[:, :, None], (RET_HEADS, c, dh))
    cd_b = jnp.broadcast_to(cdec[:, None, None], (RET_HEADS, dh, dh))
    return cosf, sinf, dmask, xi_b, zt_b, cd_b


def _retention(proj, tabs):
    b, s, _ = proj.shape
    w = BRANCH_W
    cosf, sinf, dmask, xi_b, zt_b, cd_b = tabs
    c = RET_CHUNK
    dh = RET_HD
    full3 = lambda i, j: (0, 0, 0)
    return pl.pallas_call(
        _ret_kernel,
        grid=(b, s // T_RET),
        in_specs=[pl.BlockSpec((1, T_RET, w), lambda i, j: (i, j, 2)),
                  pl.BlockSpec((1, T_RET, w), lambda i, j: (i, j, 3)),
                  pl.BlockSpec((1, T_RET, w), lambda i, j: (i, j, 4)),
                  pl.BlockSpec((T_RET, dh), lambda i, j: (j, 0)),
                  pl.BlockSpec((T_RET, dh), lambda i, j: (j, 0)),
                  pl.BlockSpec((RET_HEADS, c, c), full3),
                  pl.BlockSpec((RET_HEADS, c, dh), full3),
                  pl.BlockSpec((RET_HEADS, c, dh), full3),
                  pl.BlockSpec((RET_HEADS, dh, dh), full3)],
        out_specs=pl.BlockSpec((1, T_RET, w), lambda i, j: (i, j, 0)),
        out_shape=jax.ShapeDtypeStruct((b, s, w), F32),
        scratch_shapes=[pltpu.VMEM((RET_HEADS, dh, dh), F32)],
        compiler_params=_cparams("parallel", "arbitrary"),
        name="retention",
    )(proj, proj, proj, cosf, sinf, dmask, xi_b, zt_b, cd_b)


def _rope_lanes(x, cos, s_up, s_dn):
    half = ROPE_DIM // 2
    n = x.shape[-1]
    return x * cos + pltpu.roll(x, half, 1) * s_up + pltpu.roll(x, n - half, 1) * s_dn


def _nsa_prep_kernel(q_ref, ks_ref, vs_ref, kw_ref, vw_ref, bg_ref, cos_ref, up_ref, dn_ref,
                     qt_ref, gt_ref, ka_ref, kwo_ref, vst_ref, vsd_ref, vwt_ref):
    t = NSA_T
    qb = Q_BLOCK
    dh = NSA_HD
    cos = cos_ref[...]
    up = up_ref[...]
    dn = dn_ref[...]
    scale = NSA_HD ** -0.5 * math.log2(math.e)
    lane = lax.broadcasted_iota(jnp.int32, (t, LANE), 1)
    low = lane < dh
    tok = pl.program_id(1) * t + lax.broadcasted_iota(jnp.int32, (t, LANE), 0)
    onehot = jnp.where(tok // SLC_BLOCK == lane, 1.0, 0.0).astype(MM)
    ks = _rope_lanes(ks_ref[0], cos, up, dn)
    kw = _rope_lanes(kw_ref[0], cos, up, dn)
    for h in range(NSA_KV_HEADS):
        ks_h = ks if h == 0 else pltpu.roll(ks, LANE - h * dh, 1)
        kw_h = kw if h == 0 else pltpu.roll(kw, LANE - h * dh, 1)
        ka_ref[0, h, :, 0:LANE] = onehot
        ka_ref[0, h, :, LANE:2 * LANE] = jnp.where(low, ks_h, 0.0).astype(MM)
        kwo_ref[0, h] = jnp.where(low, kw_h, 0.0).astype(MM)

    def ones_row(n):
        return jnp.where(lax.broadcasted_iota(jnp.int32, (V_ROWS - dh, n), 0) == 0, 1.0, 0.0).astype(MM)

    r8 = lax.broadcasted_iota(jnp.int32, (8, NSA_GQA * qb), 0)
    for i in range(t // qb):
        rows = slice(i * qb, (i + 1) * qb)
        vs_t = vs_ref[0, rows, :].T
        vw_t = vw_ref[0, rows, :].T
        bg_t = bg_ref[0, rows, :].T
        for h in range(NSA_KV_HEADS):
            vst_ref[0, h, 0, 0:dh, rows] = vs_t[h * dh:(h + 1) * dh].astype(MM)
            vsd_ref[0, h, i, 0:dh, :] = vs_t[h * dh:(h + 1) * dh].astype(MM)
            vsd_ref[0, h, i, dh:V_ROWS, :] = ones_row(qb)
            vwt_ref[0, h, i, 0:dh, :] = vw_t[h * dh:(h + 1) * dh].astype(MM)
            vwt_ref[0, h, i, dh:V_ROWS, :] = ones_row(qb)
            qt_ref[0, h, i, dh:LANE, :] = jnp.zeros((LANE - dh, NSA_GQA * qb), F32)
            gates = jnp.zeros((8, NSA_GQA * qb), F32)
            for br in range(3):
                base = br * NSA_HEADS + h * NSA_GQA
                row = jnp.concatenate([bg_t[base + g:base + g + 1] for g in range(NSA_GQA)], axis=1)
                gates = jnp.where(r8 == br, row, gates)
            gt_ref[0, h, i] = gates
        for cb in range(BRANCH_W // LANE):
            cols = slice(cb * LANE, (cb + 1) * LANE)
            x_t = (_rope_lanes(q_ref[0, rows, cols], cos[rows], up[rows], dn[rows]) * scale).T
            h = (2 * cb) // NSA_GQA
            g0 = (2 * cb) % NSA_GQA
            qt_ref[0, h, i, 0:dh, g0 * qb:(g0 + 1) * qb] = x_t[0:dh]
            qt_ref[0, h, i, 0:dh, (g0 + 1) * qb:(g0 + 2) * qb] = x_t[dh:2 * dh]
    for h in range(NSA_KV_HEADS):
        vst_ref[0, h, 0, dh:V_ROWS, :] = ones_row(t)


def _nsa_rope_tables(pos):
    half = ROPE_DIM // 2
    inv = ROPE_THETA ** (-jnp.arange(half, dtype=F32) * 2.0 / ROPE_DIM)
    ang = pos.astype(F32)[:, None] * inv[None, :]
    cos = jnp.cos(ang)
    sin = jnp.sin(ang)
    n = pos.shape[0]
    zeros = jnp.zeros((n, NSA_HD - ROPE_DIM), F32)
    zh = jnp.zeros((n, half), F32)
    cos_h = jnp.concatenate([cos, cos, jnp.ones((n, NSA_HD - ROPE_DIM), F32)], axis=1)
    up_h = jnp.concatenate([zh, sin, zeros], axis=1)
    dn_h = jnp.concatenate([-sin, zh, zeros], axis=1)
    tile2 = lambda a: jnp.concatenate([a, a], axis=1)
    return tile2(cos_h), tile2(up_h), tile2(dn_h)


def _nsa_prep(proj, tabs):
    b, s, _ = proj.shape
    t = NSA_T
    qb = Q_BLOCK
    hkv = NSA_KV_HEADS
    cols = NSA_GQA * qb
    assert t == SEL_CHUNK and s // SLC_BLOCK <= LANE
    q_blk = sum(IN_SIZES[:3]) // BRANCH_W
    kv_blk = sum(IN_SIZES[:4]) // LANE
    lane_blk = lambda n: pl.BlockSpec((1, t, LANE), lambda i, j: (i, j, n))
    tab_spec = pl.BlockSpec((t, LANE), lambda i, j: (j, 0))
    tile4 = lambda i, j: (i, 0, j, 0)
    tile5 = lambda i, j: (i, 0, j, 0, 0)
    return pl.pallas_call(
        _nsa_prep_kernel,
        grid=(b, s // t),
        in_specs=[pl.BlockSpec((1, t, BRANCH_W), lambda i, j: (i, j, q_blk)),
                  lane_blk(kv_blk + 2), lane_blk(kv_blk + 3), lane_blk(kv_blk + 4), lane_blk(kv_blk + 5),
                  lane_blk(BG_OFF // LANE), tab_spec, tab_spec, tab_spec],
        out_specs=[pl.BlockSpec((1, hkv, t // qb, LANE, cols), tile5),
                   pl.BlockSpec((1, hkv, t // qb, 8, cols), tile5),
                   pl.BlockSpec((1, hkv, t, 2 * LANE), tile4),
                   pl.BlockSpec((1, hkv, t, LANE), tile4),
                   pl.BlockSpec((1, hkv, 1, V_ROWS, t), tile5),
                   pl.BlockSpec((1, hkv, t // qb, V_ROWS, qb), tile5),
                   pl.BlockSpec((1, hkv, t // qb, V_ROWS, qb), tile5)],
        out_shape=[jax.ShapeDtypeStruct((b, hkv, s // qb, LANE, cols), F32),
                   jax.ShapeDtypeStruct((b, hkv, s // qb, 8, cols), F32),
                   jax.ShapeDtypeStruct((b, hkv, s, 2 * LANE), MM),
                   jax.ShapeDtypeStruct((b, hkv, s, LANE), MM),
                   jax.ShapeDtypeStruct((b, hkv, s // t, V_ROWS, t), MM),
                   jax.ShapeDtypeStruct((b, hkv, s // qb, V_ROWS, qb), MM),
                   jax.ShapeDtypeStruct((b, hkv, s // qb, V_ROWS, qb), MM)],
        compiler_params=_cparams("parallel", "parallel"),
        name="nsa_prep",
    )(proj, proj, proj, proj, proj, proj, *tabs)


def _cmp_kernel(xk_ref, xv_ref, pos_ref, w1_ref, w2_ref, cos_ref, up_ref, dn_ref, kc_ref, vct_ref):
    nh = kc_ref.shape[2]
    hid_w = NSA_KV_HEADS * CMP_HIDDEN
    for kind, x_ref in enumerate((xk_ref, xv_ref)):
        lo = jnp.zeros((nh, hid_w), F32)
        hi = jnp.zeros((nh, hid_w), F32)
        posb = jnp.zeros((8, hid_w), F32)
        for j in range(CMP_STRIDE):
            xj = x_ref[0, pl.ds(j, nh, stride=CMP_STRIDE), :].astype(MM)
            lo = lo + _dot(xj, w1_ref[kind, j])
            hi = hi + _dot(xj, w1_ref[kind, CMP_STRIDE + j])
        for j in range(CMP_LEN):
            posb = posb + _dot(pos_ref[kind, j].astype(MM), w1_ref[kind, j])
        hid = jax.nn.gelu(lo + pltpu.roll(hi, nh - 1, 0) + posb[0:1, :])
        for h in range(NSA_KV_HEADS):
            out = _dot(hid[:, h * CMP_HIDDEN:(h + 1) * CMP_HIDDEN].astype(MM), w2_ref[kind])
            if kind == 0:
                kc_ref[0, h] = _rope_lanes(out, cos_ref[...], up_ref[...], dn_ref[...])
            else:
                vct_ref[0, h] = out.T.astype(MM)


def _nsa_compress(proj, pos, w1, w2, tabs):
    b, s, _ = proj.shape
    nh = s // CMP_STRIDE
    hkv = NSA_KV_HEADS
    kv_blk = sum(IN_SIZES[:4]) // LANE
    once = pl.Buffered(1)
    c2 = lambda i: (0, 0)
    c3 = lambda i: (0, 0, 0)
    c4 = lambda i: (0, 0, 0, 0)
    return pl.pallas_call(
        _cmp_kernel,
        grid=(b,),
        in_specs=[pl.BlockSpec((1, s, LANE), lambda i: (i, 0, kv_blk)),
                  pl.BlockSpec((1, s, LANE), lambda i: (i, 0, kv_blk + 1)),
                  pl.BlockSpec(pos.shape, c4),
                  pl.BlockSpec(w1.shape, c4, pipeline_mode=once),
                  pl.BlockSpec(w2.shape, c3),
                  pl.BlockSpec((nh, LANE), c2), pl.BlockSpec((nh, LANE), c2), pl.BlockSpec((nh, LANE), c2)],
        out_specs=[pl.BlockSpec((1, hkv, nh, LANE), lambda i: (i, 0, 0, 0)),
                   pl.BlockSpec((1, hkv, LANE, nh), lambda i: (i, 0, 0, 0))],
        out_shape=[jax.ShapeDtypeStruct((b, hkv, nh, LANE), F32),
                   jax.ShapeDtypeStruct((b, hkv, LANE, nh), MM)],
        compiler_params=_cparams("parallel"),
        name="nsa_compress",
    )(proj, proj, pos, w1, w2, *tabs)


def _softmax_update(s, vt, m, acc):
    m_new = jnp.maximum(m, jnp.max(s, axis=0, keepdims=True))
    alpha = jnp.exp2(m - m_new)
    p = jnp.exp2(s - m_new)
    acc = alpha * acc + _dot(vt, p.astype(MM))
    return m_new, acc


def _normalise(acc):
    return acc[:NSA_HD, :] * (1.0 / acc[NSA_HD:NSA_HD + 1, :])


def _nsa_attn_kernel(q_ref, gt_ref, kc_ref, vc_ref, ka_ref, vs_ref, vd_ref, kw_ref, vw_ref, o_ref,
                     ps_ref, sa_ref, sb_ref):
    blk = pl.program_id(2)
    qb = Q_BLOCK
    cols = NSA_GQA * qb
    ncmp_pad = kc_ref.shape[2]
    nsel = LANE
    qf = q_ref[0, 0, 0]
    qm = qf.astype(MM)
    t = blk * qb + (lax.broadcasted_iota(jnp.int32, (1, cols), 1) & (qb - 1))
    lower = lax.broadcasted_iota(jnp.int32, (qb, 1), 0) <= (t & (qb - 1))

    s_w = []
    v_w = []
    for back in range(WIN // qb, -1, -1):
        c = blk - back
        cc = jnp.maximum(c, 0)
        off = pl.multiple_of(cc * qb, qb)
        s = _dot(kw_ref[0, 0, pl.ds(off, qb), :], qm)
        if back == WIN // qb:
            s = jnp.where(lower | (c < 0), NEG, s)
        elif back == 0:
            s = jnp.where(lower, s, NEG)
        else:
            s = jnp.where(c < 0, NEG, s)
        s_w.append(s)
        v_w.append(vw_ref[0, 0, cc])
    s_w = jnp.concatenate(s_w, axis=0)
    p_w = jnp.exp2(s_w - jnp.max(s_w, axis=0, keepdims=True))
    o_w = _normalise(_dot(jnp.concatenate(v_w, axis=1), p_w.astype(MM)))

    s_c = _dot(kc_ref[0, 0], qf)
    n_id = lax.broadcasted_iota(jnp.int32, (ncmp_pad, 1), 0)
    s_c = jnp.where(n_id * CMP_STRIDE + (CMP_LEN - 1) <= t, s_c, NEG)
    e_c = jnp.exp2(s_c - jnp.max(s_c, axis=0, keepdims=True))
    den = jnp.sum(e_c, axis=0, keepdims=True)
    p_c = e_c * jnp.where(t >= CMP_LEN - 1, 1.0 / den, 0.0)
    o_c = _dot(vc_ref[0, 0], p_c.astype(MM))[:NSA_HD, :]

    r = SLC_BLOCK // CMP_STRIDE
    nrow = ncmp_pad // r
    ps_ref[...] = p_c[:, 0:qb] + p_c[:, qb:2 * qb] + p_c[:, 2 * qb:3 * qb] + p_c[:, 3 * qb:4 * qb]
    imp = ps_ref[pl.ds(0, nrow, stride=r), :]
    for k in range(1, r - 1):
        imp = imp + ps_ref[pl.ds(k, nrow, stride=r), :]
    edge = 0.5 * ps_ref[pl.ds(r - 1, nrow, stride=r), :]
    first = lax.broadcasted_iota(jnp.int32, (nrow, qb), 0) == 0
    imp = imp + edge + jnp.where(first, 0.0, pltpu.roll(edge, 1, 0))
    if nrow < nsel:
        imp = jnp.concatenate([imp, jnp.zeros((nsel - nrow, qb), F32)], axis=0)

    cur = (blk * qb + lax.broadcasted_iota(jnp.int32, (1, qb), 1)) // SLC_BLOCK
    jid = lax.broadcasted_iota(jnp.int32, (nsel, qb), 0)
    valid = jid <= cur
    forced = (jid == 0) | (jid == cur) | (jid == cur - 1)
    score0 = jnp.where(valid, jnp.where(forced, FORCE_SCORE, imp), -1.0)
    score = jnp.where(forced & valid, -3e38, score0)
    for _ in range(SLC_TOPK - 3):
        mx = jnp.max(score, axis=0, keepdims=True)
        idx = jnp.min(jnp.where(score == mx, jid, nsel), axis=0, keepdims=True)
        score = jnp.where(jid == idx, -3e38, score)
    own = 2 * blk * (qb // (2 * SLC_BLOCK))
    bias = jnp.where((score < -1e38) & (score0 >= 0.0) & (jid < own), 0.0, NEG).astype(MM)
    q_aug = jnp.concatenate([jnp.concatenate([bias] * NSA_GQA, axis=1), qm], axis=0)

    kb = SEL_CHUNK
    nchunk = ka_ref.shape[2] // kb
    diag = blk * qb

    def scores_to(s_ref, j):
        off = pl.multiple_of(jnp.minimum(j, nchunk - 1) * kb, kb)
        s_ref[...] = _dot(ka_ref[0, 0, pl.ds(off, kb), :], q_aug)

    def update_from(s_ref, j, m, acc):
        return _softmax_update(s_ref[...], vs_ref[0, 0, jnp.minimum(j, nchunk - 1)], m, acc)

    def pair_body(i, carry):
        scores_to(sb_ref, 2 * i + 1)
        carry = update_from(sa_ref, 2 * i, *carry)
        scores_to(sa_ref, 2 * i + 2)
        return update_from(sb_ref, 2 * i + 1, *carry)

    nneed = (diag + kb - 1) // kb
    scores_to(sa_ref, 0)
    s_d = _dot(ka_ref[0, 0, pl.ds(pl.multiple_of(diag, qb), qb), LANE:2 * LANE], qm)
    init = _softmax_update(jnp.where(lower, s_d, NEG), vd_ref[0, 0, blk],
                           jnp.full((1, cols), NEG, F32), jnp.zeros((V_ROWS, cols), F32))
    o_s = _normalise(lax.fori_loop(0, (nneed + 1) // 2, pair_body, init)[1])

    g = jax.nn.sigmoid(gt_ref[0, 0, 0])
    o = g[0:1, :] * o_c + g[1:2, :] * o_s + g[2:3, :] * o_w
    for gp in range(NSA_GQA // 2):
        pair = jnp.concatenate([o[:, 2 * gp * qb:(2 * gp + 1) * qb],
                                o[:, (2 * gp + 1) * qb:(2 * gp + 2) * qb]], axis=0)
        o_ref[0, :, gp * LANE:(gp + 1) * LANE] = pair.T


def _nsa_attn(qt, gt, kc, vct, ka, vst, vsd, kw, vwt):
    b, hkv, nqb, _, cols = qt.shape
    s = ka.shape[2]
    ncmp_pad = kc.shape[2]
    assert CMP_LEN == 2 * CMP_STRIDE and SLC_BLOCK % CMP_STRIDE == 0 and s // SLC_BLOCK <= LANE
    assert (s // SEL_CHUNK) % 2 == 0
    per_head = lambda i, h, j: (i, h, 0, 0)
    per_head5 = lambda i, h, j: (i, h, 0, 0, 0)
    per_blk = lambda i, h, j: (i, h, j, 0, 0)
    return pl.pallas_call(
        _nsa_attn_kernel,
        grid=(b, hkv, nqb),
        in_specs=[pl.BlockSpec((1, 1, 1, LANE, cols), per_blk),
                  pl.BlockSpec((1, 1, 1, 8, cols), per_blk),
                  pl.BlockSpec((1, 1, ncmp_pad, LANE), per_head),
                  pl.BlockSpec((1, 1, LANE, ncmp_pad), per_head),
                  pl.BlockSpec((1, 1, s, 2 * LANE), per_head),
                  pl.BlockSpec((1, 1, s // SEL_CHUNK, V_ROWS, SEL_CHUNK), per_head5),
                  pl.BlockSpec((1, 1, s // Q_BLOCK, V_ROWS, Q_BLOCK), per_head5),
                  pl.BlockSpec((1, 1, s, LANE), per_head),
                  pl.BlockSpec((1, 1, s // Q_BLOCK, V_ROWS, Q_BLOCK), per_head5)],
        out_specs=pl.BlockSpec((1, Q_BLOCK, NSA_GQA * NSA_HD), lambda i, h, j: (i, j, h)),
        out_shape=jax.ShapeDtypeStruct((b, s, hkv * NSA_GQA * NSA_HD), F32),
        scratch_shapes=[pltpu.VMEM((ncmp_pad, Q_BLOCK), F32),
                        pltpu.VMEM((SEL_CHUNK, cols), F32), pltpu.VMEM((SEL_CHUNK, cols), F32)],
        compiler_params=_cparams("parallel", "parallel", "arbitrary"),
        name="nsa_attn",
    )(qt, gt, kc, vct, ka, vst, vsd, kw, vwt)


def _nsa_weights(pos_k, pos_v, ck_w1, ck_w2, cv_w1, cv_w2):
    dh = NSA_HD
    eye = jnp.eye(NSA_KV_HEADS, dtype=F32)
    slabs = lambda w: jnp.einsum("jrc,hg->jhrgc", w.reshape(CMP_LEN, dh, CMP_HIDDEN), eye).reshape(
        CMP_LEN, NSA_KV_HEADS * dh, NSA_KV_HEADS * CMP_HIDDEN)
    w1 = jnp.stack([slabs(ck_w1), slabs(cv_w1)]).astype(MM)
    w2 = jnp.pad(jnp.stack([ck_w2, cv_w2]), ((0, 0), (0, 0), (0, LANE - dh))).astype(MM)
    pos = jnp.stack([pos_k, pos_v])
    pos = jnp.broadcast_to(jnp.concatenate([pos] * NSA_KV_HEADS, axis=-1)[:, :, None, :],
                           (2, CMP_LEN, 8, NSA_KV_HEADS * dh))
    return pos, w1, w2


def _nsa(proj, weights, tabs):
    tok_tabs, cmp_tabs = tabs
    qt, gt, ka, kw, vst, vsd, vwt = _nsa_prep(proj, tok_tabs)
    kc, vct = _nsa_compress(proj, *weights, cmp_tabs)
    return _nsa_attn(qt, gt, kc, vct, ka, vst, vsd, kw, vwt)


def _nsa_tables(s):
    nhalf = s // CMP_STRIDE
    tok_tabs = _nsa_rope_tables(jnp.arange(s))
    cmp_tabs = _nsa_rope_tables(jnp.arange(nhalf) * CMP_STRIDE + CMP_LEN - 1)
    return tok_tabs, cmp_tabs


def _out_kernel(x_ref, g_ref, yl_ref, ys_ref, yr_ref, yn_ref, wgm_ref, wb_ref, wo_ref,
                fg_ref, o_ref, *, final):
    x = x_ref[0]
    h = _rms(x, g_ref[...]).astype(MM)
    w = BRANCH_W
    d = D_MODEL
    m0 = N_BRANCH * w
    merged = jnp.zeros(x.shape, F32)
    for n, y_ref in enumerate((yl_ref, ys_ref, yr_ref, yn_ref)):
        gate = jax.nn.silu(_dot(h, wgm_ref[0, :, n * w:(n + 1) * w]))
        branch = _dot((y_ref[0] * gate).astype(MM), wb_ref[n])
        merged = merged + jax.nn.sigmoid(_dot(h, wgm_ref[0, :, m0 + n * d:m0 + (n + 1) * d])) * branch
    out = x + _dot(merged.astype(MM), wo_ref[...])
    if final:
        out = _rms(out, fg_ref[...])
    o_ref[0] = out


def _out(x, g, ys, wgm, l, wb, wo, fg, final):
    b, s, d = x.shape
    w = BRANCH_W
    tm = TM_PROJ
    row = lambda i, j: (i, j, 0)
    c2 = lambda i, j: (0, 0)
    once = pl.Buffered(1)
    return pl.pallas_call(
        functools.partial(_out_kernel, final=final),
        grid=(b, s // tm),
        in_specs=[pl.BlockSpec((1, tm, d), row),
                  pl.BlockSpec((1, d), c2)]
                 + [pl.BlockSpec((1, tm, w), row)] * N_BRANCH
                 + [pl.BlockSpec((1, d, N_BRANCH * (w + d)), lambda i, j: (l, 0, 0), pipeline_mode=once),
                    pl.BlockSpec((N_BRANCH, w, d), lambda i, j: (0, 0, 0), pipeline_mode=once),
                    pl.BlockSpec((d, d), c2, pipeline_mode=once),
                    pl.BlockSpec((1, d), c2)],
        out_specs=pl.BlockSpec((1, tm, d), row),
        out_shape=jax.ShapeDtypeStruct((b, s, d), F32),
        compiler_params=_cparams("parallel", "arbitrary"),
        name="gate_merge_out",
    )(x, g, *ys, wgm, wb, wo, fg)


def _split_w_in(w_in):
    pad = jnp.zeros(w_in.shape[:2] + (PROJ_W - GATE_OFF,), w_in.dtype)
    w_mix = jnp.concatenate([w_in[:, :, :GATE_OFF], pad], axis=2).astype(MM)
    return w_mix, w_in[:, :, GATE_OFF:].astype(MM)


def _layer(x, p, l, w_mix, w_gm, ret_tabs, nsa_tabs, final_g):
    g = p["norm_g"][l].reshape(1, D_MODEL)
    proj = _inproj(x, g, w_mix, l)

    wab = jnp.concatenate([_block_diag(p["lru_wa"][l]), _block_diag(p["lru_wx"][l])], axis=1).astype(MM)
    bab = jnp.concatenate([p["lru_ba"][l], p["lru_bx"][l]]).reshape(1, 2 * BRANCH_W)
    y_lru = _lru(proj, p["lru_conv_w"][l], p["lru_conv_b"][l].reshape(1, -1), wab, bab,
                 p["lru_lambda"][l].reshape(1, -1))

    wb, apr, api, wc = _s5_tables(p["s5_lambda_re"][l], p["s5_lambda_im"][l], p["s5_b_re"][l],
                                  p["s5_b_im"][l], p["s5_c_re"][l], p["s5_c_im"][l], p["s5_log_dt"][l])
    y_s5 = _s5(proj, wb, apr, api, wc, p["s5_d"][l].reshape(1, -1), p["s5_w_glu"][l].astype(MM),
               p["s5_b_glu"][l].reshape(1, -1))

    y_ret = _retention(proj, ret_tabs)

    nsa_w = _nsa_weights(p["nsa_pos_k"][l], p["nsa_pos_v"][l], p["nsa_ck_w1"][l], p["nsa_ck_w2"][l],
                         p["nsa_cv_w1"][l], p["nsa_cv_w2"][l])
    y_nsa = _nsa(proj, nsa_w, nsa_tabs)

    final = l == DEPTH - 1
    return _out(x, g, (y_lru, y_s5, y_ret, y_nsa), w_gm, l, p["w_branch"][l].astype(MM),
                p["w_out"][l].astype(MM), final_g.reshape(1, D_MODEL), final)


def kernel(x, norm_g, w_in, lru_conv_w, lru_conv_b, lru_wa, lru_ba, lru_wx, lru_bx, lru_lambda,
           s5_lambda_re, s5_lambda_im, s5_b_re, s5_b_im, s5_c_re, s5_c_im, s5_log_dt, s5_d,
           s5_w_glu, s5_b_glu, nsa_pos_k, nsa_pos_v, nsa_ck_w1, nsa_ck_w2, nsa_cv_w1, nsa_cv_w2,
           w_branch, w_out, final_norm_g):
    p = dict(norm_g=norm_g, w_in=w_in, lru_conv_w=lru_conv_w, lru_conv_b=lru_conv_b, lru_wa=lru_wa,
             lru_ba=lru_ba, lru_wx=lru_wx, lru_bx=lru_bx, lru_lambda=lru_lambda,
             s5_lambda_re=s5_lambda_re, s5_lambda_im=s5_lambda_im, s5_b_re=s5_b_re, s5_b_im=s5_b_im,
             s5_c_re=s5_c_re, s5_c_im=s5_c_im, s5_log_dt=s5_log_dt, s5_d=s5_d, s5_w_glu=s5_w_glu,
             s5_b_glu=s5_b_glu, nsa_pos_k=nsa_pos_k, nsa_pos_v=nsa_pos_v, nsa_ck_w1=nsa_ck_w1,
             nsa_ck_w2=nsa_ck_w2, nsa_cv_w1=nsa_cv_w1, nsa_cv_w2=nsa_cv_w2, w_branch=w_branch,
             w_out=w_out)
    s = x.shape[1]
    ret_tabs = _ret_tables(s)
    nsa_tabs = _nsa_tables(s)
    w_mix, w_gm = _split_w_in(w_in)
    for l in range(DEPTH):
        x = _layer(x, p, l, w_mix, w_gm, ret_tabs, nsa_tabs, final_norm_g)
    return x
```

```python
import functools
import math

import jax
import jax.numpy as jnp
from jax import lax
from jax.experimental import pallas as pl
from jax.experimental.pallas import tpu as pltpu

F32 = jnp.float32
MM = jnp.bfloat16

D_MODEL = 1024
DEPTH = 4
N_BRANCH = 4
BRANCH_W = 512
NORM_EPS = 1e-6
LRU_HEADS = 8
LRU_HD = BRANCH_W // LRU_HEADS
CONV_W = 4
LRU_C = 8.0
S5_GROUP = 16
S5_GROUPS = BRANCH_W // S5_GROUP
S5_STATE = 64
S5_N = S5_GROUPS * S5_STATE
RET_HEADS = 4
RET_HD = BRANCH_W // RET_HEADS
RET_CHUNK = 128
RET_ROPE_BASE = 10000.0
NSA_HEADS = 8
NSA_KV_HEADS = 2
NSA_HD = BRANCH_W // NSA_HEADS
NSA_GQA = NSA_HEADS // NSA_KV_HEADS
CMP_LEN = 32
CMP_STRIDE = 16
CMP_HIDDEN = 256
SLC_BLOCK = 64
SLC_TOPK = 16
WIN = 512
Q_BLOCK = 128
ROPE_THETA = 500000.0
ROPE_DIM = NSA_HD // 4
FORCE_SCORE = 1e4
NEG = -1e30
NSA_KV_W = 6 * NSA_KV_HEADS * NSA_HD
IN_SIZES = (BRANCH_W, BRANCH_W, 3 * BRANCH_W, NSA_HEADS * NSA_HD, NSA_KV_W, 3 * NSA_HEADS,
            N_BRANCH * BRANCH_W, N_BRANCH * D_MODEL)
LANE = 128
MIX_W = sum(IN_SIZES[:5])
BG_OFF = MIX_W
GATE_OFF = BG_OFF + IN_SIZES[5]
MERGE_OFF = GATE_OFF + IN_SIZES[6]
PROJ_W = MIX_W + LANE
VMEM_LIMIT = 56 * 1024 * 1024

TM_PROJ = 256
T_LRU = 256
T_S5 = 256
T_RET = 512
S5_TILE_GROUP = 4
SEL_CHUNK = 512
NSA_T = SEL_CHUNK
V_ROWS = NSA_HD + 16


def _cparams(*sem):
    return pltpu.CompilerParams(dimension_semantics=sem, vmem_limit_bytes=VMEM_LIMIT)


def _rms(x, g):
    ms = jnp.mean(x * x, axis=-1, keepdims=True)
    return (x * lax.rsqrt(ms + NORM_EPS)) * g


def _dot(a, b):
    return jnp.dot(a, b, preferred_element_type=F32)


def _dot_nt(a, b):
    return lax.dot_general(a, b, (((1,), (1,)), ((), ())), preferred_element_type=F32)


def _dot_tn(a, b):
    return lax.dot_general(a, b, (((0,), (0,)), ((), ())), preferred_element_type=F32)


def _inproj_kernel(x_ref, g_ref, w_ref, o_ref):
    h = _rms(x_ref[0], g_ref[...])
    o_ref[0] = _dot(h.astype(MM), w_ref[0])


def _inproj(x, g, w, l):
    b, s, d = x.shape
    n = w.shape[2]
    return pl.pallas_call(
        _inproj_kernel,
        grid=(b, s // TM_PROJ),
        in_specs=[pl.BlockSpec((1, TM_PROJ, d), lambda i, j: (i, j, 0)),
                  pl.BlockSpec((1, d), lambda i, j: (0, 0)),
                  pl.BlockSpec((1, d, n), lambda i, j: (l, 0, 0))],
        out_specs=pl.BlockSpec((1, TM_PROJ, n), lambda i, j: (i, j, 0)),
        out_shape=jax.ShapeDtypeStruct((b, s, n), F32),
        compiler_params=_cparams("parallel", "arbitrary"),
        name="inproj",
    )(x, g, w)


def _lru_kernel(u_ref, cw_ref, cb_ref, wab_ref, bab_ref, lam_ref, o_ref, ubuf, hcar):
    t = T_LRU
    w = BRANCH_W

    @pl.when(pl.program_id(1) == 0)
    def _():
        ubuf[0:8, :] = jnp.zeros((8, w), F32)
        hcar[...] = jnp.zeros((1, w), F32)

    ubuf[8:8 + t, :] = u_ref[0]
    xc = cb_ref[...] + cw_ref[0:1, :] * ubuf[5:5 + t, :]
    for k in range(1, CONV_W):
        xc = xc + cw_ref[k:k + 1, :] * ubuf[5 + k:5 + k + t, :]
    ubuf[0:8, :] = ubuf[t:t + 8, :]

    ri = _dot(xc.astype(MM), wab_ref[...]) + bab_ref[...]
    r = jax.nn.sigmoid(ri[:, :w])
    gi = jax.nn.sigmoid(ri[:, w:])
    z = -lam_ref[...]
    softplus = jnp.maximum(z, 0.0) + jnp.log1p(jnp.exp(-jnp.abs(z)))
    a = jnp.exp(-LRU_C * r * softplus)
    bt = jnp.sqrt(1.0 - a * a) * gi * xc

    row = lax.broadcasted_iota(jnp.int32, (t, w), 0)
    k = 1
    while k < t:
        keep = row >= k
        a_sh = jnp.where(keep, pltpu.roll(a, k, 0), 1.0)
        b_sh = jnp.where(keep, pltpu.roll(bt, k, 0), 0.0)
        bt = a * b_sh + bt
        a = a * a_sh
        k *= 2
    h = bt + a * hcar[...]
    o_ref[0] = h
    hcar[...] = h[t - 1:t, :]


def _lru(proj, cw, cb, wab, bab, lam):
    b, s, _ = proj.shape
    w = BRANCH_W
    return pl.pallas_call(
        _lru_kernel,
        grid=(b, s // T_LRU),
        in_specs=[pl.BlockSpec((1, T_LRU, w), lambda i, j: (i, j, 0)),
                  pl.BlockSpec((CONV_W, w), lambda i, j: (0, 0)),
                  pl.BlockSpec((1, w), lambda i, j: (0, 0)),
                  pl.BlockSpec((w, 2 * w), lambda i, j: (0, 0)),
                  pl.BlockSpec((1, 2 * w), lambda i, j: (0, 0)),
                  pl.BlockSpec((1, w), lambda i, j: (0, 0))],
        out_specs=pl.BlockSpec((1, T_LRU, w), lambda i, j: (i, j, 0)),
        out_shape=jax.ShapeDtypeStruct((b, s, w), F32),
        scratch_shapes=[pltpu.VMEM((T_LRU + 8, w), F32), pltpu.VMEM((1, w), F32)],
        compiler_params=_cparams("parallel", "arbitrary"),
        name="rg_lru",
    )(proj, cw, cb, wab, bab, lam)


def _block_diag(blocks):
    n, r, c = blocks.shape
    eye = jnp.eye(n, dtype=blocks.dtype)
    return jnp.einsum("nrc,nm->nrmc", blocks, eye).reshape(n * r, n * c)


def _cmul_add(xr, xi, ar, ai, br, bi):
    return xr + (ar * br - ai * bi), xi + (ar * bi + ai * br)


def _s5_kernel(u_ref, wb_ref, apr_ref, api_ref, wc_ref, d_ref, wg_ref, bg_ref, o_ref,
               car, lr_ref, li_ref, y_ref):
    t = T_S5
    n = S5_N
    w = BRANCH_W
    sub = 8
    grp = t // sub
    mxu = 256

    @pl.when(pl.program_id(1) == 0)
    def _():
        car[...] = jnp.zeros((2, n), F32)

    u = u_ref[0]
    ub = u.astype(MM)
    y_ref[...] = d_ref[...] * u
    first = lax.broadcasted_iota(jnp.int32, (t, LANE), 0) == 0
    i8 = lax.broadcasted_iota(jnp.int32, (sub, LANE), 0)
    rowg = lax.broadcasted_iota(jnp.int32, (grp, LANE), 0)
    rep = lax.broadcasted_iota(jnp.int32, (t, grp), 0) // sub == lax.broadcasted_iota(jnp.int32, (t, grp), 1)
    rep = jnp.where(rep, 1.0, 0.0).astype(MM)
    rep3 = jnp.concatenate([rep, rep, rep], axis=1)
    last = pl.ds(sub - 1, grp, stride=sub)

    ntile = n // LANE
    for c0 in range(0, ntile, S5_TILE_GROUP):
        tiles = range(c0, c0 + S5_TILE_GROUP)
        lanes = [slice(cb * LANE, (cb + 1) * LANE) for cb in tiles]
        ios = [slice(io * mxu, (io + 1) * mxu) for io in ((cb * LANE // S5_STATE * S5_GROUP) // mxu for cb in tiles)]
        aprs = [apr_ref[:, ln] for ln in lanes]
        apis = [api_ref[:, ln] for ln in lanes]
        xs = [_dot(ub[:, io], wb_ref[cb]) for cb, io in zip(tiles, ios)]
        loc = []
        for x, apr, api, ln in zip(xs, aprs, apis, lanes):
            cr, ci = _cmul_add(0.0, 0.0, apr[0:1], api[0:1], car[0:1, ln], car[1:2, ln])
            xr = (x[:, :LANE] + jnp.where(first, cr, 0.0)).reshape(grp, sub, LANE)
            xi = (x[:, LANE:] + jnp.where(first, ci, 0.0)).reshape(grp, sub, LANE)
            k = 1
            while k < sub:
                ar = jnp.where(i8 >= k, apr[k - 1:k], 0.0)[None]
                ai = jnp.where(i8 >= k, api[k - 1:k], 0.0)[None]
                xr, xi = _cmul_add(xr, xi, ar, ai, pltpu.roll(xr, k, 1), pltpu.roll(xi, k, 1))
                k *= 2
            loc.append((xr, xi))
        for j, (xr, xi) in enumerate(loc):
            lr_ref[j] = xr.reshape(t, LANE)
            li_ref[j] = xi.reshape(t, LANE)
        ends = [(lr_ref[j, last, :], li_ref[j, last, :]) for j in range(len(loc))]
        k = 1
        step = sub - 1
        while k < grp:
            keep = rowg >= k
            nxt = []
            for (fr, fi), apr, api in zip(ends, aprs, apis):
                sr = jnp.where(keep, pltpu.roll(fr, k, 0), 0.0)
                si = jnp.where(keep, pltpu.roll(fi, k, 0), 0.0)
                nxt.append(_cmul_add(fr, fi, apr[step:step + 1], api[step:step + 1], sr, si))
            ends = nxt
            k *= 2
            step += 1
        enter = rowg >= 1
        zs = []
        for (fr, fi), ln in zip(ends, lanes):
            car[0:1, ln] = fr[grp - 1:grp, :]
            car[1:2, ln] = fi[grp - 1:grp, :]
            e = jnp.concatenate([jnp.where(enter, pltpu.roll(fr, 1, 0), 0.0),
                                 jnp.where(enter, pltpu.roll(fi, 1, 0), 0.0)], axis=1)
            e_hi = e.astype(MM)
            r1 = e - e_hi.astype(F32)
            e_mid = r1.astype(MM)
            e_lo = (r1 - e_mid.astype(F32)).astype(MM)
            zs.append(_dot(rep3, jnp.concatenate([e_hi, e_mid, e_lo], axis=0)))
        for cb, io, (xr, xi), z, apr, api in zip(tiles, ios, loc, zs, aprs, apis):
            hr, hi = _cmul_add(xr, xi, apr[0:sub][None], api[0:sub][None],
                               z[:, :LANE].reshape(grp, sub, LANE), z[:, LANE:].reshape(grp, sub, LANE))
            h = jnp.concatenate([hr.reshape(t, LANE), hi.reshape(t, LANE)], axis=1).astype(MM)
            y_ref[:, io] += _dot(h, wc_ref[cb])

    z = jax.nn.gelu(y_ref[...])
    g = _dot(z.astype(MM), wg_ref[...]) + bg_ref[...]
    o_ref[0] = g[:, :w] * jax.nn.sigmoid(g[:, w:])


def _s5_tables(lam_re, lam_im, b_re, b_im, c_re, c_im, log_dt):
    dt = jnp.exp(log_dt)[:, None]
    lr = jnp.minimum(lam_re, -1e-4)
    li = lam_im
    mag = jnp.exp(lr * dt)
    ab_re = mag * jnp.cos(li * dt)
    ab_im = mag * jnp.sin(li * dt)
    den = lr * lr + li * li
    nr = ab_re - 1.0
    ni = ab_im
    f_re = (nr * lr + ni * li) / den
    f_im = (ni * lr - nr * li) / den
    bb_re = f_re[..., None] * b_re - f_im[..., None] * b_im
    bb_im = f_re[..., None] * b_im + f_im[..., None] * b_re
    wb_re = _block_diag(jnp.swapaxes(bb_re, 1, 2))
    wb_im = _block_diag(jnp.swapaxes(bb_im, 1, 2))
    wc_re = _block_diag(jnp.swapaxes(c_re, 1, 2))
    wc_im = _block_diag(jnp.swapaxes(-c_im, 1, 2))
    wb, wc = [], []
    for cb in range(S5_N // LANE):
        lanes = slice(cb * LANE, (cb + 1) * LANE)
        io = (cb * LANE // S5_STATE * S5_GROUP) // 256
        ios = slice(io * 256, (io + 1) * 256)
        wb.append(jnp.concatenate([wb_re[ios, lanes], wb_im[ios, lanes]], axis=1))
        wc.append(jnp.concatenate([wc_re[lanes, ios], wc_im[lanes, ios]], axis=0))
    wb = jnp.stack(wb)
    wc = jnp.stack(wc)
    a_re = ab_re.reshape(1, S5_N)
    a_im = ab_im.reshape(1, S5_N)
    pr, pi = [a_re], [a_im]
    for _ in range(7):
        pr.append(pr[-1] * a_re - pi[-1] * a_im)
        pi.append(pr[-2] * a_im + pi[-1] * a_re)
    for _ in range(int(math.log2(T_S5 // 8)) - 1):
        r0, i0 = pr[-1], pi[-1]
        pr.append(r0 * r0 - i0 * i0)
        pi.append(2.0 * r0 * i0)
    pad = [jnp.zeros_like(a_re)] * (16 - len(pr))
    return wb.astype(MM), jnp.concatenate(pr + pad, 0), jnp.concatenate(pi + pad, 0), wc.astype(MM)


def _s5(proj, wb, apr, api, wc, d, wg, bg):
    b, s, _ = proj.shape
    w = BRANCH_W
    n = S5_N
    c2 = lambda i, j: (0, 0)
    c3 = lambda i, j: (0, 0, 0)
    return pl.pallas_call(
        _s5_kernel,
        grid=(b, s // T_S5),
        in_specs=[pl.BlockSpec((1, T_S5, w), lambda i, j: (i, j, 1)),
                  pl.BlockSpec(wb.shape, c3),
                  pl.BlockSpec(apr.shape, c2),
                  pl.BlockSpec(api.shape, c2),
                  pl.BlockSpec(wc.shape, c3),
                  pl.BlockSpec((1, w), c2),
                  pl.BlockSpec((w, 2 * w), c2),
                  pl.BlockSpec((1, 2 * w), c2)],
        out_specs=pl.BlockSpec((1, T_S5, w), lambda i, j: (i, j, 0)),
        out_shape=jax.ShapeDtypeStruct((b, s, w), F32),
        scratch_shapes=[pltpu.VMEM((2, n), F32), pltpu.VMEM((S5_TILE_GROUP, T_S5, LANE), F32),
                        pltpu.VMEM((S5_TILE_GROUP, T_S5, LANE), F32), pltpu.VMEM((T_S5, w), F32)],
        compiler_params=_cparams("parallel", "arbitrary"),
        name="s5",
    )(proj, wb, apr, api, wc, d, wg, bg)


def _ret_kernel(q_ref, k_ref, v_ref, cos_ref, sin_ref, dm_ref, xi_ref, zt_ref, cd_ref, o_ref, rst):
    c = RET_CHUNK
    dh = RET_HD

    @pl.when(pl.program_id(1) == 0)
    def _():
        rst[...] = jnp.zeros((RET_HEADS, dh, dh), F32)

    for ci in range(T_RET // c):
        rows = slice(ci * c, (ci + 1) * c)
        cos = cos_ref[rows, :]
        sin = sin_ref[rows, :]
        heads = range(RET_HEADS)
        cols = [slice(h * dh, (h + 1) * dh) for h in heads]
        qs, ks, vbs = [], [], []
        for h in heads:
            q = q_ref[0, rows, cols[h]]
            k = k_ref[0, rows, cols[h]]
            qs.append(q * cos + pltpu.roll(q, dh // 2, 1) * sin)
            ks.append((k * cos + pltpu.roll(k, dh // 2, 1) * sin) * (dh ** -0.5))
            vbs.append(v_ref[0, rows, cols[h]].astype(MM))
        scores = [_dot_nt(qs[h].astype(MM), ks[h].astype(MM)) * dm_ref[h] for h in heads]
        r_prev = [rst[h] for h in heads]
        cross = [_dot((qs[h] * xi_ref[h]).astype(MM), r_prev[h].astype(MM)) for h in heads]
        kv = [_dot_tn((ks[h] * zt_ref[h]).astype(MM), vbs[h]) for h in heads]
        inner = [_dot(scores[h].astype(MM), vbs[h]) for h in heads]
        for h in heads:
            rst[h] = cd_ref[h] * r_prev[h] + kv[h]
            o = inner[h] + cross[h]
            mu = jnp.mean(o, axis=-1, keepdims=True)
            var = jnp.mean(jnp.square(o - mu), axis=-1, keepdims=True)
            o_ref[0, rows, cols[h]] = (o - mu) * lax.rsqrt(var + 1e-5)


def _ret_tables(s):
    dh = RET_HD
    c = RET_CHUNK
    half = dh // 2
    inv = RET_ROPE_BASE ** (-jnp.arange(half, dtype=F32) * 2.0 / dh)
    ang = jnp.arange(s).astype(F32)[:, None] * inv[None, :]
    cos = jnp.cos(ang)
    sin = jnp.sin(ang)
    cosf = jnp.concatenate([cos, cos], axis=1)
    sinf = jnp.concatenate([-sin, sin], axis=1)
    gamma = 1.0 - jnp.exp(jnp.linspace(math.log(1.0 / 32.0), math.log(1.0 / 512.0), RET_HEADS, dtype=F32))
    log_g = jnp.log(gamma)
    n = jnp.arange(c, dtype=F32)
    diff = n[:, None] - n[None, :]
    dmask = jnp.where(diff[None] >= 0, jnp.exp(jnp.maximum(diff, 0.0)[None] * log_g[:, None, None]), 0.0)
    xi = jnp.exp((n[None, :] + 1.0) * log_g[:, None])
    zeta = jnp.exp((c - 1.0 - n)[None, :] * log_g[:, None])
    cdec = jnp.exp(c * log_g)
    xi_b = jnp.broadcast_to(xi[:, :, None], (RET_HEADS, c, dh))
    zt_b = jnp.broadcast_to(zeta[:, :, None], (RET_HEADS, c, dh))
    cd_b = jnp.broadcast_to(cdec[:, None, None], (RET_HEADS, dh, dh))
    return cosf, sinf, dmask, xi_b, zt_b, cd_b


def _retention(proj, tabs):
    b, s, _ = proj.shape
    w = BRANCH_W
    cosf, sinf, dmask, xi_b, zt_b, cd_b = tabs
    c = RET_CHUNK
    dh = RET_HD
    full3 = lambda i, j: (0, 0, 0)
    return pl.pallas_call(
        _ret_kernel,
        grid=(b, s // T_RET),
        in_specs=[pl.BlockSpec((1, T_RET, w), lambda i, j: (i, j, 2)),
                  pl.BlockSpec((1, T_RET, w), lambda i, j: (i, j, 3)),
                  pl.BlockSpec((1, T_RET, w), lambda i, j: (i, j, 4)),
                  pl.BlockSpec((T_RET, dh), lambda i, j: (j, 0)),
                  pl.BlockSpec((T_RET, dh), lambda i, j: (j, 0)),
                  pl.BlockSpec((RET_HEADS, c, c), full3),
                  pl.BlockSpec((RET_HEADS, c, dh), full3),
                  pl.BlockSpec((RET_HEADS, c, dh), full3),
                  pl.BlockSpec((RET_HEADS, dh, dh), full3)],
        out_specs=pl.BlockSpec((1, T_RET, w), lambda i, j: (i, j, 0)),
        out_shape=jax.ShapeDtypeStruct((b, s, w), F32),
        scratch_shapes=[pltpu.VMEM((RET_HEADS, dh, dh), F32)],
        compiler_params=_cparams("parallel", "arbitrary"),
        name="retention",
    )(proj, proj, proj, cosf, sinf, dmask, xi_b, zt_b, cd_b)


def _rope_lanes(x, cos, s_up, s_dn):
    half = ROPE_DIM // 2
    n = x.shape[-1]
    return x * cos + pltpu.roll(x, half, 1) * s_up + pltpu.roll(x, n - half, 1) * s_dn


def _nsa_prep_kernel(q_ref, ks_ref, vs_ref, kw_ref, vw_ref, bg_ref, cos_ref, up_ref, dn_ref,
                     qt_ref, gt_ref, ka_ref, kwo_ref, vst_ref, vsd_ref, vwt_ref):
    t = NSA_T
    qb = Q_BLOCK
    dh = NSA_HD
    cos = cos_ref[...]
    up = up_ref[...]
    dn = dn_ref[...]
    scale = NSA_HD ** -0.5 * math.log2(math.e)
    lane = lax.broadcasted_iota(jnp.int32, (t, LANE), 1)
    low = lane < dh
    tok = pl.program_id(1) * t + lax.broadcasted_iota(jnp.int32, (t, LANE), 0)
    onehot = jnp.where(tok // SLC_BLOCK == lane, 1.0, 0.0).astype(MM)
    ks = _rope_lanes(ks_ref[0], cos, up, dn)
    kw = _rope_lanes(kw_ref[0], cos, up, dn)
    for h in range(NSA_KV_HEADS):
        ks_h = ks if h == 0 else pltpu.roll(ks, LANE - h * dh, 1)
        kw_h = kw if h == 0 else pltpu.roll(kw, LANE - h * dh, 1)
        ka_ref[0, h, :, 0:LANE] = onehot
        ka_ref[0, h, :, LANE:2 * LANE] = jnp.where(low, ks_h, 0.0).astype(MM)
        kwo_ref[0, h] = jnp.where(low, kw_h, 0.0).astype(MM)

    def ones_row(n):
        return jnp.where(lax.broadcasted_iota(jnp.int32, (V_ROWS - dh, n), 0) == 0, 1.0, 0.0).astype(MM)

    r8 = lax.broadcasted_iota(jnp.int32, (8, NSA_GQA * qb), 0)
    for i in range(t // qb):
        rows = slice(i * qb, (i + 1) * qb)
        vs_t = vs_ref[0, rows, :].T
        vw_t = vw_ref[0, rows, :].T
        bg_t = bg_ref[0, rows, :].T
        for h in range(NSA_KV_HEADS):
            vst_ref[0, h, 0, 0:dh, rows] = vs_t[h * dh:(h + 1) * dh].astype(MM)
            vsd_ref[0, h, i, 0:dh, :] = vs_t[h * dh:(h + 1) * dh].astype(MM)
            vsd_ref[0, h, i, dh:V_ROWS, :] = ones_row(qb)
            vwt_ref[0, h, i, 0:dh, :] = vw_t[h * dh:(h + 1) * dh].astype(MM)
            vwt_ref[0, h, i, dh:V_ROWS, :] = ones_row(qb)
            qt_ref[0, h, i, dh:LANE, :] = jnp.zeros((LANE - dh, NSA_GQA * qb), F32)
            gates = jnp.zeros((8, NSA_GQA * qb), F32)
            for br in range(3):
                base = br * NSA_HEADS + h * NSA_GQA
                row = jnp.concatenate([bg_t[base + g:base + g + 1] for g in range(NSA_GQA)], axis=1)
                gates = jnp.where(r8 == br, row, gates)
            gt_ref[0, h, i] = gates
        for cb in range(BRANCH_W // LANE):
            cols = slice(cb * LANE, (cb + 1) * LANE)
            x_t = (_rope_lanes(q_ref[0, rows, cols], cos[rows], up[rows], dn[rows]) * scale).T
            h = (2 * cb) // NSA_GQA
            g0 = (2 * cb) % NSA_GQA
            qt_ref[0, h, i, 0:dh, g0 * qb:(g0 + 1) * qb] = x_t[0:dh]
            qt_ref[0, h, i, 0:dh, (g0 + 1) * qb:(g0 + 2) * qb] = x_t[dh:2 * dh]
    for h in range(NSA_KV_HEADS):
        vst_ref[0, h, 0, dh:V_ROWS, :] = ones_row(t)


def _nsa_rope_tables(pos):
    half = ROPE_DIM // 2
    inv = ROPE_THETA ** (-jnp.arange(half, dtype=F32) * 2.0 / ROPE_DIM)
    ang = pos.astype(F32)[:, None] * inv[None, :]
    cos = jnp.cos(ang)
    sin = jnp.sin(ang)
    n = pos.shape[0]
    zeros = jnp.zeros((n, NSA_HD - ROPE_DIM), F32)
    zh = jnp.zeros((n, half), F32)
    cos_h = jnp.concatenate([cos, cos, jnp.ones((n, NSA_HD - ROPE_DIM), F32)], axis=1)
    up_h = jnp.concatenate([zh, sin, zeros], axis=1)
    dn_h = jnp.concatenate([-sin, zh, zeros], axis=1)
    tile2 = lambda a: jnp.concatenate([a, a], axis=1)
    return tile2(cos_h), tile2(up_h), tile2(dn_h)


def _nsa_prep(proj, tabs):
    b, s, _ = proj.shape
    t = NSA_T
    qb = Q_BLOCK
    hkv = NSA_KV_HEADS
    cols = NSA_GQA * qb
    assert t == SEL_CHUNK and s // SLC_BLOCK <= LANE
    q_blk = sum(IN_SIZES[:3]) // BRANCH_W
    kv_blk = sum(IN_SIZES[:4]) // LANE
    lane_blk = lambda n: pl.BlockSpec((1, t, LANE), lambda i, j: (i, j, n))
    tab_spec = pl.BlockSpec((t, LANE), lambda i, j: (j, 0))
    tile4 = lambda i, j: (i, 0, j, 0)
    tile5 = lambda i, j: (i, 0, j, 0, 0)
    return pl.pallas_call(
        _nsa_prep_kernel,
        grid=(b, s // t),
        in_specs=[pl.BlockSpec((1, t, BRANCH_W), lambda i, j: (i, j, q_blk)),
                  lane_blk(kv_blk + 2), lane_blk(kv_blk + 3), lane_blk(kv_blk + 4), lane_blk(kv_blk + 5),
                  lane_blk(BG_OFF // LANE), tab_spec, tab_spec, tab_spec],
        out_specs=[pl.BlockSpec((1, hkv, t // qb, LANE, cols), tile5),
                   pl.BlockSpec((1, hkv, t // qb, 8, cols), tile5),
                   pl.BlockSpec((1, hkv, t, 2 * LANE), tile4),
                   pl.BlockSpec((1, hkv, t, LANE), tile4),
                   pl.BlockSpec((1, hkv, 1, V_ROWS, t), tile5),
                   pl.BlockSpec((1, hkv, t // qb, V_ROWS, qb), tile5),
                   pl.BlockSpec((1, hkv, t // qb, V_ROWS, qb), tile5)],
        out_shape=[jax.ShapeDtypeStruct((b, hkv, s // qb, LANE, cols), F32),
                   jax.ShapeDtypeStruct((b, hkv, s // qb, 8, cols), F32),
                   jax.ShapeDtypeStruct((b, hkv, s, 2 * LANE), MM),
                   jax.ShapeDtypeStruct((b, hkv, s, LANE), MM),
                   jax.ShapeDtypeStruct((b, hkv, s // t, V_ROWS, t), MM),
                   jax.ShapeDtypeStruct((b, hkv, s // qb, V_ROWS, qb), MM),
                   jax.ShapeDtypeStruct((b, hkv, s // qb, V_ROWS, qb), MM)],
        compiler_params=_cparams("parallel", "parallel"),
        name="nsa_prep",
    )(proj, proj, proj, proj, proj, proj, *tabs)


def _cmp_kernel(xk_ref, xv_ref, pos_ref, w1_ref, w2_ref, cos_ref, up_ref, dn_ref, kc_ref, vct_ref):
    nh = kc_ref.shape[2]
    hid_w = NSA_KV_HEADS * CMP_HIDDEN
    for kind, x_ref in enumerate((xk_ref, xv_ref)):
        lo = jnp.zeros((nh, hid_w), F32)
        hi = jnp.zeros((nh, hid_w), F32)
        posb = jnp.zeros((8, hid_w), F32)
        for j in range(CMP_STRIDE):
            xj = x_ref[0, pl.ds(j, nh, stride=CMP_STRIDE), :].astype(MM)
            lo = lo + _dot(xj, w1_ref[kind, j])
            hi = hi + _dot(xj, w1_ref[kind, CMP_STRIDE + j])
        for j in range(CMP_LEN):
            posb = posb + _dot(pos_ref[kind, j].astype(MM), w1_ref[kind, j])
        hid = jax.nn.gelu(lo + pltpu.roll(hi, nh - 1, 0) + posb[0:1, :])
        for h in range(NSA_KV_HEADS):
            out = _dot(hid[:, h * CMP_HIDDEN:(h + 1) * CMP_HIDDEN].astype(MM), w2_ref[kind])
            if kind == 0:
                kc_ref[0, h] = _rope_lanes(out, cos_ref[...], up_ref[...], dn_ref[...])
            else:
                vct_ref[0, h] = out.T.astype(MM)


def _nsa_compress(proj, pos, w1, w2, tabs):
    b, s, _ = proj.shape
    nh = s // CMP_STRIDE
    hkv = NSA_KV_HEADS
    kv_blk = sum(IN_SIZES[:4]) // LANE
    once = pl.Buffered(1)
    c2 = lambda i: (0, 0)
    c3 = lambda i: (0, 0, 0)
    c4 = lambda i: (0, 0, 0, 0)
    return pl.pallas_call(
        _cmp_kernel,
        grid=(b,),
        in_specs=[pl.BlockSpec((1, s, LANE), lambda i: (i, 0, kv_blk)),
                  pl.BlockSpec((1, s, LANE), lambda i: (i, 0, kv_blk + 1)),
                  pl.BlockSpec(pos.shape, c4),
                  pl.BlockSpec(w1.shape, c4, pipeline_mode=once),
                  pl.BlockSpec(w2.shape, c3),
                  pl.BlockSpec((nh, LANE), c2), pl.BlockSpec((nh, LANE), c2), pl.BlockSpec((nh, LANE), c2)],
        out_specs=[pl.BlockSpec((1, hkv, nh, LANE), lambda i: (i, 0, 0, 0)),
                   pl.BlockSpec((1, hkv, LANE, nh), lambda i: (i, 0, 0, 0))],
        out_shape=[jax.ShapeDtypeStruct((b, hkv, nh, LANE), F32),
                   jax.ShapeDtypeStruct((b, hkv, LANE, nh), MM)],
        compiler_params=_cparams("parallel"),
        name="nsa_compress",
    )(proj, proj, pos, w1, w2, *tabs)


def _softmax_update(s, vt, m, acc):
    m_new = jnp.maximum(m, jnp.max(s, axis=0, keepdims=True))
    alpha = jnp.exp2(m - m_new)
    p = jnp.exp2(s - m_new)
    acc = alpha * acc + _dot(vt, p.astype(MM))
    return m_new, acc


def _normalise(acc):
    return acc[:NSA_HD, :] * (1.0 / acc[NSA_HD:NSA_HD + 1, :])


def _nsa_attn_kernel(q_ref, gt_ref, kc_ref, vc_ref, ka_ref, vs_ref, vd_ref, kw_ref, vw_ref, o_ref,
                     ps_ref, sa_ref, sb_ref):
    blk = pl.program_id(2)
    qb = Q_BLOCK
    cols = NSA_GQA * qb
    ncmp_pad = kc_ref.shape[2]
    nsel = LANE
    qf = q_ref[0, 0, 0]
    qm = qf.astype(MM)
    t = blk * qb + (lax.broadcasted_iota(jnp.int32, (1, cols), 1) & (qb - 1))
    lower = lax.broadcasted_iota(jnp.int32, (qb, 1), 0) <= (t & (qb - 1))

    kb = SEL_CHUNK

    s_c = _dot(kc_ref[0, 0], qf)
    n_id = lax.broadcasted_iota(jnp.int32, (ncmp_pad, 1), 0)
    s_c = jnp.where(n_id * CMP_STRIDE + (CMP_LEN - 1) <= t, s_c, NEG)
    e_c = jnp.exp2(s_c - jnp.max(s_c, axis=0, keepdims=True))
    den = jnp.sum(e_c, axis=0, keepdims=True)
    p_c = e_c * jnp.where(t >= CMP_LEN - 1, 1.0 / den, 0.0)
    o_c = _dot(vc_ref[0, 0], p_c.astype(MM))[:NSA_HD, :]

    r = SLC_BLOCK // CMP_STRIDE
    nrow = ncmp_pad // r
    ps_ref[...] = p_c[:, 0:qb] + p_c[:, qb:2 * qb] + p_c[:, 2 * qb:3 * qb] + p_c[:, 3 * qb:4 * qb]
    imp = ps_ref[pl.ds(0, nrow, stride=r), :]
    for k in range(1, r - 1):
        imp = imp + ps_ref[pl.ds(k, nrow, stride=r), :]
    edge = 0.5 * ps_ref[pl.ds(r - 1, nrow, stride=r), :]
    first = lax.broadcasted_iota(jnp.int32, (nrow, qb), 0) == 0
    imp = imp + edge + jnp.where(first, 0.0, pltpu.roll(edge, 1, 0))
    if nrow < nsel:
        imp = jnp.concatenate([imp, jnp.zeros((nsel - nrow, qb), F32)], axis=0)

    cur = (blk * qb + lax.broadcasted_iota(jnp.int32, (1, qb), 1)) // SLC_BLOCK
    jid = lax.broadcasted_iota(jnp.int32, (nsel, qb), 0)
    valid = jid <= cur
    forced = (jid == 0) | (jid == cur) | (jid == cur - 1)
    score0 = jnp.where(valid, jnp.where(forced, FORCE_SCORE, imp), -1.0)
    score = jnp.where(forced & valid, -3e38, score0)
    for _ in range(SLC_TOPK - 3):
        mx = jnp.max(score, axis=0, keepdims=True)
        idx = jnp.min(jnp.where(score == mx, jid, nsel), axis=0, keepdims=True)
        score = jnp.where(jid == idx, -3e38, score)
    own = 2 * blk * (qb // (2 * SLC_BLOCK))
    bias = jnp.where((score < -1e38) & (score0 >= 0.0) & (jid < own), 0.0, NEG).astype(MM)
    s_w = []
    v_w = []
    for back in range(WIN // qb, -1, -1):
        c = blk - back
        cc = jnp.maximum(c, 0)
        off = pl.multiple_of(cc * qb, qb)
        s = _dot(kw_ref[0, 0, pl.ds(off, qb), :], qm)
        if back == WIN // qb:
            s = jnp.where(lower | (c < 0), NEG, s)
        elif back == 0:
            s = jnp.where(lower, s, NEG)
        else:
            s = jnp.where(c < 0, NEG, s)
        s_w.append(s)
        v_w.append(vw_ref[0, 0, cc])
    s_w = jnp.concatenate(s_w, axis=0)
    p_w = jnp.exp2(s_w - jnp.max(s_w, axis=0, keepdims=True))
    o_w = _normalise(_dot(jnp.concatenate(v_w, axis=1), p_w.astype(MM)))

    s_d = _dot(ka_ref[0, 0, pl.ds(pl.multiple_of(blk * qb, qb), qb), LANE:2 * LANE], qm)
    init = _softmax_update(jnp.where(lower, s_d, NEG), vd_ref[0, 0, blk],
                           jnp.full((1, cols), NEG, F32), jnp.zeros((V_ROWS, cols), F32))
    q_aug = jnp.concatenate([jnp.concatenate([bias] * NSA_GQA, axis=1), qm], axis=0)

    nchunk = ka_ref.shape[2] // kb
    diag = blk * qb

    def scores_to(s_ref, j):
        off = pl.multiple_of(jnp.minimum(j, nchunk - 1) * kb, kb)
        s_ref[...] = _dot(ka_ref[0, 0, pl.ds(off, kb), :], q_aug)

    def update_from(s_ref, j, m, acc):
        return _softmax_update(s_ref[...], vs_ref[0, 0, jnp.minimum(j, nchunk - 1)], m, acc)

    def pair_body(i, carry):
        scores_to(sb_ref, 2 * i + 1)
        carry = update_from(sa_ref, 2 * i, *carry)
        scores_to(sa_ref, 2 * i + 2)
        return update_from(sb_ref, 2 * i + 1, *carry)

    nneed = (diag + kb - 1) // kb
    scores_to(sa_ref, 0)
    o_s = _normalise(lax.fori_loop(0, (nneed + 1) // 2, pair_body, init)[1])

    g = jax.nn.sigmoid(gt_ref[0, 0, 0])
    o = g[0:1, :] * o_c + g[1:2, :] * o_s + g[2:3, :] * o_w
    for gp in range(NSA_GQA // 2):
        pair = jnp.concatenate([o[:, 2 * gp * qb:(2 * gp + 1) * qb],
                                o[:, (2 * gp + 1) * qb:(2 * gp + 2) * qb]], axis=0)
        o_ref[0, :, gp * LANE:(gp + 1) * LANE] = pair.T


def _nsa_attn(qt, gt, kc, vct, ka, vst, vsd, kw, vwt):
    b, hkv, nqb, _, cols = qt.shape
    s = ka.shape[2]
    ncmp_pad = kc.shape[2]
    assert CMP_LEN == 2 * CMP_STRIDE and SLC_BLOCK % CMP_STRIDE == 0 and s // SLC_BLOCK <= LANE
    assert (s // SEL_CHUNK) % 2 == 0
    per_head = lambda i, h, j: (i, h, 0, 0)
    per_head5 = lambda i, h, j: (i, h, 0, 0, 0)
    per_blk = lambda i, h, j: (i, h, j, 0, 0)
    return pl.pallas_call(
        _nsa_attn_kernel,
        grid=(b, hkv, nqb),
        in_specs=[pl.BlockSpec((1, 1, 1, LANE, cols), per_blk),
                  pl.BlockSpec((1, 1, 1, 8, cols), per_blk),
                  pl.BlockSpec((1, 1, ncmp_pad, LANE), per_head),
                  pl.BlockSpec((1, 1, LANE, ncmp_pad), per_head),
                  pl.BlockSpec((1, 1, s, 2 * LANE), per_head),
                  pl.BlockSpec((1, 1, s // SEL_CHUNK, V_ROWS, SEL_CHUNK), per_head5),
                  pl.BlockSpec((1, 1, s // Q_BLOCK, V_ROWS, Q_BLOCK), per_head5),
                  pl.BlockSpec((1, 1, s, LANE), per_head),
                  pl.BlockSpec((1, 1, s // Q_BLOCK, V_ROWS, Q_BLOCK), per_head5)],
        out_specs=pl.BlockSpec((1, Q_BLOCK, NSA_GQA * NSA_HD), lambda i, h, j: (i, j, h)),
        out_shape=jax.ShapeDtypeStruct((b, s, hkv * NSA_GQA * NSA_HD), F32),
        scratch_shapes=[pltpu.VMEM((ncmp_pad, Q_BLOCK), F32),
                        pltpu.VMEM((SEL_CHUNK, cols), F32), pltpu.VMEM((SEL_CHUNK, cols), F32)],
        compiler_params=_cparams("parallel", "parallel", "arbitrary"),
        name="nsa_attn",
    )(qt, gt, kc, vct, ka, vst, vsd, kw, vwt)


def _nsa_weights(pos_k, pos_v, ck_w1, ck_w2, cv_w1, cv_w2):
    dh = NSA_HD
    eye = jnp.eye(NSA_KV_HEADS, dtype=F32)
    slabs = lambda w: jnp.einsum("jrc,hg->jhrgc", w.reshape(CMP_LEN, dh, CMP_HIDDEN), eye).reshape(
        CMP_LEN, NSA_KV_HEADS * dh, NSA_KV_HEADS * CMP_HIDDEN)
    w1 = jnp.stack([slabs(ck_w1), slabs(cv_w1)]).astype(MM)
    w2 = jnp.pad(jnp.stack([ck_w2, cv_w2]), ((0, 0), (0, 0), (0, LANE - dh))).astype(MM)
    pos = jnp.stack([pos_k, pos_v])
    pos = jnp.broadcast_to(jnp.concatenate([pos] * NSA_KV_HEADS, axis=-1)[:, :, None, :],
                           (2, CMP_LEN, 8, NSA_KV_HEADS * dh))
    return pos, w1, w2


def _nsa(proj, weights, tabs):
    tok_tabs, cmp_tabs = tabs
    qt, gt, ka, kw, vst, vsd, vwt = _nsa_prep(proj, tok_tabs)
    kc, vct = _nsa_compress(proj, *weights, cmp_tabs)
    return _nsa_attn(qt, gt, kc, vct, ka, vst, vsd, kw, vwt)


def _nsa_tables(s):
    nhalf = s // CMP_STRIDE
    tok_tabs = _nsa_rope_tables(jnp.arange(s))
    cmp_tabs = _nsa_rope_tables(jnp.arange(nhalf) * CMP_STRIDE + CMP_LEN - 1)
    return tok_tabs, cmp_tabs


def _out_kernel(x_ref, g_ref, yl_ref, ys_ref, yr_ref, yn_ref, wgm_ref, wb_ref, wo_ref,
                fg_ref, o_ref, *, final):
    x = x_ref[0]
    h = _rms(x, g_ref[...]).astype(MM)
    w = BRANCH_W
    d = D_MODEL
    m0 = N_BRANCH * w
    merged = jnp.zeros(x.shape, F32)
    for n, y_ref in enumerate((yl_ref, ys_ref, yr_ref, yn_ref)):
        gate = jax.nn.silu(_dot(h, wgm_ref[0, :, n * w:(n + 1) * w]))
        branch = _dot((y_ref[0] * gate).astype(MM), wb_ref[n])
        merged = merged + jax.nn.sigmoid(_dot(h, wgm_ref[0, :, m0 + n * d:m0 + (n + 1) * d])) * branch
    out = x + _dot(merged.astype(MM), wo_ref[...])
    if final:
        out = _rms(out, fg_ref[...])
    o_ref[0] = out


def _out(x, g, ys, wgm, l, wb, wo, fg, final):
    b, s, d = x.shape
    w = BRANCH_W
    tm = TM_PROJ
    row = lambda i, j: (i, j, 0)
    c2 = lambda i, j: (0, 0)
    once = pl.Buffered(1)
    return pl.pallas_call(
        functools.partial(_out_kernel, final=final),
        grid=(b, s // tm),
        in_specs=[pl.BlockSpec((1, tm, d), row),
                  pl.BlockSpec((1, d), c2)]
                 + [pl.BlockSpec((1, tm, w), row)] * N_BRANCH
                 + [pl.BlockSpec((1, d, N_BRANCH * (w + d)), lambda i, j: (l, 0, 0), pipeline_mode=once),
                    pl.BlockSpec((N_BRANCH, w, d), lambda i, j: (0, 0, 0), pipeline_mode=once),
                    pl.BlockSpec((d, d), c2, pipeline_mode=once),
                    pl.BlockSpec((1, d), c2)],
        out_specs=pl.BlockSpec((1, tm, d), row),
        out_shape=jax.ShapeDtypeStruct((b, s, d), F32),
        compiler_params=_cparams("parallel", "arbitrary"),
        name="gate_merge_out",
    )(x, g, *ys, wgm, wb, wo, fg)


def _split_w_in(w_in):
    pad = jnp.zeros(w_in.shape[:2] + (PROJ_W - GATE_OFF,), w_in.dtype)
    w_mix = jnp.concatenate([w_in[:, :, :GATE_OFF], pad], axis=2).astype(MM)
    return w_mix, w_in[:, :, GATE_OFF:].astype(MM)


def _layer(x, p, l, w_mix, w_gm, ret_tabs, nsa_tabs, final_g):
    g = p["norm_g"][l].reshape(1, D_MODEL)
    proj = _inproj(x, g, w_mix, l)

    wab = jnp.concatenate([_block_diag(p["lru_wa"][l]), _block_diag(p["lru_wx"][l])], axis=1).astype(MM)
    bab = jnp.concatenate([p["lru_ba"][l], p["lru_bx"][l]]).reshape(1, 2 * BRANCH_W)
    y_lru = _lru(proj, p["lru_conv_w"][l], p["lru_conv_b"][l].reshape(1, -1), wab, bab,
                 p["lru_lambda"][l].reshape(1, -1))

    wb, apr, api, wc = _s5_tables(p["s5_lambda_re"][l], p["s5_lambda_im"][l], p["s5_b_re"][l],
                                  p["s5_b_im"][l], p["s5_c_re"][l], p["s5_c_im"][l], p["s5_log_dt"][l])
    y_s5 = _s5(proj, wb, apr, api, wc, p["s5_d"][l].reshape(1, -1), p["s5_w_glu"][l].astype(MM),
               p["s5_b_glu"][l].reshape(1, -1))

    y_ret = _retention(proj, ret_tabs)

    nsa_w = _nsa_weights(p["nsa_pos_k"][l], p["nsa_pos_v"][l], p["nsa_ck_w1"][l], p["nsa_ck_w2"][l],
                         p["nsa_cv_w1"][l], p["nsa_cv_w2"][l])
    y_nsa = _nsa(proj, nsa_w, nsa_tabs)

    final = l == DEPTH - 1
    return _out(x, g, (y_lru, y_s5, y_ret, y_nsa), w_gm, l, p["w_branch"][l].astype(MM),
                p["w_out"][l].astype(MM), final_g.reshape(1, D_MODEL), final)


def kernel(x, norm_g, w_in, lru_conv_w, lru_conv_b, lru_wa, lru_ba, lru_wx, lru_bx, lru_lambda,
           s5_lambda_re, s5_lambda_im, s5_b_re, s5_b_im, s5_c_re, s5_c_im, s5_log_dt, s5_d,
           s5_w_glu, s5_b_glu, nsa_pos_k, nsa_pos_v, nsa_ck_w1, nsa_ck_w2, nsa_cv_w1, nsa_cv_w2,
           w_branch, w_out, final_norm_g):
    p = dict(norm_g=norm_g, w_in=w_in, lru_conv_w=lru_conv_w, lru_conv_b=lru_conv_b, lru_wa=lru_wa,
             lru_ba=lru_ba, lru_wx=lru_wx, lru_bx=lru_bx, lru_lambda=lru_lambda,
             s5_lambda_re=s5_lambda_re, s5_lambda_im=s5_lambda_im, s5_b_re=s5_b_re, s5_b_im=s5_b_im,
             s5_c_re=s5_c_re, s5_c_im=s5_c_im, s5_log_dt=s5_log_dt, s5_d=s5_d, s5_w_glu=s5_w_glu,
             s5_b_glu=s5_b_glu, nsa_pos_k=nsa_pos_k, nsa_pos_v=nsa_pos_v, nsa_ck_w1=nsa_ck_w1,
             nsa_ck_w2=nsa_ck_w2, nsa_cv_w1=nsa_cv_w1, nsa_cv_w2=nsa_cv_w2, w_branch=w_branch,
             w_out=w_out)
    s = x.shape[1]
    ret_tabs = _ret_tables(s)
    nsa_tabs = _nsa_tables(s)
    w_mix, w_gm = _split_w_in(w_in)
    for l in range(DEPTH):
        x = _layer(x, p, l, w_mix, w_gm, ret_tabs, nsa_tabs, final_norm_g)
    return x
```

```python
import functools
import math

import jax
import jax.numpy as jnp
from jax import lax
from jax.experimental import pallas as pl
from jax.experimental.pallas import tpu as pltpu

F32 = jnp.float32
MM = jnp.bfloat16

D_MODEL = 1024
DEPTH = 4
N_BRANCH = 4
BRANCH_W = 512
NORM_EPS = 1e-6
LRU_HEADS = 8
LRU_HD = BRANCH_W // LRU_HEADS
CONV_W = 4
LRU_C = 8.0
S5_GROUP = 16
S5_GROUPS = BRANCH_W // S5_GROUP
S5_STATE = 64
S5_N = S5_GROUPS * S5_STATE
RET_HEADS = 4
RET_HD = BRANCH_W // RET_HEADS
RET_CHUNK = 128
RET_ROPE_BASE = 10000.0
NSA_HEADS = 8
NSA_KV_HEADS = 2
NSA_HD = BRANCH_W // NSA_HEADS
NSA_GQA = NSA_HEADS // NSA_KV_HEADS
CMP_LEN = 32
CMP_STRIDE = 16
CMP_HIDDEN = 256
SLC_BLOCK = 64
SLC_TOPK = 16
WIN = 512
Q_BLOCK = 128
ROPE_THETA = 500000.0
ROPE_DIM = NSA_HD // 4
FORCE_SCORE = 1e4
NEG = -1e30
NSA_KV_W = 6 * NSA_KV_HEADS * NSA_HD
IN_SIZES = (BRANCH_W, BRANCH_W, 3 * BRANCH_W, NSA_HEADS * NSA_HD, NSA_KV_W, 3 * NSA_HEADS,
            N_BRANCH * BRANCH_W, N_BRANCH * D_MODEL)
LANE = 128
MIX_W = sum(IN_SIZES[:5])
BG_OFF = MIX_W
GATE_OFF = BG_OFF + IN_SIZES[5]
MERGE_OFF = GATE_OFF + IN_SIZES[6]
PROJ_W = MIX_W + LANE
VMEM_LIMIT = 56 * 1024 * 1024

TM_PROJ = 256
T_LRU = 256
T_S5 = 256
T_RET = 512
S5_TILE_GROUP = 4
SEL_CHUNK = 512
NSA_T = SEL_CHUNK
V_ROWS = NSA_HD + 16


def _cparams(*sem):
    return pltpu.CompilerParams(dimension_semantics=sem, vmem_limit_bytes=VMEM_LIMIT)


def _rms(x, g):
    ms = jnp.mean(x * x, axis=-1, keepdims=True)
    return (x * lax.rsqrt(ms + NORM_EPS)) * g


def _dot(a, b):
    return jnp.dot(a, b, preferred_element_type=F32)


def _dot_nt(a, b):
    return lax.dot_general(a, b, (((1,), (1,)), ((), ())), preferred_element_type=F32)


def _dot_tn(a, b):
    return lax.dot_general(a, b, (((0,), (0,)), ((), ())), preferred_element_type=F32)


def _inproj_kernel(x_ref, g_ref, w_ref, o_ref):
    h = _rms(x_ref[0], g_ref[...])
    o_ref[0] = _dot(h.astype(MM), w_ref[0])


def _inproj(x, g, w, l):
    b, s, d = x.shape
    n = w.shape[2]
    return pl.pallas_call(
        _inproj_kernel,
        grid=(b, s // TM_PROJ),
        in_specs=[pl.BlockSpec((1, TM_PROJ, d), lambda i, j: (i, j, 0)),
                  pl.BlockSpec((1, d), lambda i, j: (0, 0)),
                  pl.BlockSpec((1, d, n), lambda i, j: (l, 0, 0))],
        out_specs=pl.BlockSpec((1, TM_PROJ, n), lambda i, j: (i, j, 0)),
        out_shape=jax.ShapeDtypeStruct((b, s, n), F32),
        compiler_params=_cparams("parallel", "arbitrary"),
        name="inproj",
    )(x, g, w)


def _lru_kernel(u_ref, cw_ref, cb_ref, wab_ref, bab_ref, lam_ref, o_ref, ubuf, hcar, la_ref, lb_ref):
    t = T_LRU
    w = BRANCH_W

    @pl.when(pl.program_id(1) == 0)
    def _():
        ubuf[0:8, :] = jnp.zeros((8, w), F32)
        hcar[...] = jnp.zeros((1, w), F32)

    ubuf[8:8 + t, :] = u_ref[0]
    xc = cb_ref[...] + cw_ref[0:1, :] * ubuf[5:5 + t, :]
    for k in range(1, CONV_W):
        xc = xc + cw_ref[k:k + 1, :] * ubuf[5 + k:5 + k + t, :]
    ubuf[0:8, :] = ubuf[t:t + 8, :]

    ri = _dot(xc.astype(MM), wab_ref[...]) + bab_ref[...]
    r = jax.nn.sigmoid(ri[:, :w])
    gi = jax.nn.sigmoid(ri[:, w:])
    z = -lam_ref[...]
    softplus = jnp.maximum(z, 0.0) + jnp.log1p(jnp.exp(-jnp.abs(z)))
    a = jnp.exp(-LRU_C * r * softplus)
    bt = jnp.sqrt(1.0 - a * a) * gi * xc

    sub = 8
    grp = t // sub
    i8 = lax.broadcasted_iota(jnp.int32, (sub, LANE), 0)
    rowg = lax.broadcasted_iota(jnp.int32, (grp, LANE), 0)
    last = pl.ds(sub - 1, grp, stride=sub)
    tiles = range(w // LANE)
    loc = []
    for cb in tiles:
        lanes = slice(cb * LANE, (cb + 1) * LANE)
        a3 = a[:, lanes].reshape(grp, sub, LANE)
        b3 = bt[:, lanes].reshape(grp, sub, LANE)
        k = 1
        while k < sub:
            keep = (i8 >= k)[None]
            a_sh = jnp.where(keep, pltpu.roll(a3, k, 1), 1.0)
            b_sh = jnp.where(keep, pltpu.roll(b3, k, 1), 0.0)
            b3 = a3 * b_sh + b3
            a3 = a3 * a_sh
            k *= 2
        loc.append((a3, b3))
        la_ref[cb] = a3.reshape(t, LANE)
        lb_ref[cb] = b3.reshape(t, LANE)
    ends = [(la_ref[cb, last, :], lb_ref[cb, last, :]) for cb in tiles]
    k = 1
    while k < grp:
        keep = rowg >= k
        ends = [(ae * jnp.where(keep, pltpu.roll(ae, k, 0), 1.0),
                 ae * jnp.where(keep, pltpu.roll(be, k, 0), 0.0) + be) for ae, be in ends]
        k *= 2
    hc = hcar[...]
    full = jnp.concatenate([be + ae * hc[:, cb * LANE:(cb + 1) * LANE] for cb, (ae, be) in zip(tiles, ends)], axis=1)
    hcar[...] = full[grp - 1:grp, :]
    row_w = lax.broadcasted_iota(jnp.int32, (grp, w), 0)
    e = jnp.where(row_w >= 1, pltpu.roll(full, 1, 0), hc)
    e_hi = e.astype(MM)
    r1 = e - e_hi.astype(F32)
    e_mid = r1.astype(MM)
    e_lo = (r1 - e_mid.astype(F32)).astype(MM)
    rep = lax.broadcasted_iota(jnp.int32, (t, grp), 0) // sub == lax.broadcasted_iota(jnp.int32, (t, grp), 1)
    rep = jnp.where(rep, 1.0, 0.0).astype(MM)
    z_in = _dot(jnp.concatenate([rep, rep, rep], axis=1), jnp.concatenate([e_hi, e_mid, e_lo], axis=0))
    for cb, (a3, b3) in zip(tiles, loc):
        lanes = slice(cb * LANE, (cb + 1) * LANE)
        o_ref[0, :, lanes] = b3.reshape(t, LANE) + a3.reshape(t, LANE) * z_in[:, lanes]


def _lru(proj, cw, cb, wab, bab, lam):
    b, s, _ = proj.shape
    w = BRANCH_W
    return pl.pallas_call(
        _lru_kernel,
        grid=(b, s // T_LRU),
        in_specs=[pl.BlockSpec((1, T_LRU, w), lambda i, j: (i, j, 0)),
                  pl.BlockSpec((CONV_W, w), lambda i, j: (0, 0)),
                  pl.BlockSpec((1, w), lambda i, j: (0, 0)),
                  pl.BlockSpec((w, 2 * w), lambda i, j: (0, 0)),
                  pl.BlockSpec((1, 2 * w), lambda i, j: (0, 0)),
                  pl.BlockSpec((1, w), lambda i, j: (0, 0))],
        out_specs=pl.BlockSpec((1, T_LRU, w), lambda i, j: (i, j, 0)),
        out_shape=jax.ShapeDtypeStruct((b, s, w), F32),
        scratch_shapes=[pltpu.VMEM((T_LRU + 8, w), F32), pltpu.VMEM((1, w), F32),
                        pltpu.VMEM((w // LANE, T_LRU, LANE), F32), pltpu.VMEM((w // LANE, T_LRU, LANE), F32)],
        compiler_params=_cparams("parallel", "arbitrary"),
        name="rg_lru",
    )(proj, cw, cb, wab, bab, lam)


def _block_diag(blocks):
    n, r, c = blocks.shape
    eye = jnp.eye(n, dtype=blocks.dtype)
    return jnp.einsum("nrc,nm->nrmc", blocks, eye).reshape(n * r, n * c)


def _cmul_add(xr, xi, ar, ai, br, bi):
    return xr + (ar * br - ai * bi), xi + (ar * bi + ai * br)


def _s5_kernel(u_ref, wb_ref, apr_ref, api_ref, wc_ref, d_ref, wg_ref, bg_ref, o_ref,
               car, lr_ref, li_ref, y_ref):
    t = T_S5
    n = S5_N
    w = BRANCH_W
    sub = 8
    grp = t // sub
    mxu = 256

    @pl.when(pl.program_id(1) == 0)
    def _():
        car[...] = jnp.zeros((2, n), F32)

    u = u_ref[0]
    ub = u.astype(MM)
    y_ref[...] = d_ref[...] * u
    first = lax.broadcasted_iota(jnp.int32, (t, LANE), 0) == 0
    i8 = lax.broadcasted_iota(jnp.int32, (sub, LANE), 0)
    rowg = lax.broadcasted_iota(jnp.int32, (grp, LANE), 0)
    rep = lax.broadcasted_iota(jnp.int32, (t, grp), 0) // sub == lax.broadcasted_iota(jnp.int32, (t, grp), 1)
    rep = jnp.where(rep, 1.0, 0.0).astype(MM)
    rep3 = jnp.concatenate([rep, rep, rep], axis=1)
    last = pl.ds(sub - 1, grp, stride=sub)

    ntile = n // LANE
    for c0 in range(0, ntile, S5_TILE_GROUP):
        tiles = range(c0, c0 + S5_TILE_GROUP)
        lanes = [slice(cb * LANE, (cb + 1) * LANE) for cb in tiles]
        ios = [slice(io * mxu, (io + 1) * mxu) for io in ((cb * LANE // S5_STATE * S5_GROUP) // mxu for cb in tiles)]
        aprs = [apr_ref[:, ln] for ln in lanes]
        apis = [api_ref[:, ln] for ln in lanes]
        xs = [_dot(ub[:, io], wb_ref[cb]) for cb, io in zip(tiles, ios)]
        loc = []
        for x, apr, api, ln in zip(xs, aprs, apis, lanes):
            cr, ci = _cmul_add(0.0, 0.0, apr[0:1], api[0:1], car[0:1, ln], car[1:2, ln])
            xr = (x[:, :LANE] + jnp.where(first, cr, 0.0)).reshape(grp, sub, LANE)
            xi = (x[:, LANE:] + jnp.where(first, ci, 0.0)).reshape(grp, sub, LANE)
            k = 1
            while k < sub:
                ar = jnp.where(i8 >= k, apr[k - 1:k], 0.0)[None]
                ai = jnp.where(i8 >= k, api[k - 1:k], 0.0)[None]
                xr, xi = _cmul_add(xr, xi, ar, ai, pltpu.roll(xr, k, 1), pltpu.roll(xi, k, 1))
                k *= 2
            loc.append((xr, xi))
        for j, (xr, xi) in enumerate(loc):
            lr_ref[j] = xr.reshape(t, LANE)
            li_ref[j] = xi.reshape(t, LANE)
        ends = [(lr_ref[j, last, :], li_ref[j, last, :]) for j in range(len(loc))]
        k = 1
        step = sub - 1
        while k < grp:
            keep = rowg >= k
            nxt = []
            for (fr, fi), apr, api in zip(ends, aprs, apis):
                sr = jnp.where(keep, pltpu.roll(fr, k, 0), 0.0)
                si = jnp.where(keep, pltpu.roll(fi, k, 0), 0.0)
                nxt.append(_cmul_add(fr, fi, apr[step:step + 1], api[step:step + 1], sr, si))
            ends = nxt
            k *= 2
            step += 1
        enter = rowg >= 1
        zs = []
        for (fr, fi), ln in zip(ends, lanes):
            car[0:1, ln] = fr[grp - 1:grp, :]
            car[1:2, ln] = fi[grp - 1:grp, :]
            e = jnp.concatenate([jnp.where(enter, pltpu.roll(fr, 1, 0), 0.0),
                                 jnp.where(enter, pltpu.roll(fi, 1, 0), 0.0)], axis=1)
            e_hi = e.astype(MM)
            r1 = e - e_hi.astype(F32)
            e_mid = r1.astype(MM)
            e_lo = (r1 - e_mid.astype(F32)).astype(MM)
            zs.append(_dot(rep3, jnp.concatenate([e_hi, e_mid, e_lo], axis=0)))
        for cb, io, (xr, xi), z, apr, api in zip(tiles, ios, loc, zs, aprs, apis):
            hr, hi = _cmul_add(xr, xi, apr[0:sub][None], api[0:sub][None],
                               z[:, :LANE].reshape(grp, sub, LANE), z[:, LANE:].reshape(grp, sub, LANE))
            h = jnp.concatenate([hr.reshape(t, LANE), hi.reshape(t, LANE)], axis=1).astype(MM)
            y_ref[:, io] += _dot(h, wc_ref[cb])

    z = jax.nn.gelu(y_ref[...])
    g = _dot(z.astype(MM), wg_ref[...]) + bg_ref[...]
    o_ref[0] = g[:, :w] * jax.nn.sigmoid(g[:, w:])


def _s5_tables(lam_re, lam_im, b_re, b_im, c_re, c_im, log_dt):
    dt = jnp.exp(log_dt)[:, None]
    lr = jnp.minimum(lam_re, -1e-4)
    li = lam_im
    mag = jnp.exp(lr * dt)
    ab_re = mag * jnp.cos(li * dt)
    ab_im = mag * jnp.sin(li * dt)
    den = lr * lr + li * li
    nr = ab_re - 1.0
    ni = ab_im
    f_re = (nr * lr + ni * li) / den
    f_im = (ni * lr - nr * li) / den
    bb_re = f_re[..., None] * b_re - f_im[..., None] * b_im
    bb_im = f_re[..., None] * b_im + f_im[..., None] * b_re
    wb_re = _block_diag(jnp.swapaxes(bb_re, 1, 2))
    wb_im = _block_diag(jnp.swapaxes(bb_im, 1, 2))
    wc_re = _block_diag(jnp.swapaxes(c_re, 1, 2))
    wc_im = _block_diag(jnp.swapaxes(-c_im, 1, 2))
    wb, wc = [], []
    for cb in range(S5_N // LANE):
        lanes = slice(cb * LANE, (cb + 1) * LANE)
        io = (cb * LANE // S5_STATE * S5_GROUP) // 256
        ios = slice(io * 256, (io + 1) * 256)
        wb.append(jnp.concatenate([wb_re[ios, lanes], wb_im[ios, lanes]], axis=1))
        wc.append(jnp.concatenate([wc_re[lanes, ios], wc_im[lanes, ios]], axis=0))
    wb = jnp.stack(wb)
    wc = jnp.stack(wc)
    a_re = ab_re.reshape(1, S5_N)
    a_im = ab_im.reshape(1, S5_N)
    pr, pi = [a_re], [a_im]
    for _ in range(7):
        pr.append(pr[-1] * a_re - pi[-1] * a_im)
        pi.append(pr[-2] * a_im + pi[-1] * a_re)
    for _ in range(int(math.log2(T_S5 // 8)) - 1):
        r0, i0 = pr[-1], pi[-1]
        pr.append(r0 * r0 - i0 * i0)
        pi.append(2.0 * r0 * i0)
    pad = [jnp.zeros_like(a_re)] * (16 - len(pr))
    return wb.astype(MM), jnp.concatenate(pr + pad, 0), jnp.concatenate(pi + pad, 0), wc.astype(MM)


def _s5(proj, wb, apr, api, wc, d, wg, bg):
    b, s, _ = proj.shape
    w = BRANCH_W
    n = S5_N
    c2 = lambda i, j: (0, 0)
    c3 = lambda i, j: (0, 0, 0)
    return pl.pallas_call(
        _s5_kernel,
        grid=(b, s // T_S5),
        in_specs=[pl.BlockSpec((1, T_S5, w), lambda i, j: (i, j, 1)),
                  pl.BlockSpec(wb.shape, c3),
                  pl.BlockSpec(apr.shape, c2),
                  pl.BlockSpec(api.shape, c2),
                  pl.BlockSpec(wc.shape, c3),
                  pl.BlockSpec((1, w), c2),
                  pl.BlockSpec((w, 2 * w), c2),
                  pl.BlockSpec((1, 2 * w), c2)],
        out_specs=pl.BlockSpec((1, T_S5, w), lambda i, j: (i, j, 0)),
        out_shape=jax.ShapeDtypeStruct((b, s, w), F32),
        scratch_shapes=[pltpu.VMEM((2, n), F32), pltpu.VMEM((S5_TILE_GROUP, T_S5, LANE), F32),
                        pltpu.VMEM((S5_TILE_GROUP, T_S5, LANE), F32), pltpu.VMEM((T_S5, w), F32)],
        compiler_params=_cparams("parallel", "arbitrary"),
        name="s5",
    )(proj, wb, apr, api, wc, d, wg, bg)


def _ret_kernel(q_ref, k_ref, v_ref, cos_ref, sin_ref, dm_ref, xi_ref, zt_ref, cd_ref, o_ref, rst):
    c = RET_CHUNK
    dh = RET_HD

    @pl.when(pl.program_id(1) == 0)
    def _():
        rst[...] = jnp.zeros((RET_HEADS, dh, dh), F32)

    for ci in range(T_RET // c):
        rows = slice(ci * c, (ci + 1) * c)
        cos = cos_ref[rows, :]
        sin = sin_ref[rows, :]
        heads = range(RET_HEADS)
        cols = [slice(h * dh, (h + 1) * dh) for h in heads]
        qs, ks, vbs = [], [], []
        for h in heads:
            q = q_ref[0, rows, cols[h]]
            k = k_ref[0, rows, cols[h]]
            qs.append(q * cos + pltpu.roll(q, dh // 2, 1) * sin)
            ks.append((k * cos + pltpu.roll(k, dh // 2, 1) * sin) * (dh ** -0.5))
            vbs.append(v_ref[0, rows, cols[h]].astype(MM))
        scores = [_dot_nt(qs[h].astype(MM), ks[h].astype(MM)) * dm_ref[h] for h in heads]
        r_prev = [rst[h] for h in heads]
        cross = [_dot((qs[h] * xi_ref[h]).astype(MM), r_prev[h].astype(MM)) for h in heads]
        kv = [_dot_tn((ks[h] * zt_ref[h]).astype(MM), vbs[h]) for h in heads]
        inner = [_dot(scores[h].astype(MM), vbs[h]) for h in heads]
        for h in heads:
            rst[h] = cd_ref[h] * r_prev[h] + kv[h]
            o = inner[h] + cross[h]
            mu = jnp.mean(o, axis=-1, keepdims=True)
            var = jnp.mean(jnp.square(o - mu), axis=-1, keepdims=True)
            o_ref[0, rows, cols[h]] = (o - mu) * lax.rsqrt(var + 1e-5)


def _ret_tables(s):
    dh = RET_HD
    c = RET_CHUNK
    half = dh // 2
    inv = RET_ROPE_BASE ** (-jnp.arange(half, dtype=F32) * 2.0 / dh)
    ang = jnp.arange(s).astype(F32)[:, None] * inv[None, :]
    cos = jnp.cos(ang)
    sin = jnp.sin(ang)
    cosf = jnp.concatenate([cos, cos], axis=1)
    sinf = jnp.concatenate([-sin, sin], axis=1)
    gamma = 1.0 - jnp.exp(jnp.linspace(math.log(1.0 / 32.0), math.log(1.0 / 512.0), RET_HEADS, dtype=F32))
    log_g = jnp.log(gamma)
    n = jnp.arange(c, dtype=F32)
    diff = n[:, None] - n[None, :]
    dmask = jnp.where(diff[None] >= 0, jnp.exp(jnp.maximum(diff, 0.0)[None] * log_g[:, None, None]), 0.0)
    xi = jnp.exp((n[None, :] + 1.0) * log_g[:, None])
    zeta = jnp.exp((c - 1.0 - n)[None, :] * log_g[:, None])
    cdec = jnp.exp(c * log_g)
    xi_b = jnp.broadcast_to(xi[:, :, None], (RET_HEADS, c, dh))
    zt_b = jnp.broadcast_to(zeta[:, :, None], (RET_HEADS, c, dh))
    cd_b = jnp.broadcast_to(cdec[:, None, None], (RET_HEADS, dh, dh))
    return cosf, sinf, dmask, xi_b, zt_b, cd_b


def _retention(proj, tabs):
    b, s, _ = proj.shape
    w = BRANCH_W
    cosf, sinf, dmask, xi_b, zt_b, cd_b = tabs
    c = RET_CHUNK
    dh = RET_HD
    full3 = lambda i, j: (0, 0, 0)
    return pl.pallas_call(
        _ret_kernel,
        grid=(b, s // T_RET),
        in_specs=[pl.BlockSpec((1, T_RET, w), lambda i, j: (i, j, 2)),
                  pl.BlockSpec((1, T_RET, w), lambda i, j: (i, j, 3)),
                  pl.BlockSpec((1, T_RET, w), lambda i, j: (i, j, 4)),
                  pl.BlockSpec((T_RET, dh), lambda i, j: (j, 0)),
                  pl.BlockSpec((T_RET, dh), lambda i, j: (j, 0)),
                  pl.BlockSpec((RET_HEADS, c, c), full3),
                  pl.BlockSpec((RET_HEADS, c, dh), full3),
                  pl.BlockSpec((RET_HEADS, c, dh), full3),
                  pl.BlockSpec((RET_HEADS, dh, dh), full3)],
        out_specs=pl.BlockSpec((1, T_RET, w), lambda i, j: (i, j, 0)),
        out_shape=jax.ShapeDtypeStruct((b, s, w), F32),
        scratch_shapes=[pltpu.VMEM((RET_HEADS, dh, dh), F32)],
        compiler_params=_cparams("parallel", "arbitrary"),
        name="retention",
    )(proj, proj, proj, cosf, sinf, dmask, xi_b, zt_b, cd_b)


def _rope_lanes(x, cos, s_up, s_dn):
    half = ROPE_DIM // 2
    n = x.shape[-1]
    return x * cos + pltpu.roll(x, half, 1) * s_up + pltpu.roll(x, n - half, 1) * s_dn


def _nsa_prep_kernel(q_ref, ks_ref, vs_ref, kw_ref, vw_ref, bg_ref, cos_ref, up_ref, dn_ref,
                     qt_ref, gt_ref, ka_ref, kwo_ref, vst_ref, vsd_ref, vwt_ref):
    t = NSA_T
    qb = Q_BLOCK
    dh = NSA_HD
    cos = cos_ref[...]
    up = up_ref[...]
    dn = dn_ref[...]
    scale = NSA_HD ** -0.5 * math.log2(math.e)
    lane = lax.broadcasted_iota(jnp.int32, (t, LANE), 1)
    low = lane < dh
    tok = pl.program_id(1) * t + lax.broadcasted_iota(jnp.int32, (t, LANE), 0)
    onehot = jnp.where(tok // SLC_BLOCK == lane, 1.0, 0.0).astype(MM)
    ks = _rope_lanes(ks_ref[0], cos, up, dn)
    kw = _rope_lanes(kw_ref[0], cos, up, dn)
    for h in range(NSA_KV_HEADS):
        ks_h = ks if h == 0 else pltpu.roll(ks, LANE - h * dh, 1)
        kw_h = kw if h == 0 else pltpu.roll(kw, LANE - h * dh, 1)
        ka_ref[0, h, :, 0:LANE] = onehot
        ka_ref[0, h, :, LANE:2 * LANE] = jnp.where(low, ks_h, 0.0).astype(MM)
        kwo_ref[0, h] = jnp.where(low, kw_h, 0.0).astype(MM)

    def ones_row(n):
        return jnp.where(lax.broadcasted_iota(jnp.int32, (V_ROWS - dh, n), 0) == 0, 1.0, 0.0).astype(MM)

    r8 = lax.broadcasted_iota(jnp.int32, (8, NSA_GQA * qb), 0)
    for i in range(t // qb):
        rows = slice(i * qb, (i + 1) * qb)
        vs_t = vs_ref[0, rows, :].T
        vw_t = vw_ref[0, rows, :].T
        bg_t = bg_ref[0, rows, :].T
        for h in range(NSA_KV_HEADS):
            vst_ref[0, h, 0, 0:dh, rows] = vs_t[h * dh:(h + 1) * dh].astype(MM)
            vsd_ref[0, h, i, 0:dh, :] = vs_t[h * dh:(h + 1) * dh].astype(MM)
            vsd_ref[0, h, i, dh:V_ROWS, :] = ones_row(qb)
            vwt_ref[0, h, i, 0:dh, :] = vw_t[h * dh:(h + 1) * dh].astype(MM)
            vwt_ref[0, h, i, dh:V_ROWS, :] = ones_row(qb)
            qt_ref[0, h, i, dh:LANE, :] = jnp.zeros((LANE - dh, NSA_GQA * qb), F32)
            gates = jnp.zeros((8, NSA_GQA * qb), F32)
            for br in range(3):
                base = br * NSA_HEADS + h * NSA_GQA
                row = jnp.concatenate([bg_t[base + g:base + g + 1] for g in range(NSA_GQA)], axis=1)
                gates = jnp.where(r8 == br, row, gates)
            gt_ref[0, h, i] = gates
        for cb in range(BRANCH_W // LANE):
            cols = slice(cb * LANE, (cb + 1) * LANE)
            x_t = (_rope_lanes(q_ref[0, rows, cols], cos[rows], up[rows], dn[rows]) * scale).T
            h = (2 * cb) // NSA_GQA
            g0 = (2 * cb) % NSA_GQA
            qt_ref[0, h, i, 0:dh, g0 * qb:(g0 + 1) * qb] = x_t[0:dh]
            qt_ref[0, h, i, 0:dh, (g0 + 1) * qb:(g0 + 2) * qb] = x_t[dh:2 * dh]
    for h in range(NSA_KV_HEADS):
        vst_ref[0, h, 0, dh:V_ROWS, :] = ones_row(t)


def _nsa_rope_tables(pos):
    half = ROPE_DIM // 2
    inv = ROPE_THETA ** (-jnp.arange(half, dtype=F32) * 2.0 / ROPE_DIM)
    ang = pos.astype(F32)[:, None] * inv[None, :]
    cos = jnp.cos(ang)
    sin = jnp.sin(ang)
    n = pos.shape[0]
    zeros = jnp.zeros((n, NSA_HD - ROPE_DIM), F32)
    zh = jnp.zeros((n, half), F32)
    cos_h = jnp.concatenate([cos, cos, jnp.ones((n, NSA_HD - ROPE_DIM), F32)], axis=1)
    up_h = jnp.concatenate([zh, sin, zeros], axis=1)
    dn_h = jnp.concatenate([-sin, zh, zeros], axis=1)
    tile2 = lambda a: jnp.concatenate([a, a], axis=1)
    return tile2(cos_h), tile2(up_h), tile2(dn_h)


def _nsa_prep(proj, tabs):
    b, s, _ = proj.shape
    t = NSA_T
    qb = Q_BLOCK
    hkv = NSA_KV_HEADS
    cols = NSA_GQA * qb
    assert t == SEL_CHUNK and s // SLC_BLOCK <= LANE
    q_blk = sum(IN_SIZES[:3]) // BRANCH_W
    kv_blk = sum(IN_SIZES[:4]) // LANE
    lane_blk = lambda n: pl.BlockSpec((1, t, LANE), lambda i, j: (i, j, n))
    tab_spec = pl.BlockSpec((t, LANE), lambda i, j: (j, 0))
    tile4 = lambda i, j: (i, 0, j, 0)
    tile5 = lambda i, j: (i, 0, j, 0, 0)
    return pl.pallas_call(
        _nsa_prep_kernel,
        grid=(b, s // t),
        in_specs=[pl.BlockSpec((1, t, BRANCH_W), lambda i, j: (i, j, q_blk)),
                  lane_blk(kv_blk + 2), lane_blk(kv_blk + 3), lane_blk(kv_blk + 4), lane_blk(kv_blk + 5),
                  lane_blk(BG_OFF // LANE), tab_spec, tab_spec, tab_spec],
        out_specs=[pl.BlockSpec((1, hkv, t // qb, LANE, cols), tile5),
                   pl.BlockSpec((1, hkv, t // qb, 8, cols), tile5),
                   pl.BlockSpec((1, hkv, t, 2 * LANE), tile4),
                   pl.BlockSpec((1, hkv, t, LANE), tile4),
                   pl.BlockSpec((1, hkv, 1, V_ROWS, t), tile5),
                   pl.BlockSpec((1, hkv, t // qb, V_ROWS, qb), tile5),
                   pl.BlockSpec((1, hkv, t // qb, V_ROWS, qb), tile5)],
        out_shape=[jax.ShapeDtypeStruct((b, hkv, s // qb, LANE, cols), F32),
                   jax.ShapeDtypeStruct((b, hkv, s // qb, 8, cols), F32),
                   jax.ShapeDtypeStruct((b, hkv, s, 2 * LANE), MM),
                   jax.ShapeDtypeStruct((b, hkv, s, LANE), MM),
                   jax.ShapeDtypeStruct((b, hkv, s // t, V_ROWS, t), MM),
                   jax.ShapeDtypeStruct((b, hkv, s // qb, V_ROWS, qb), MM),
                   jax.ShapeDtypeStruct((b, hkv, s // qb, V_ROWS, qb), MM)],
        compiler_params=_cparams("parallel", "parallel"),
        name="nsa_prep",
    )(proj, proj, proj, proj, proj, proj, *tabs)


def _cmp_kernel(xk_ref, xv_ref, pos_ref, w1_ref, w2_ref, cos_ref, up_ref, dn_ref, kc_ref, vct_ref):
    nh = kc_ref.shape[2]
    hid_w = NSA_KV_HEADS * CMP_HIDDEN
    for kind, x_ref in enumerate((xk_ref, xv_ref)):
        lo = jnp.zeros((nh, hid_w), F32)
        hi = jnp.zeros((nh, hid_w), F32)
        posb = jnp.zeros((8, hid_w), F32)
        for j in range(CMP_STRIDE):
            xj = x_ref[0, pl.ds(j, nh, stride=CMP_STRIDE), :].astype(MM)
            lo = lo + _dot(xj, w1_ref[kind, j])
            hi = hi + _dot(xj, w1_ref[kind, CMP_STRIDE + j])
        for j in range(CMP_LEN):
            posb = posb + _dot(pos_ref[kind, j].astype(MM), w1_ref[kind, j])
        hid = jax.nn.gelu(lo + pltpu.roll(hi, nh - 1, 0) + posb[0:1, :])
        for h in range(NSA_KV_HEADS):
            out = _dot(hid[:, h * CMP_HIDDEN:(h + 1) * CMP_HIDDEN].astype(MM), w2_ref[kind])
            if kind == 0:
                kc_ref[0, h] = _rope_lanes(out, cos_ref[...], up_ref[...], dn_ref[...])
            else:
                vct_ref[0, h] = out.T.astype(MM)


def _nsa_compress(proj, pos, w1, w2, tabs):
    b, s, _ = proj.shape
    nh = s // CMP_STRIDE
    hkv = NSA_KV_HEADS
    kv_blk = sum(IN_SIZES[:4]) // LANE
    once = pl.Buffered(1)
    c2 = lambda i: (0, 0)
    c3 = lambda i: (0, 0, 0)
    c4 = lambda i: (0, 0, 0, 0)
    return pl.pallas_call(
        _cmp_kernel,
        grid=(b,),
        in_specs=[pl.BlockSpec((1, s, LANE), lambda i: (i, 0, kv_blk)),
                  pl.BlockSpec((1, s, LANE), lambda i: (i, 0, kv_blk + 1)),
                  pl.BlockSpec(pos.shape, c4),
                  pl.BlockSpec(w1.shape, c4, pipeline_mode=once),
                  pl.BlockSpec(w2.shape, c3),
                  pl.BlockSpec((nh, LANE), c2), pl.BlockSpec((nh, LANE), c2), pl.BlockSpec((nh, LANE), c2)],
        out_specs=[pl.BlockSpec((1, hkv, nh, LANE), lambda i: (i, 0, 0, 0)),
                   pl.BlockSpec((1, hkv, LANE, nh), lambda i: (i, 0, 0, 0))],
        out_shape=[jax.ShapeDtypeStruct((b, hkv, nh, LANE), F32),
                   jax.ShapeDtypeStruct((b, hkv, LANE, nh), MM)],
        compiler_params=_cparams("parallel"),
        name="nsa_compress",
    )(proj, proj, pos, w1, w2, *tabs)


def _softmax_update(s, vt, m, acc):
    m_new = jnp.maximum(m, jnp.max(s, axis=0, keepdims=True))
    alpha = jnp.exp2(m - m_new)
    p = jnp.exp2(s - m_new)
    acc = alpha * acc + _dot(vt, p.astype(MM))
    return m_new, acc


def _normalise(acc):
    return acc[:NSA_HD, :] * (1.0 / acc[NSA_HD:NSA_HD + 1, :])


def _nsa_attn_kernel(q_ref, gt_ref, kc_ref, vc_ref, ka_ref, vs_ref, vd_ref, kw_ref, vw_ref, o_ref,
                     ps_ref, sa_ref, sb_ref):
    blk = pl.program_id(2)
    qb = Q_BLOCK
    cols = NSA_GQA * qb
    ncmp_pad = kc_ref.shape[2]
    nsel = LANE
    qf = q_ref[0, 0, 0]
    qm = qf.astype(MM)
    t = blk * qb + (lax.broadcasted_iota(jnp.int32, (1, cols), 1) & (qb - 1))
    lower = lax.broadcasted_iota(jnp.int32, (qb, 1), 0) <= (t & (qb - 1))

    kb = SEL_CHUNK

    s_c = _dot(kc_ref[0, 0], qf)
    n_id = lax.broadcasted_iota(jnp.int32, (ncmp_pad, 1), 0)
    s_c = jnp.where(n_id * CMP_STRIDE + (CMP_LEN - 1) <= t, s_c, NEG)
    e_c = jnp.exp2(s_c - jnp.max(s_c, axis=0, keepdims=True))
    den = jnp.sum(e_c, axis=0, keepdims=True)
    p_c = e_c * jnp.where(t >= CMP_LEN - 1, 1.0 / den, 0.0)
    o_c = _dot(vc_ref[0, 0], p_c.astype(MM))[:NSA_HD, :]

    r = SLC_BLOCK // CMP_STRIDE
    nrow = ncmp_pad // r
    ps_ref[...] = p_c[:, 0:qb] + p_c[:, qb:2 * qb] + p_c[:, 2 * qb:3 * qb] + p_c[:, 3 * qb:4 * qb]
    imp = ps_ref[pl.ds(0, nrow, stride=r), :]
    for k in range(1, r - 1):
        imp = imp + ps_ref[pl.ds(k, nrow, stride=r), :]
    edge = 0.5 * ps_ref[pl.ds(r - 1, nrow, stride=r), :]
    first = lax.broadcasted_iota(jnp.int32, (nrow, qb), 0) == 0
    imp = imp + edge + jnp.where(first, 0.0, pltpu.roll(edge, 1, 0))
    if nrow < nsel:
        imp = jnp.concatenate([imp, jnp.zeros((nsel - nrow, qb), F32)], axis=0)

    cur = (blk * qb + lax.broadcasted_iota(jnp.int32, (1, qb), 1)) // SLC_BLOCK
    jid = lax.broadcasted_iota(jnp.int32, (nsel, qb), 0)
    valid = jid <= cur
    forced = (jid == 0) | (jid == cur) | (jid == cur - 1)
    score0 = jnp.where(valid, jnp.where(forced, FORCE_SCORE, imp), -1.0)
    score = jnp.where(forced & valid, -3e38, score0)
    for _ in range(SLC_TOPK - 3):
        mx = jnp.max(score, axis=0, keepdims=True)
        idx = jnp.min(jnp.where(score == mx, jid, nsel), axis=0, keepdims=True)
        score = jnp.where(jid == idx, -3e38, score)
    own = 2 * blk * (qb // (2 * SLC_BLOCK))
    bias = jnp.where((score < -1e38) & (score0 >= 0.0) & (jid < own), 0.0, NEG).astype(MM)
    s_w = []
    v_w = []
    for back in range(WIN // qb, -1, -1):
        c = blk - back
        cc = jnp.maximum(c, 0)
        off = pl.multiple_of(cc * qb, qb)
        s = _dot(kw_ref[0, 0, pl.ds(off, qb), :], qm)
        if back == WIN // qb:
            s = jnp.where(lower | (c < 0), NEG, s)
        elif back == 0:
            s = jnp.where(lower, s, NEG)
        else:
            s = jnp.where(c < 0, NEG, s)
        s_w.append(s)
        v_w.append(vw_ref[0, 0, cc])
    s_w = jnp.concatenate(s_w, axis=0)
    p_w = jnp.exp2(s_w - jnp.max(s_w, axis=0, keepdims=True))
    o_w = _normalise(_dot(jnp.concatenate(v_w, axis=1), p_w.astype(MM)))

    s_d = _dot(ka_ref[0, 0, pl.ds(pl.multiple_of(blk * qb, qb), qb), LANE:2 * LANE], qm)
    init = _softmax_update(jnp.where(lower, s_d, NEG), vd_ref[0, 0, blk],
                           jnp.full((1, cols), NEG, F32), jnp.zeros((V_ROWS, cols), F32))
    q_aug = jnp.concatenate([jnp.concatenate([bias] * NSA_GQA, axis=1), qm], axis=0)

    nchunk = ka_ref.shape[2] // kb
    diag = blk * qb

    def scores_to(s_ref, j):
        off = pl.multiple_of(jnp.minimum(j, nchunk - 1) * kb, kb)
        s_ref[...] = _dot(ka_ref[0, 0, pl.ds(off, kb), :], q_aug)

    def update_from(s_ref, j, m, acc):
        return _softmax_update(s_ref[...], vs_ref[0, 0, jnp.minimum(j, nchunk - 1)], m, acc)

    def pair_body(i, carry):
        scores_to(sb_ref, 2 * i + 1)
        carry = update_from(sa_ref, 2 * i, *carry)
        scores_to(sa_ref, 2 * i + 2)
        return update_from(sb_ref, 2 * i + 1, *carry)

    nneed = (diag + kb - 1) // kb
    scores_to(sa_ref, 0)
    carry = lax.fori_loop(0, nneed // 2, pair_body, init)
    carry = lax.cond(nneed % 2 == 1, lambda c: update_from(sa_ref, nneed - 1, *c), lambda c: c, carry)
    o_s = _normalise(carry[1])

    g = jax.nn.sigmoid(gt_ref[0, 0, 0])
    o = g[0:1, :] * o_c + g[1:2, :] * o_s + g[2:3, :] * o_w
    for gp in range(NSA_GQA // 2):
        pair = jnp.concatenate([o[:, 2 * gp * qb:(2 * gp + 1) * qb],
                                o[:, (2 * gp + 1) * qb:(2 * gp + 2) * qb]], axis=0)
        o_ref[0, :, gp * LANE:(gp + 1) * LANE] = pair.T


def _nsa_attn(qt, gt, kc, vct, ka, vst, vsd, kw, vwt):
    b, hkv, nqb, _, cols = qt.shape
    s = ka.shape[2]
    ncmp_pad = kc.shape[2]
    assert CMP_LEN == 2 * CMP_STRIDE and SLC_BLOCK % CMP_STRIDE == 0 and s // SLC_BLOCK <= LANE
    assert (s // SEL_CHUNK) % 2 == 0
    per_head = lambda i, h, j: (i, h, 0, 0)
    per_head5 = lambda i, h, j: (i, h, 0, 0, 0)
    per_blk = lambda i, h, j: (i, h, j, 0, 0)
    return pl.pallas_call(
        _nsa_attn_kernel,
        grid=(b, hkv, nqb),
        in_specs=[pl.BlockSpec((1, 1, 1, LANE, cols), per_blk),
                  pl.BlockSpec((1, 1, 1, 8, cols), per_blk),
                  pl.BlockSpec((1, 1, ncmp_pad, LANE), per_head),
                  pl.BlockSpec((1, 1, LANE, ncmp_pad), per_head),
                  pl.BlockSpec((1, 1, s, 2 * LANE), per_head),
                  pl.BlockSpec((1, 1, s // SEL_CHUNK, V_ROWS, SEL_CHUNK), per_head5),
                  pl.BlockSpec((1, 1, s // Q_BLOCK, V_ROWS, Q_BLOCK), per_head5),
                  pl.BlockSpec((1, 1, s, LANE), per_head),
                  pl.BlockSpec((1, 1, s // Q_BLOCK, V_ROWS, Q_BLOCK), per_head5)],
        out_specs=pl.BlockSpec((1, Q_BLOCK, NSA_GQA * NSA_HD), lambda i, h, j: (i, j, h)),
        out_shape=jax.ShapeDtypeStruct((b, s, hkv * NSA_GQA * NSA_HD), F32),
        scratch_shapes=[pltpu.VMEM((ncmp_pad, Q_BLOCK), F32),
                        pltpu.VMEM((SEL_CHUNK, cols), F32), pltpu.VMEM((SEL_CHUNK, cols), F32)],
        compiler_params=_cparams("parallel", "parallel", "arbitrary"),
        name="nsa_attn",
    )(qt, gt, kc, vct, ka, vst, vsd, kw, vwt)


def _nsa_weights(pos_k, pos_v, ck_w1, ck_w2, cv_w1, cv_w2):
    dh = NSA_HD
    eye = jnp.eye(NSA_KV_HEADS, dtype=F32)
    slabs = lambda w: jnp.einsum("jrc,hg->jhrgc", w.reshape(CMP_LEN, dh, CMP_HIDDEN), eye).reshape(
        CMP_LEN, NSA_KV_HEADS * dh, NSA_KV_HEADS * CMP_HIDDEN)
    w1 = jnp.stack([slabs(ck_w1), slabs(cv_w1)]).astype(MM)
    w2 = jnp.pad(jnp.stack([ck_w2, cv_w2]), ((0, 0), (0, 0), (0, LANE - dh))).astype(MM)
    pos = jnp.stack([pos_k, pos_v])
    pos = jnp.broadcast_to(jnp.concatenate([pos] * NSA_KV_HEADS, axis=-1)[:, :, None, :],
                           (2, CMP_LEN, 8, NSA_KV_HEADS * dh))
    return pos, w1, w2


def _nsa(proj, weights, tabs):
    tok_tabs, cmp_tabs = tabs
    qt, gt, ka, kw, vst, vsd, vwt = _nsa_prep(proj, tok_tabs)
    kc, vct = _nsa_compress(proj, *weights, cmp_tabs)
    return _nsa_attn(qt, gt, kc, vct, ka, vst, vsd, kw, vwt)


def _nsa_tables(s):
    nhalf = s // CMP_STRIDE
    tok_tabs = _nsa_rope_tables(jnp.arange(s))
    cmp_tabs = _nsa_rope_tables(jnp.arange(nhalf) * CMP_STRIDE + CMP_LEN - 1)
    return tok_tabs, cmp_tabs


def _out_kernel(x_ref, g_ref, yl_ref, ys_ref, yr_ref, yn_ref, wgm_ref, wb_ref, wo_ref,
                fg_ref, o_ref, *, final):
    x = x_ref[0]
    h = _rms(x, g_ref[...]).astype(MM)
    w = BRANCH_W
    d = D_MODEL
    m0 = N_BRANCH * w
    merged = jnp.zeros(x.shape, F32)
    for n, y_ref in enumerate((yl_ref, ys_ref, yr_ref, yn_ref)):
        gate = jax.nn.silu(_dot(h, wgm_ref[0, :, n * w:(n + 1) * w]))
        branch = _dot((y_ref[0] * gate).astype(MM), wb_ref[n])
        merged = merged + jax.nn.sigmoid(_dot(h, wgm_ref[0, :, m0 + n * d:m0 + (n + 1) * d])) * branch
    out = x + _dot(merged.astype(MM), wo_ref[...])
    if final:
        out = _rms(out, fg_ref[...])
    o_ref[0] = out


def _out(x, g, ys, wgm, l, wb, wo, fg, final):
    b, s, d = x.shape
    w = BRANCH_W
    tm = TM_PROJ
    row = lambda i, j: (i, j, 0)
    c2 = lambda i, j: (0, 0)
    once = pl.Buffered(1)
    return pl.pallas_call(
        functools.partial(_out_kernel, final=final),
        grid=(b, s // tm),
        in_specs=[pl.BlockSpec((1, tm, d), row),
                  pl.BlockSpec((1, d), c2)]
                 + [pl.BlockSpec((1, tm, w), row)] * N_BRANCH
                 + [pl.BlockSpec((1, d, N_BRANCH * (w + d)), lambda i, j: (l, 0, 0), pipeline_mode=once),
                    pl.BlockSpec((N_BRANCH, w, d), lambda i, j: (0, 0, 0), pipeline_mode=once),
                    pl.BlockSpec((d, d), c2, pipeline_mode=once),
                    pl.BlockSpec((1, d), c2)],
        out_specs=pl.BlockSpec((1, tm, d), row),
        out_shape=jax.ShapeDtypeStruct((b, s, d), F32),
        compiler_params=_cparams("parallel", "arbitrary"),
        name="gate_merge_out",
    )(x, g, *ys, wgm, wb, wo, fg)


def _split_w_in(w_in):
    pad = jnp.zeros(w_in.shape[:2] + (PROJ_W - GATE_OFF,), w_in.dtype)
    w_mix = jnp.concatenate([w_in[:, :, :GATE_OFF], pad], axis=2).astype(MM)
    return w_mix, w_in[:, :, GATE_OFF:].astype(MM)


def _layer(x, p, l, w_mix, w_gm, ret_tabs, nsa_tabs, final_g):
    g = p["norm_g"][l].reshape(1, D_MODEL)
    proj = _inproj(x, g, w_mix, l)

    wab = jnp.concatenate([_block_diag(p["lru_wa"][l]), _block_diag(p["lru_wx"][l])], axis=1).astype(MM)
    bab = jnp.concatenate([p["lru_ba"][l], p["lru_bx"][l]]).reshape(1, 2 * BRANCH_W)
    y_lru = _lru(proj, p["lru_conv_w"][l], p["lru_conv_b"][l].reshape(1, -1), wab, bab,
                 p["lru_lambda"][l].reshape(1, -1))

    wb, apr, api, wc = _s5_tables(p["s5_lambda_re"][l], p["s5_lambda_im"][l], p["s5_b_re"][l],
                                  p["s5_b_im"][l], p["s5_c_re"][l], p["s5_c_im"][l], p["s5_log_dt"][l])
    y_s5 = _s5(proj, wb, apr, api, wc, p["s5_d"][l].reshape(1, -1), p["s5_w_glu"][l].astype(MM),
               p["s5_b_glu"][l].reshape(1, -1))

    y_ret = _retention(proj, ret_tabs)

    nsa_w = _nsa_weights(p["nsa_pos_k"][l], p["nsa_pos_v"][l], p["nsa_ck_w1"][l], p["nsa_ck_w2"][l],
                         p["nsa_cv_w1"][l], p["nsa_cv_w2"][l])
    y_nsa = _nsa(proj, nsa_w, nsa_tabs)

    final = l == DEPTH - 1
    return _out(x, g, (y_lru, y_s5, y_ret, y_nsa), w_gm, l, p["w_branch"][l].astype(MM),
                p["w_out"][l].astype(MM), final_g.reshape(1, D_MODEL), final)


def kernel(x, norm_g, w_in, lru_conv_w, lru_conv_b, lru_wa, lru_ba, lru_wx, lru_bx, lru_lambda,
           s5_lambda_re, s5_lambda_im, s5_b_re, s5_b_im, s5_c_re, s5_c_im, s5_log_dt, s5_d,
           s5_w_glu, s5_b_glu, nsa_pos_k, nsa_pos_v, nsa_ck_w1, nsa_ck_w2, nsa_cv_w1, nsa_cv_w2,
           w_branch, w_out, final_norm_g):
    p = dict(norm_g=norm_g, w_in=w_in, lru_conv_w=lru_conv_w, lru_conv_b=lru_conv_b, lru_wa=lru_wa,
             lru_ba=lru_ba, lru_wx=lru_wx, lru_bx=lru_bx, lru_lambda=lru_lambda,
             s5_lambda_re=s5_lambda_re, s5_lambda_im=s5_lambda_im, s5_b_re=s5_b_re, s5_b_im=s5_b_im,
             s5_c_re=s5_c_re, s5_c_im=s5_c_im, s5_log_dt=s5_log_dt, s5_d=s5_d, s5_w_glu=s5_w_glu,
             s5_b_glu=s5_b_glu, nsa_pos_k=nsa_pos_k, nsa_pos_v=nsa_pos_v, nsa_ck_w1=nsa_ck_w1,
             nsa_ck_w2=nsa_ck_w2, nsa_cv_w1=nsa_cv_w1, nsa_cv_w2=nsa_cv_w2, w_branch=w_branch,
             w_out=w_out)
    s = x.shape[1]
    ret_tabs = _ret_tables(s)
    nsa_tabs = _nsa_tables(s)
    w_mix, w_gm = _split_w_in(w_in)
    for l in range(DEPTH):
        x = _layer(x, p, l, w_mix, w_gm, ret_tabs, nsa_tabs, final_norm_g)
    return x
```

```python
import functools
import math

import jax
import jax.numpy as jnp
from jax import lax
from jax.experimental import pallas as pl
from jax.experimental.pallas import tpu as pltpu

F32 = jnp.float32
MM = jnp.bfloat16

D_MODEL = 1024
DEPTH = 4
N_BRANCH = 4
BRANCH_W = 512
NORM_EPS = 1e-6
LRU_HEADS = 8
LRU_HD = BRANCH_W // LRU_HEADS
CONV_W = 4
LRU_C = 8.0
S5_GROUP = 16
S5_GROUPS = BRANCH_W // S5_GROUP
S5_STATE = 64
S5_N = S5_GROUPS * S5_STATE
RET_HEADS = 4
RET_HD = BRANCH_W // RET_HEADS
RET_CHUNK = 128
RET_ROPE_BASE = 10000.0
NSA_HEADS = 8
NSA_KV_HEADS = 2
NSA_HD = BRANCH_W // NSA_HEADS
NSA_GQA = NSA_HEADS // NSA_KV_HEADS
CMP_LEN = 32
CMP_STRIDE = 16
CMP_HIDDEN = 256
SLC_BLOCK = 64
SLC_TOPK = 16
WIN = 512
Q_BLOCK = 128
ROPE_THETA = 500000.0
ROPE_DIM = NSA_HD // 4
FORCE_SCORE = 1e4
NEG = -1e30
NSA_KV_W = 6 * NSA_KV_HEADS * NSA_HD
IN_SIZES = (BRANCH_W, BRANCH_W, 3 * BRANCH_W, NSA_HEADS * NSA_HD, NSA_KV_W, 3 * NSA_HEADS,
            N_BRANCH * BRANCH_W, N_BRANCH * D_MODEL)
LANE = 128
MIX_W = sum(IN_SIZES[:5])
BG_OFF = MIX_W
GATE_OFF = BG_OFF + IN_SIZES[5]
MERGE_OFF = GATE_OFF + IN_SIZES[6]
PROJ_W = MIX_W + LANE
VMEM_LIMIT = 56 * 1024 * 1024

TM_PROJ = 512
T_LRU = 256
T_S5 = 256
T_RET = 512
S5_TILE_GROUP = 4
SEL_CHUNK = 512
NSA_T = SEL_CHUNK
V_ROWS = NSA_HD + 16


def _cparams(*sem):
    return pltpu.CompilerParams(dimension_semantics=sem, vmem_limit_bytes=VMEM_LIMIT)


def _rms(x, g):
    ms = jnp.mean(x * x, axis=-1, keepdims=True)
    return (x * lax.rsqrt(ms + NORM_EPS)) * g


def _dot(a, b):
    return jnp.dot(a, b, preferred_element_type=F32)


def _dot_nt(a, b):
    return lax.dot_general(a, b, (((1,), (1,)), ((), ())), preferred_element_type=F32)


def _dot_tn(a, b):
    return lax.dot_general(a, b, (((0,), (0,)), ((), ())), preferred_element_type=F32)


def _inproj_kernel(x_ref, g_ref, w_ref, o_ref):
    h = _rms(x_ref[0], g_ref[...])
    o_ref[0] = _dot(h.astype(MM), w_ref[0])


def _inproj(x, g, w, l):
    b, s, d = x.shape
    n = w.shape[2]
    return pl.pallas_call(
        _inproj_kernel,
        grid=(b, s // TM_PROJ),
        in_specs=[pl.BlockSpec((1, TM_PROJ, d), lambda i, j: (i, j, 0)),
                  pl.BlockSpec((1, d), lambda i, j: (0, 0)),
                  pl.BlockSpec((1, d, n), lambda i, j: (l, 0, 0))],
        out_specs=pl.BlockSpec((1, TM_PROJ, n), lambda i, j: (i, j, 0)),
        out_shape=jax.ShapeDtypeStruct((b, s, n), F32),
        compiler_params=_cparams("parallel", "arbitrary"),
        name="inproj",
    )(x, g, w)


def _lru_kernel(u_ref, cw_ref, cb_ref, wab_ref, bab_ref, lam_ref, o_ref, ubuf, hcar, la_ref, lb_ref):
    t = T_LRU
    w = BRANCH_W

    @pl.when(pl.program_id(1) == 0)
    def _():
        ubuf[0:8, :] = jnp.zeros((8, w), F32)
        hcar[...] = jnp.zeros((1, w), F32)

    ubuf[8:8 + t, :] = u_ref[0]
    xc = cb_ref[...] + cw_ref[0:1, :] * ubuf[5:5 + t, :]
    for k in range(1, CONV_W):
        xc = xc + cw_ref[k:k + 1, :] * ubuf[5 + k:5 + k + t, :]
    ubuf[0:8, :] = ubuf[t:t + 8, :]

    ri = _dot(xc.astype(MM), wab_ref[...]) + bab_ref[...]
    r = jax.nn.sigmoid(ri[:, :w])
    gi = jax.nn.sigmoid(ri[:, w:])
    z = -lam_ref[...]
    softplus = jnp.maximum(z, 0.0) + jnp.log1p(jnp.exp(-jnp.abs(z)))
    a = jnp.exp(-LRU_C * r * softplus)
    bt = jnp.sqrt(1.0 - a * a) * gi * xc

    sub = 8
    grp = t // sub
    i8 = lax.broadcasted_iota(jnp.int32, (sub, LANE), 0)
    rowg = lax.broadcasted_iota(jnp.int32, (grp, LANE), 0)
    last = pl.ds(sub - 1, grp, stride=sub)
    tiles = range(w // LANE)
    loc = []
    for cb in tiles:
        lanes = slice(cb * LANE, (cb + 1) * LANE)
        a3 = a[:, lanes].reshape(grp, sub, LANE)
        b3 = bt[:, lanes].reshape(grp, sub, LANE)
        k = 1
        while k < sub:
            keep = (i8 >= k)[None]
            a_sh = jnp.where(keep, pltpu.roll(a3, k, 1), 1.0)
            b_sh = jnp.where(keep, pltpu.roll(b3, k, 1), 0.0)
            b3 = a3 * b_sh + b3
            a3 = a3 * a_sh
            k *= 2
        loc.append((a3, b3))
        la_ref[cb] = a3.reshape(t, LANE)
        lb_ref[cb] = b3.reshape(t, LANE)
    ends = [(la_ref[cb, last, :], lb_ref[cb, last, :]) for cb in tiles]
    k = 1
    while k < grp:
        keep = rowg >= k
        ends = [(ae * jnp.where(keep, pltpu.roll(ae, k, 0), 1.0),
                 ae * jnp.where(keep, pltpu.roll(be, k, 0), 0.0) + be) for ae, be in ends]
        k *= 2
    hc = hcar[...]
    full = jnp.concatenate([be + ae * hc[:, cb * LANE:(cb + 1) * LANE] for cb, (ae, be) in zip(tiles, ends)], axis=1)
    hcar[...] = full[grp - 1:grp, :]
    row_w = lax.broadcasted_iota(jnp.int32, (grp, w), 0)
    e = jnp.where(row_w >= 1, pltpu.roll(full, 1, 0), hc)
    e_hi = e.astype(MM)
    r1 = e - e_hi.astype(F32)
    e_mid = r1.astype(MM)
    e_lo = (r1 - e_mid.astype(F32)).astype(MM)
    rep = lax.broadcasted_iota(jnp.int32, (t, grp), 0) // sub == lax.broadcasted_iota(jnp.int32, (t, grp), 1)
    rep = jnp.where(rep, 1.0, 0.0).astype(MM)
    z_in = _dot(jnp.concatenate([rep, rep, rep], axis=1), jnp.concatenate([e_hi, e_mid, e_lo], axis=0))
    for cb, (a3, b3) in zip(tiles, loc):
        lanes = slice(cb * LANE, (cb + 1) * LANE)
        o_ref[0, :, lanes] = b3.reshape(t, LANE) + a3.reshape(t, LANE) * z_in[:, lanes]


def _lru(proj, cw, cb, wab, bab, lam):
    b, s, _ = proj.shape
    w = BRANCH_W
    return pl.pallas_call(
        _lru_kernel,
        grid=(b, s // T_LRU),
        in_specs=[pl.BlockSpec((1, T_LRU, w), lambda i, j: (i, j, 0)),
                  pl.BlockSpec((CONV_W, w), lambda i, j: (0, 0)),
                  pl.BlockSpec((1, w), lambda i, j: (0, 0)),
                  pl.BlockSpec((w, 2 * w), lambda i, j: (0, 0)),
                  pl.BlockSpec((1, 2 * w), lambda i, j: (0, 0)),
                  pl.BlockSpec((1, w), lambda i, j: (0, 0))],
        out_specs=pl.BlockSpec((1, T_LRU, w), lambda i, j: (i, j, 0)),
        out_shape=jax.ShapeDtypeStruct((b, s, w), F32),
        scratch_shapes=[pltpu.VMEM((T_LRU + 8, w), F32), pltpu.VMEM((1, w), F32),
                        pltpu.VMEM((w // LANE, T_LRU, LANE), F32), pltpu.VMEM((w // LANE, T_LRU, LANE), F32)],
        compiler_params=_cparams("parallel", "arbitrary"),
        name="rg_lru",
    )(proj, cw, cb, wab, bab, lam)


def _block_diag(blocks):
    n, r, c = blocks.shape
    eye = jnp.eye(n, dtype=blocks.dtype)
    return jnp.einsum("nrc,nm->nrmc", blocks, eye).reshape(n * r, n * c)


def _cmul_add(xr, xi, ar, ai, br, bi):
    return xr + (ar * br - ai * bi), xi + (ar * bi + ai * br)


def _s5_kernel(u_ref, wb_ref, apr_ref, api_ref, wc_ref, d_ref, wg_ref, bg_ref, o_ref,
               car, lr_ref, li_ref, y_ref):
    t = T_S5
    n = S5_N
    w = BRANCH_W
    sub = 8
    grp = t // sub
    mxu = 256

    @pl.when(pl.program_id(1) == 0)
    def _():
        car[...] = jnp.zeros((2, n), F32)

    u = u_ref[0]
    ub = u.astype(MM)
    y_ref[...] = d_ref[...] * u
    first = lax.broadcasted_iota(jnp.int32, (t, LANE), 0) == 0
    i8 = lax.broadcasted_iota(jnp.int32, (sub, LANE), 0)
    rowg = lax.broadcasted_iota(jnp.int32, (grp, LANE), 0)
    rep = lax.broadcasted_iota(jnp.int32, (t, grp), 0) // sub == lax.broadcasted_iota(jnp.int32, (t, grp), 1)
    rep = jnp.where(rep, 1.0, 0.0).astype(MM)
    rep3 = jnp.concatenate([rep, rep, rep], axis=1)
    last = pl.ds(sub - 1, grp, stride=sub)

    ntile = n // LANE
    for c0 in range(0, ntile, S5_TILE_GROUP):
        tiles = range(c0, c0 + S5_TILE_GROUP)
        lanes = [slice(cb * LANE, (cb + 1) * LANE) for cb in tiles]
        ios = [slice(io * mxu, (io + 1) * mxu) for io in ((cb * LANE // S5_STATE * S5_GROUP) // mxu for cb in tiles)]
        aprs = [apr_ref[:, ln] for ln in lanes]
        apis = [api_ref[:, ln] for ln in lanes]
        xs = [_dot(ub[:, io], wb_ref[cb]) for cb, io in zip(tiles, ios)]
        loc = []
        for x, apr, api, ln in zip(xs, aprs, apis, lanes):
            cr, ci = _cmul_add(0.0, 0.0, apr[0:1], api[0:1], car[0:1, ln], car[1:2, ln])
            xr = (x[:, :LANE] + jnp.where(first, cr, 0.0)).reshape(grp, sub, LANE)
            xi = (x[:, LANE:] + jnp.where(first, ci, 0.0)).reshape(grp, sub, LANE)
            k = 1
            while k < sub:
                ar = jnp.where(i8 >= k, apr[k - 1:k], 0.0)[None]
                ai = jnp.where(i8 >= k, api[k - 1:k], 0.0)[None]
                xr, xi = _cmul_add(xr, xi, ar, ai, pltpu.roll(xr, k, 1), pltpu.roll(xi, k, 1))
                k *= 2
            loc.append((xr, xi))
        for j, (xr, xi) in enumerate(loc):
            lr_ref[j] = xr.reshape(t, LANE)
            li_ref[j] = xi.reshape(t, LANE)
        ends = [(lr_ref[j, last, :], li_ref[j, last, :]) for j in range(len(loc))]
        k = 1
        step = sub - 1
        while k < grp:
            keep = rowg >= k
            nxt = []
            for (fr, fi), apr, api in zip(ends, aprs, apis):
                sr = jnp.where(keep, pltpu.roll(fr, k, 0), 0.0)
                si = jnp.where(keep, pltpu.roll(fi, k, 0), 0.0)
                nxt.append(_cmul_add(fr, fi, apr[step:step + 1], api[step:step + 1], sr, si))
            ends = nxt
            k *= 2
            step += 1
        enter = rowg >= 1
        zs = []
        for (fr, fi), ln in zip(ends, lanes):
            car[0:1, ln] = fr[grp - 1:grp, :]
            car[1:2, ln] = fi[grp - 1:grp, :]
            e = jnp.concatenate([jnp.where(enter, pltpu.roll(fr, 1, 0), 0.0),
                                 jnp.where(enter, pltpu.roll(fi, 1, 0), 0.0)], axis=1)
            e_hi = e.astype(MM)
            r1 = e - e_hi.astype(F32)
            e_mid = r1.astype(MM)
            e_lo = (r1 - e_mid.astype(F32)).astype(MM)
            zs.append(_dot(rep3, jnp.concatenate([e_hi, e_mid, e_lo], axis=0)))
        for cb, io, (xr, xi), z, apr, api in zip(tiles, ios, loc, zs, aprs, apis):
            hr, hi = _cmul_add(xr, xi, apr[0:sub][None], api[0:sub][None],
                               z[:, :LANE].reshape(grp, sub, LANE), z[:, LANE:].reshape(grp, sub, LANE))
            h = jnp.concatenate([hr.reshape(t, LANE), hi.reshape(t, LANE)], axis=1).astype(MM)
            y_ref[:, io] += _dot(h, wc_ref[cb])

    z = jax.nn.gelu(y_ref[...])
    g = _dot(z.astype(MM), wg_ref[...]) + bg_ref[...]
    o_ref[0] = g[:, :w] * jax.nn.sigmoid(g[:, w:])


def _s5_tables(lam_re, lam_im, b_re, b_im, c_re, c_im, log_dt):
    dt = jnp.exp(log_dt)[:, None]
    lr = jnp.minimum(lam_re, -1e-4)
    li = lam_im
    mag = jnp.exp(lr * dt)
    ab_re = mag * jnp.cos(li * dt)
    ab_im = mag * jnp.sin(li * dt)
    den = lr * lr + li * li
    nr = ab_re - 1.0
    ni = ab_im
    f_re = (nr * lr + ni * li) / den
    f_im = (ni * lr - nr * li) / den
    bb_re = f_re[..., None] * b_re - f_im[..., None] * b_im
    bb_im = f_re[..., None] * b_im + f_im[..., None] * b_re
    ntile = S5_N // LANE
    gpt = LANE // S5_STATE
    rgs = 256 // S5_GROUP
    cb = jnp.arange(ntile)
    g = jnp.arange(S5_GROUPS)
    in_tile = (g[None, None, :] == gpt * cb[:, None, None] + jnp.arange(gpt)[None, :, None]).astype(F32)
    io = (cb * gpt) // rgs
    at_row = (g[None, :, None] - rgs * io[:, None, None] == jnp.arange(rgs)[None, None, :]).astype(F32)
    tiles_b = lambda bb: jnp.einsum("kjg,kgr,gpc->krcjp", in_tile, at_row, bb).reshape(ntile, 256, LANE)
    tiles_c = lambda cc: jnp.einsum("kjg,kgr,gcp->kjprc", in_tile, at_row, cc).reshape(ntile, LANE, 256)
    wb = jnp.concatenate([tiles_b(bb_re), tiles_b(bb_im)], axis=2)
    wc = jnp.concatenate([tiles_c(c_re), tiles_c(-c_im)], axis=1)
    a_re = ab_re.reshape(1, S5_N)
    a_im = ab_im.reshape(1, S5_N)
    pr, pi = [a_re], [a_im]
    for _ in range(7):
        pr.append(pr[-1] * a_re - pi[-1] * a_im)
        pi.append(pr[-2] * a_im + pi[-1] * a_re)
    for _ in range(int(math.log2(T_S5 // 8)) - 1):
        r0, i0 = pr[-1], pi[-1]
        pr.append(r0 * r0 - i0 * i0)
        pi.append(2.0 * r0 * i0)
    pad = [jnp.zeros_like(a_re)] * (16 - len(pr))
    return wb.astype(MM), jnp.concatenate(pr + pad, 0), jnp.concatenate(pi + pad, 0), wc.astype(MM)


def _s5(proj, wb, apr, api, wc, d, wg, bg):
    b, s, _ = proj.shape
    w = BRANCH_W
    n = S5_N
    c2 = lambda i, j: (0, 0)
    c3 = lambda i, j: (0, 0, 0)
    return pl.pallas_call(
        _s5_kernel,
        grid=(b, s // T_S5),
        in_specs=[pl.BlockSpec((1, T_S5, w), lambda i, j: (i, j, 1)),
                  pl.BlockSpec(wb.shape, c3),
                  pl.BlockSpec(apr.shape, c2),
                  pl.BlockSpec(api.shape, c2),
                  pl.BlockSpec(wc.shape, c3),
                  pl.BlockSpec((1, w), c2),
                  pl.BlockSpec((w, 2 * w), c2),
                  pl.BlockSpec((1, 2 * w), c2)],
        out_specs=pl.BlockSpec((1, T_S5, w), lambda i, j: (i, j, 0)),
        out_shape=jax.ShapeDtypeStruct((b, s, w), F32),
        scratch_shapes=[pltpu.VMEM((2, n), F32), pltpu.VMEM((S5_TILE_GROUP, T_S5, LANE), F32),
                        pltpu.VMEM((S5_TILE_GROUP, T_S5, LANE), F32), pltpu.VMEM((T_S5, w), F32)],
        compiler_params=_cparams("parallel", "arbitrary"),
        name="s5",
    )(proj, wb, apr, api, wc, d, wg, bg)


def _ret_kernel(q_ref, k_ref, v_ref, cos_ref, sin_ref, dm_ref, xi_ref, zt_ref, cd_ref, o_ref, rst):
    c = RET_CHUNK
    dh = RET_HD

    @pl.when(pl.program_id(1) == 0)
    def _():
        rst[...] = jnp.zeros((RET_HEADS, dh, dh), F32)

    for ci in range(T_RET // c):
        rows = slice(ci * c, (ci + 1) * c)
        cos = cos_ref[rows, :]
        sin = sin_ref[rows, :]
        heads = range(RET_HEADS)
        cols = [slice(h * dh, (h + 1) * dh) for h in heads]
        qs, ks, vbs = [], [], []
        for h in heads:
            q = q_ref[0, rows, cols[h]]
            k = k_ref[0, rows, cols[h]]
            qs.append(q * cos + pltpu.roll(q, dh // 2, 1) * sin)
            ks.append((k * cos + pltpu.roll(k, dh // 2, 1) * sin) * (dh ** -0.5))
            vbs.append(v_ref[0, rows, cols[h]].astype(MM))
        scores = [_dot_nt(qs[h].astype(MM), ks[h].astype(MM)) * dm_ref[h] for h in heads]
        r_prev = [rst[h] for h in heads]
        cross = [_dot((qs[h] * xi_ref[h]).astype(MM), r_prev[h].astype(MM)) for h in heads]
        kv = [_dot_tn((ks[h] * zt_ref[h]).astype(MM), vbs[h]) for h in heads]
        inner = [_dot(scores[h].astype(MM), vbs[h]) for h in heads]
        for h in heads:
            rst[h] = cd_ref[h] * r_prev[h] + kv[h]
            o = inner[h] + cross[h]
            mu = jnp.mean(o, axis=-1, keepdims=True)
            var = jnp.mean(jnp.square(o - mu), axis=-1, keepdims=True)
            o_ref[0, rows, cols[h]] = (o - mu) * lax.rsqrt(var + 1e-5)


def _ret_tables(s):
    dh = RET_HD
    c = RET_CHUNK
    half = dh // 2
    inv = RET_ROPE_BASE ** (-jnp.arange(half, dtype=F32) * 2.0 / dh)
    ang = jnp.arange(s).astype(F32)[:, None] * inv[None, :]
    cos = jnp.cos(ang)
    sin = jnp.sin(ang)
    cosf = jnp.concatenate([cos, cos], axis=1)
    sinf = jnp.concatenate([-sin, sin], axis=1)
    gamma = 1.0 - jnp.exp(jnp.linspace(math.log(1.0 / 32.0), math.log(1.0 / 512.0), RET_HEADS, dtype=F32))
    log_g = jnp.log(gamma)
    n = jnp.arange(c, dtype=F32)
    diff = n[:, None] - n[None, :]
    dmask = jnp.where(diff[None] >= 0, jnp.exp(jnp.maximum(diff, 0.0)[None] * log_g[:, None, None]), 0.0)
    xi = jnp.exp((n[None, :] + 1.0) * log_g[:, None])
    zeta = jnp.exp((c - 1.0 - n)[None, :] * log_g[:, None])
    cdec = jnp.exp(c * log_g)
    xi_b = jnp.broadcast_to(xi[:, :, None], (RET_HEADS, c, dh))
    zt_b = jnp.broadcast_to(zeta[:, :, None], (RET_HEADS, c, dh))
    cd_b = jnp.broadcast_to(cdec[:, None, None], (RET_HEADS, dh, dh))
    return cosf, sinf, dmask, xi_b, zt_b, cd_b


def _retention(proj, tabs):
    b, s, _ = proj.shape
    w = BRANCH_W
    cosf, sinf, dmask, xi_b, zt_b, cd_b = tabs
    c = RET_CHUNK
    dh = RET_HD
    full3 = lambda i, j: (0, 0, 0)
    return pl.pallas_call(
        _ret_kernel,
        grid=(b, s // T_RET),
        in_specs=[pl.BlockSpec((1, T_RET, w), lambda i, j: (i, j, 2)),
                  pl.BlockSpec((1, T_RET, w), lambda i, j: (i, j, 3)),
                  pl.BlockSpec((1, T_RET, w), lambda i, j: (i, j, 4)),
                  pl.BlockSpec((T_RET, dh), lambda i, j: (j, 0)),
                  pl.BlockSpec((T_RET, dh), lambda i, j: (j, 0)),
                  pl.BlockSpec((RET_HEADS, c, c), full3),
                  pl.BlockSpec((RET_HEADS, c, dh), full3),
                  pl.BlockSpec((RET_HEADS, c, dh), full3),
                  pl.BlockSpec((RET_HEADS, dh, dh), full3)],
        out_specs=pl.BlockSpec((1, T_RET, w), lambda i, j: (i, j, 0)),
        out_shape=jax.ShapeDtypeStruct((b, s, w), F32),
        scratch_shapes=[pltpu.VMEM((RET_HEADS, dh, dh), F32)],
        compiler_params=_cparams("parallel", "arbitrary"),
        name="retention",
    )(proj, proj, proj, cosf, sinf, dmask, xi_b, zt_b, cd_b)


def _rope_lanes(x, cos, s_up, s_dn):
    half = ROPE_DIM // 2
    n = x.shape[-1]
    return x * cos + pltpu.roll(x, half, 1) * s_up + pltpu.roll(x, n - half, 1) * s_dn


def _nsa_prep_kernel(q_ref, ks_ref, vs_ref, kw_ref, vw_ref, bg_ref, cos_ref, up_ref, dn_ref,
                     qt_ref, gt_ref, ka_ref, kwo_ref, vst_ref, vsd_ref, vwt_ref):
    t = NSA_T
    qb = Q_BLOCK
    dh = NSA_HD
    cos = cos_ref[...]
    up = up_ref[...]
    dn = dn_ref[...]
    scale = NSA_HD ** -0.5 * math.log2(math.e)
    lane = lax.broadcasted_iota(jnp.int32, (t, LANE), 1)
    low = lane < dh
    tok = pl.program_id(1) * t + lax.broadcasted_iota(jnp.int32, (t, LANE), 0)
    onehot = jnp.where(tok // SLC_BLOCK == lane, 1.0, 0.0).astype(MM)
    ks = _rope_lanes(ks_ref[0], cos, up, dn)
    kw = _rope_lanes(kw_ref[0], cos, up, dn)
    for h in range(NSA_KV_HEADS):
        ks_h = ks if h == 0 else pltpu.roll(ks, LANE - h * dh, 1)
        kw_h = kw if h == 0 else pltpu.roll(kw, LANE - h * dh, 1)
        ka_ref[0, h, :, 0:LANE] = onehot
        ka_ref[0, h, :, LANE:2 * LANE] = jnp.where(low, ks_h, 0.0).astype(MM)
        kwo_ref[0, h] = jnp.where(low, kw_h, 0.0).astype(MM)

    def ones_row(n):
        return jnp.where(lax.broadcasted_iota(jnp.int32, (V_ROWS - dh, n), 0) == 0, 1.0, 0.0).astype(MM)

    r8 = lax.broadcasted_iota(jnp.int32, (8, NSA_GQA * qb), 0)
    for i in range(t // qb):
        rows = slice(i * qb, (i + 1) * qb)
        vs_t = vs_ref[0, rows, :].T
        vw_t = vw_ref[0, rows, :].T
        bg_t = bg_ref[0, rows, :].T
        for h in range(NSA_KV_HEADS):
            vst_ref[0, h, 0, 0:dh, rows] = vs_t[h * dh:(h + 1) * dh].astype(MM)
            vsd_ref[0, h, i, 0:dh, :] = vs_t[h * dh:(h + 1) * dh].astype(MM)
            vsd_ref[0, h, i, dh:V_ROWS, :] = ones_row(qb)
            vwt_ref[0, h, i, 0:dh, :] = vw_t[h * dh:(h + 1) * dh].astype(MM)
            vwt_ref[0, h, i, dh:V_ROWS, :] = ones_row(qb)
            qt_ref[0, h, i, dh:LANE, :] = jnp.zeros((LANE - dh, NSA_GQA * qb), F32)
            gates = jnp.zeros((8, NSA_GQA * qb), F32)
            for br in range(3):
                base = br * NSA_HEADS + h * NSA_GQA
                row = jnp.concatenate([bg_t[base + g:base + g + 1] for g in range(NSA_GQA)], axis=1)
                gates = jnp.where(r8 == br, row, gates)
            gt_ref[0, h, i] = gates
        for cb in range(BRANCH_W // LANE):
            cols = slice(cb * LANE, (cb + 1) * LANE)
            x_t = (_rope_lanes(q_ref[0, rows, cols], cos[rows], up[rows], dn[rows]) * scale).T
            h = (2 * cb) // NSA_GQA
            g0 = (2 * cb) % NSA_GQA
            qt_ref[0, h, i, 0:dh, g0 * qb:(g0 + 1) * qb] = x_t[0:dh]
            qt_ref[0, h, i, 0:dh, (g0 + 1) * qb:(g0 + 2) * qb] = x_t[dh:2 * dh]
    for h in range(NSA_KV_HEADS):
        vst_ref[0, h, 0, dh:V_ROWS, :] = ones_row(t)


def _nsa_rope_tables(pos):
    half = ROPE_DIM // 2
    inv = ROPE_THETA ** (-jnp.arange(half, dtype=F32) * 2.0 / ROPE_DIM)
    ang = pos.astype(F32)[:, None] * inv[None, :]
    cos = jnp.cos(ang)
    sin = jnp.sin(ang)
    n = pos.shape[0]
    zeros = jnp.zeros((n, NSA_HD - ROPE_DIM), F32)
    zh = jnp.zeros((n, half), F32)
    cos_h = jnp.concatenate([cos, cos, jnp.ones((n, NSA_HD - ROPE_DIM), F32)], axis=1)
    up_h = jnp.concatenate([zh, sin, zeros], axis=1)
    dn_h = jnp.concatenate([-sin, zh, zeros], axis=1)
    tile2 = lambda a: jnp.concatenate([a, a], axis=1)
    return tile2(cos_h), tile2(up_h), tile2(dn_h)


def _nsa_prep(proj, tabs):
    b, s, _ = proj.shape
    t = NSA_T
    qb = Q_BLOCK
    hkv = NSA_KV_HEADS
    cols = NSA_GQA * qb
    assert t == SEL_CHUNK and s // SLC_BLOCK <= LANE
    q_blk = sum(IN_SIZES[:3]) // BRANCH_W
    kv_blk = sum(IN_SIZES[:4]) // LANE
    lane_blk = lambda n: pl.BlockSpec((1, t, LANE), lambda i, j: (i, j, n))
    tab_spec = pl.BlockSpec((t, LANE), lambda i, j: (j, 0))
    tile4 = lambda i, j: (i, 0, j, 0)
    tile5 = lambda i, j: (i, 0, j, 0, 0)
    return pl.pallas_call(
        _nsa_prep_kernel,
        grid=(b, s // t),
        in_specs=[pl.BlockSpec((1, t, BRANCH_W), lambda i, j: (i, j, q_blk)),
                  lane_blk(kv_blk + 2), lane_blk(kv_blk + 3), lane_blk(kv_blk + 4), lane_blk(kv_blk + 5),
                  lane_blk(BG_OFF // LANE), tab_spec, tab_spec, tab_spec],
        out_specs=[pl.BlockSpec((1, hkv, t // qb, LANE, cols), tile5),
                   pl.BlockSpec((1, hkv, t // qb, 8, cols), tile5),
                   pl.BlockSpec((1, hkv, t, 2 * LANE), tile4),
                   pl.BlockSpec((1, hkv, t, LANE), tile4),
                   pl.BlockSpec((1, hkv, 1, V_ROWS, t), tile5),
                   pl.BlockSpec((1, hkv, t // qb, V_ROWS, qb), tile5),
                   pl.BlockSpec((1, hkv, t // qb, V_ROWS, qb), tile5)],
        out_shape=[jax.ShapeDtypeStruct((b, hkv, s // qb, LANE, cols), F32),
                   jax.ShapeDtypeStruct((b, hkv, s // qb, 8, cols), F32),
                   jax.ShapeDtypeStruct((b, hkv, s, 2 * LANE), MM),
                   jax.ShapeDtypeStruct((b, hkv, s, LANE), MM),
                   jax.ShapeDtypeStruct((b, hkv, s // t, V_ROWS, t), MM),
                   jax.ShapeDtypeStruct((b, hkv, s // qb, V_ROWS, qb), MM),
                   jax.ShapeDtypeStruct((b, hkv, s // qb, V_ROWS, qb), MM)],
        compiler_params=_cparams("parallel", "parallel"),
        name="nsa_prep",
    )(proj, proj, proj, proj, proj, proj, *tabs)


def _cmp_kernel(xk_ref, xv_ref, pos_ref, w1_ref, w2_ref, cos_ref, up_ref, dn_ref, kc_ref, vct_ref):
    nh = kc_ref.shape[2]
    hid_w = NSA_KV_HEADS * CMP_HIDDEN
    for kind, x_ref in enumerate((xk_ref, xv_ref)):
        lo = jnp.zeros((nh, hid_w), F32)
        hi = jnp.zeros((nh, hid_w), F32)
        posb = jnp.zeros((8, hid_w), F32)
        for j in range(CMP_STRIDE):
            xj = x_ref[0, pl.ds(j, nh, stride=CMP_STRIDE), :].astype(MM)
            lo = lo + _dot(xj, w1_ref[kind, j])
            hi = hi + _dot(xj, w1_ref[kind, CMP_STRIDE + j])
        for j in range(CMP_LEN):
            posb = posb + _dot(pos_ref[kind, j].astype(MM), w1_ref[kind, j])
        hid = jax.nn.gelu(lo + pltpu.roll(hi, nh - 1, 0) + posb[0:1, :])
        for h in range(NSA_KV_HEADS):
            out = _dot(hid[:, h * CMP_HIDDEN:(h + 1) * CMP_HIDDEN].astype(MM), w2_ref[kind])
            if kind == 0:
                kc_ref[0, h] = _rope_lanes(out, cos_ref[...], up_ref[...], dn_ref[...])
            else:
                vct_ref[0, h] = out.T.astype(MM)


def _nsa_compress(proj, pos, w1, w2, tabs):
    b, s, _ = proj.shape
    nh = s // CMP_STRIDE
    hkv = NSA_KV_HEADS
    kv_blk = sum(IN_SIZES[:4]) // LANE
    once = pl.Buffered(1)
    c2 = lambda i: (0, 0)
    c3 = lambda i: (0, 0, 0)
    c4 = lambda i: (0, 0, 0, 0)
    return pl.pallas_call(
        _cmp_kernel,
        grid=(b,),
        in_specs=[pl.BlockSpec((1, s, LANE), lambda i: (i, 0, kv_blk)),
                  pl.BlockSpec((1, s, LANE), lambda i: (i, 0, kv_blk + 1)),
                  pl.BlockSpec(pos.shape, c4),
                  pl.BlockSpec(w1.shape, c4, pipeline_mode=once),
                  pl.BlockSpec(w2.shape, c3),
                  pl.BlockSpec((nh, LANE), c2), pl.BlockSpec((nh, LANE), c2), pl.BlockSpec((nh, LANE), c2)],
        out_specs=[pl.BlockSpec((1, hkv, nh, LANE), lambda i: (i, 0, 0, 0)),
                   pl.BlockSpec((1, hkv, LANE, nh), lambda i: (i, 0, 0, 0))],
        out_shape=[jax.ShapeDtypeStruct((b, hkv, nh, LANE), F32),
                   jax.ShapeDtypeStruct((b, hkv, LANE, nh), MM)],
        compiler_params=_cparams("parallel"),
        name="nsa_compress",
    )(proj, proj, pos, w1, w2, *tabs)


def _softmax_update(s, vt, m, acc):
    m_new = jnp.maximum(m, jnp.max(s, axis=0, keepdims=True))
    alpha = jnp.exp2(m - m_new)
    p = jnp.exp2(s - m_new)
    acc = alpha * acc + _dot(vt, p.astype(MM))
    return m_new, acc


def _normalise(acc):
    return acc[:NSA_HD, :] * (1.0 / acc[NSA_HD:NSA_HD + 1, :])


def _nsa_attn_kernel(q_ref, gt_ref, kc_ref, vc_ref, ka_ref, vs_ref, vd_ref, kw_ref, vw_ref, o_ref,
                     ps_ref, sa_ref, sb_ref):
    blk = pl.program_id(2)
    qb = Q_BLOCK
    cols = NSA_GQA * qb
    ncmp_pad = kc_ref.shape[2]
    nsel = LANE
    qf = q_ref[0, 0, 0]
    qm = qf.astype(MM)
    t = blk * qb + (lax.broadcasted_iota(jnp.int32, (1, cols), 1) & (qb - 1))
    lower = lax.broadcasted_iota(jnp.int32, (qb, 1), 0) <= (t & (qb - 1))

    kb = SEL_CHUNK

    s_c = _dot(kc_ref[0, 0], qf)
    n_id = lax.broadcasted_iota(jnp.int32, (ncmp_pad, 1), 0)
    s_c = jnp.where(n_id * CMP_STRIDE + (CMP_LEN - 1) <= t, s_c, NEG)
    e_c = jnp.exp2(s_c - jnp.max(s_c, axis=0, keepdims=True))
    den = jnp.sum(e_c, axis=0, keepdims=True)
    p_c = e_c * jnp.where(t >= CMP_LEN - 1, 1.0 / den, 0.0)
    o_c = _dot(vc_ref[0, 0], p_c.astype(MM))[:NSA_HD, :]

    r = SLC_BLOCK // CMP_STRIDE
    nrow = ncmp_pad // r
    ps_ref[...] = p_c[:, 0:qb] + p_c[:, qb:2 * qb] + p_c[:, 2 * qb:3 * qb] + p_c[:, 3 * qb:4 * qb]
    imp = ps_ref[pl.ds(0, nrow, stride=r), :]
    for k in range(1, r - 1):
        imp = imp + ps_ref[pl.ds(k, nrow, stride=r), :]
    edge = 0.5 * ps_ref[pl.ds(r - 1, nrow, stride=r), :]
    first = lax.broadcasted_iota(jnp.int32, (nrow, qb), 0) == 0
    imp = imp + edge + jnp.where(first, 0.0, pltpu.roll(edge, 1, 0))
    if nrow < nsel:
        imp = jnp.concatenate([imp, jnp.zeros((nsel - nrow, qb), F32)], axis=0)

    cur = (blk * qb + lax.broadcasted_iota(jnp.int32, (1, qb), 1)) // SLC_BLOCK
    jid = lax.broadcasted_iota(jnp.int32, (nsel, qb), 0)
    valid = jid <= cur
    forced = (jid == 0) | (jid == cur) | (jid == cur - 1)
    score0 = jnp.where(valid, jnp.where(forced, FORCE_SCORE, imp), -1.0)
    score = jnp.where(forced & valid, -3e38, score0)
    for _ in range(SLC_TOPK - 3):
        mx = jnp.max(score, axis=0, keepdims=True)
        idx = jnp.min(jnp.where(score == mx, jid, nsel), axis=0, keepdims=True)
        score = jnp.where(jid == idx, -3e38, score)
    own = 2 * blk * (qb // (2 * SLC_BLOCK))
    bias = jnp.where((score < -1e38) & (score0 >= 0.0) & (jid < own), 0.0, NEG).astype(MM)
    s_w = []
    v_w = []
    for back in range(WIN // qb, -1, -1):
        c = blk - back
        cc = jnp.maximum(c, 0)
        off = pl.multiple_of(cc * qb, qb)
        s = _dot(kw_ref[0, 0, pl.ds(off, qb), :], qm)
        if back == WIN // qb:
            s = jnp.where(lower | (c < 0), NEG, s)
        elif back == 0:
            s = jnp.where(lower, s, NEG)
        else:
            s = jnp.where(c < 0, NEG, s)
        s_w.append(s)
        v_w.append(vw_ref[0, 0, cc])
    s_w = jnp.concatenate(s_w, axis=0)
    p_w = jnp.exp2(s_w - jnp.max(s_w, axis=0, keepdims=True))
    o_w = _normalise(_dot(jnp.concatenate(v_w, axis=1), p_w.astype(MM)))

    s_d = _dot(ka_ref[0, 0, pl.ds(pl.multiple_of(blk * qb, qb), qb), LANE:2 * LANE], qm)
    init = _softmax_update(jnp.where(lower, s_d, NEG), vd_ref[0, 0, blk],
                           jnp.full((1, cols), NEG, F32), jnp.zeros((V_ROWS, cols), F32))
    q_aug = jnp.concatenate([jnp.concatenate([bias] * NSA_GQA, axis=1), qm], axis=0)

    nchunk = ka_ref.shape[2] // kb
    diag = blk * qb

    def scores_to(s_ref, j):
        off = pl.multiple_of(jnp.minimum(j, nchunk - 1) * kb, kb)
        s_ref[...] = _dot(ka_ref[0, 0, pl.ds(off, kb), :], q_aug)

    def update_from(s_ref, j, m, acc):
        return _softmax_update(s_ref[...], vs_ref[0, 0, jnp.minimum(j, nchunk - 1)], m, acc)

    def pair_body(i, carry):
        scores_to(sb_ref, 2 * i + 1)
        carry = update_from(sa_ref, 2 * i, *carry)
        scores_to(sa_ref, 2 * i + 2)
        return update_from(sb_ref, 2 * i + 1, *carry)

    nneed = (diag + kb - 1) // kb
    scores_to(sa_ref, 0)
    carry = lax.fori_loop(0, nneed // 2, pair_body, init)
    carry = lax.cond(nneed % 2 == 1, lambda c: update_from(sa_ref, nneed - 1, *c), lambda c: c, carry)
    o_s = _normalise(carry[1])

    g = jax.nn.sigmoid(gt_ref[0, 0, 0])
    o = g[0:1, :] * o_c + g[1:2, :] * o_s + g[2:3, :] * o_w
    for gp in range(NSA_GQA // 2):
        pair = jnp.concatenate([o[:, 2 * gp * qb:(2 * gp + 1) * qb],
                                o[:, (2 * gp + 1) * qb:(2 * gp + 2) * qb]], axis=0)
        o_ref[0, :, gp * LANE:(gp + 1) * LANE] = pair.T


def _nsa_attn(qt, gt, kc, vct, ka, vst, vsd, kw, vwt):
    b, hkv, nqb, _, cols = qt.shape
    s = ka.shape[2]
    ncmp_pad = kc.shape[2]
    assert CMP_LEN == 2 * CMP_STRIDE and SLC_BLOCK % CMP_STRIDE == 0 and s // SLC_BLOCK <= LANE
    assert (s // SEL_CHUNK) % 2 == 0
    per_head = lambda i, h, j: (i, h, 0, 0)
    per_head5 = lambda i, h, j: (i, h, 0, 0, 0)
    per_blk = lambda i, h, j: (i, h, j, 0, 0)
    return pl.pallas_call(
        _nsa_attn_kernel,
        grid=(b, hkv, nqb),
        in_specs=[pl.BlockSpec((1, 1, 1, LANE, cols), per_blk),
                  pl.BlockSpec((1, 1, 1, 8, cols), per_blk),
                  pl.BlockSpec((1, 1, ncmp_pad, LANE), per_head),
                  pl.BlockSpec((1, 1, LANE, ncmp_pad), per_head),
                  pl.BlockSpec((1, 1, s, 2 * LANE), per_head),
                  pl.BlockSpec((1, 1, s // SEL_CHUNK, V_ROWS, SEL_CHUNK), per_head5),
                  pl.BlockSpec((1, 1, s // Q_BLOCK, V_ROWS, Q_BLOCK), per_head5),
                  pl.BlockSpec((1, 1, s, LANE), per_head),
                  pl.BlockSpec((1, 1, s // Q_BLOCK, V_ROWS, Q_BLOCK), per_head5)],
        out_specs=pl.BlockSpec((1, Q_BLOCK, NSA_GQA * NSA_HD), lambda i, h, j: (i, j, h)),
        out_shape=jax.ShapeDtypeStruct((b, s, hkv * NSA_GQA * NSA_HD), F32),
        scratch_shapes=[pltpu.VMEM((ncmp_pad, Q_BLOCK), F32),
                        pltpu.VMEM((SEL_CHUNK, cols), F32), pltpu.VMEM((SEL_CHUNK, cols), F32)],
        compiler_params=_cparams("parallel", "parallel", "arbitrary"),
        name="nsa_attn",
    )(qt, gt, kc, vct, ka, vst, vsd, kw, vwt)


def _nsa_weights(pos_k, pos_v, ck_w1, ck_w2, cv_w1, cv_w2):
    dh = NSA_HD
    eye = jnp.eye(NSA_KV_HEADS, dtype=F32)
    slabs = lambda w: jnp.einsum("jrc,hg->jhrgc", w.reshape(CMP_LEN, dh, CMP_HIDDEN), eye).reshape(
        CMP_LEN, NSA_KV_HEADS * dh, NSA_KV_HEADS * CMP_HIDDEN)
    w1 = jnp.stack([slabs(ck_w1), slabs(cv_w1)]).astype(MM)
    w2 = jnp.pad(jnp.stack([ck_w2, cv_w2]), ((0, 0), (0, 0), (0, LANE - dh))).astype(MM)
    pos = jnp.stack([pos_k, pos_v])
    pos = jnp.broadcast_to(jnp.concatenate([pos] * NSA_KV_HEADS, axis=-1)[:, :, None, :],
                           (2, CMP_LEN, 8, NSA_KV_HEADS * dh))
    return pos, w1, w2


def _nsa(proj, weights, tabs):
    tok_tabs, cmp_tabs = tabs
    qt, gt, ka, kw, vst, vsd, vwt = _nsa_prep(proj, tok_tabs)
    kc, vct = _nsa_compress(proj, *weights, cmp_tabs)
    return _nsa_attn(qt, gt, kc, vct, ka, vst, vsd, kw, vwt)


def _nsa_tables(s):
    nhalf = s // CMP_STRIDE
    tok_tabs = _nsa_rope_tables(jnp.arange(s))
    cmp_tabs = _nsa_rope_tables(jnp.arange(nhalf) * CMP_STRIDE + CMP_LEN - 1)
    return tok_tabs, cmp_tabs


def _out_kernel(x_ref, g_ref, yl_ref, ys_ref, yr_ref, yn_ref, wgm_ref, wb_ref, wo_ref,
                fg_ref, o_ref, *, final):
    x = x_ref[0]
    h = _rms(x, g_ref[...]).astype(MM)
    w = BRANCH_W
    d = D_MODEL
    m0 = N_BRANCH * w
    merged = jnp.zeros(x.shape, F32)
    for n, y_ref in enumerate((yl_ref, ys_ref, yr_ref, yn_ref)):
        gate = jax.nn.silu(_dot(h, wgm_ref[0, :, n * w:(n + 1) * w]))
        branch = _dot((y_ref[0] * gate).astype(MM), wb_ref[n])
        merged = merged + jax.nn.sigmoid(_dot(h, wgm_ref[0, :, m0 + n * d:m0 + (n + 1) * d])) * branch
    out = x + _dot(merged.astype(MM), wo_ref[...])
    if final:
        out = _rms(out, fg_ref[...])
    o_ref[0] = out


def _out(x, g, ys, wgm, l, wb, wo, fg, final):
    b, s, d = x.shape
    w = BRANCH_W
    tm = TM_PROJ
    row = lambda i, j: (i, j, 0)
    c2 = lambda i, j: (0, 0)
    once = pl.Buffered(1)
    return pl.pallas_call(
        functools.partial(_out_kernel, final=final),
        grid=(b, s // tm),
        in_specs=[pl.BlockSpec((1, tm, d), row),
                  pl.BlockSpec((1, d), c2)]
                 + [pl.BlockSpec((1, tm, w), row)] * N_BRANCH
                 + [pl.BlockSpec((1, d, N_BRANCH * (w + d)), lambda i, j: (l, 0, 0), pipeline_mode=once),
                    pl.BlockSpec((N_BRANCH, w, d), lambda i, j: (0, 0, 0), pipeline_mode=once),
                    pl.BlockSpec((d, d), c2, pipeline_mode=once),
                    pl.BlockSpec((1, d), c2)],
        out_specs=pl.BlockSpec((1, tm, d), row),
        out_shape=jax.ShapeDtypeStruct((b, s, d), F32),
        compiler_params=_cparams("parallel", "arbitrary"),
        name="gate_merge_out",
    )(x, g, *ys, wgm, wb, wo, fg)


def _split_w_in(w_in):
    pad = jnp.zeros(w_in.shape[:2] + (PROJ_W - GATE_OFF,), w_in.dtype)
    w_mix = jnp.concatenate([w_in[:, :, :GATE_OFF], pad], axis=2).astype(MM)
    return w_mix, w_in[:, :, GATE_OFF:].astype(MM)


def _layer(x, p, l, w_mix, w_gm, ret_tabs, nsa_tabs, final_g):
    g = p["norm_g"][l].reshape(1, D_MODEL)
    proj = _inproj(x, g, w_mix, l)

    wab = jnp.concatenate([_block_diag(p["lru_wa"][l]), _block_diag(p["lru_wx"][l])], axis=1).astype(MM)
    bab = jnp.concatenate([p["lru_ba"][l], p["lru_bx"][l]]).reshape(1, 2 * BRANCH_W)
    y_lru = _lru(proj, p["lru_conv_w"][l], p["lru_conv_b"][l].reshape(1, -1), wab, bab,
                 p["lru_lambda"][l].reshape(1, -1))

    wb, apr, api, wc = _s5_tables(p["s5_lambda_re"][l], p["s5_lambda_im"][l], p["s5_b_re"][l],
                                  p["s5_b_im"][l], p["s5_c_re"][l], p["s5_c_im"][l], p["s5_log_dt"][l])
    y_s5 = _s5(proj, wb, apr, api, wc, p["s5_d"][l].reshape(1, -1), p["s5_w_glu"][l].astype(MM),
               p["s5_b_glu"][l].reshape(1, -1))

    y_ret = _retention(proj, ret_tabs)

    nsa_w = _nsa_weights(p["nsa_pos_k"][l], p["nsa_pos_v"][l], p["nsa_ck_w1"][l], p["nsa_ck_w2"][l],
                         p["nsa_cv_w1"][l], p["nsa_cv_w2"][l])
    y_nsa = _nsa(proj, nsa_w, nsa_tabs)

    final = l == DEPTH - 1
    return _out(x, g, (y_lru, y_s5, y_ret, y_nsa), w_gm, l, p["w_branch"][l].astype(MM),
                p["w_out"][l].astype(MM), final_g.reshape(1, D_MODEL), final)


def kernel(x, norm_g, w_in, lru_conv_w, lru_conv_b, lru_wa, lru_ba, lru_wx, lru_bx, lru_lambda,
           s5_lambda_re, s5_lambda_im, s5_b_re, s5_b_im, s5_c_re, s5_c_im, s5_log_dt, s5_d,
           s5_w_glu, s5_b_glu, nsa_pos_k, nsa_pos_v, nsa_ck_w1, nsa_ck_w2, nsa_cv_w1, nsa_cv_w2,
           w_branch, w_out, final_norm_g):
    p = dict(norm_g=norm_g, w_in=w_in, lru_conv_w=lru_conv_w, lru_conv_b=lru_conv_b, lru_wa=lru_wa,
             lru_ba=lru_ba, lru_wx=lru_wx, lru_bx=lru_bx, lru_lambda=lru_lambda,
             s5_lambda_re=s5_lambda_re, s5_lambda_im=s5_lambda_im, s5_b_re=s5_b_re, s5_b_im=s5_b_im,
             s5_c_re=s5_c_re, s5_c_im=s5_c_im, s5_log_dt=s5_log_dt, s5_d=s5_d, s5_w_glu=s5_w_glu,
             s5_b_glu=s5_b_glu, nsa_pos_k=nsa_pos_k, nsa_pos_v=nsa_pos_v, nsa_ck_w1=nsa_ck_w1,
             nsa_ck_w2=nsa_ck_w2, nsa_cv_w1=nsa_cv_w1, nsa_cv_w2=nsa_cv_w2, w_branch=w_branch,
             w_out=w_out)
    s = x.shape[1]
    ret_tabs = _ret_tables(s)
    nsa_tabs = _nsa_tables(s)
    w_mix, w_gm = _split_w_in(w_in)
    for l in range(DEPTH):
        x = _layer(x, p, l, w_mix, w_gm, ret_tabs, nsa_tabs, final_norm_g)
    return x
```

```python
import functools
import math

import jax
import jax.numpy as jnp
from jax import lax
from jax.experimental import pallas as pl
from jax.experimental.pallas import tpu as pltpu

F32 = jnp.float32
MM = jnp.bfloat16

D_MODEL = 1024
DEPTH = 4
N_BRANCH = 4
BRANCH_W = 512
NORM_EPS = 1e-6
LRU_HEADS = 8
LRU_HD = BRANCH_W // LRU_HEADS
CONV_W = 4
LRU_C = 8.0
S5_GROUP = 16
S5_GROUPS = BRANCH_W // S5_GROUP
S5_STATE = 64
S5_N = S5_GROUPS * S5_STATE
RET_HEADS = 4
RET_HD = BRANCH_W // RET_HEADS
RET_CHUNK = 128
RET_ROPE_BASE = 10000.0
NSA_HEADS = 8
NSA_KV_HEADS = 2
NSA_HD = BRANCH_W // NSA_HEADS
NSA_GQA = NSA_HEADS // NSA_KV_HEADS
CMP_LEN = 32
CMP_STRIDE = 16
CMP_HIDDEN = 256
SLC_BLOCK = 64
SLC_TOPK = 16
WIN = 512
Q_BLOCK = 128
ROPE_THETA = 500000.0
ROPE_DIM = NSA_HD // 4
FORCE_SCORE = 1e4
NEG = -1e30
NSA_KV_W = 6 * NSA_KV_HEADS * NSA_HD
IN_SIZES = (BRANCH_W, BRANCH_W, 3 * BRANCH_W, NSA_HEADS * NSA_HD, NSA_KV_W, 3 * NSA_HEADS,
            N_BRANCH * BRANCH_W, N_BRANCH * D_MODEL)
LANE = 128
MIX_W = sum(IN_SIZES[:5])
BG_OFF = MIX_W
GATE_OFF = BG_OFF + IN_SIZES[5]
MERGE_OFF = GATE_OFF + IN_SIZES[6]
PROJ_W = MIX_W + LANE
VMEM_LIMIT = 56 * 1024 * 1024

TM_PROJ = 512
T_LRU = 256
T_S5 = 256
T_RET = 512
S5_TILE_GROUP = 4
SEL_CHUNK = 512
NSA_T = SEL_CHUNK
V_ROWS = NSA_HD + 16


def _cparams(*sem):
    return pltpu.CompilerParams(dimension_semantics=sem, vmem_limit_bytes=VMEM_LIMIT)


def _rms(x, g):
    ms = jnp.mean(x * x, axis=-1, keepdims=True)
    return (x * lax.rsqrt(ms + NORM_EPS)) * g


def _dot(a, b):
    return jnp.dot(a, b, preferred_element_type=F32)


def _dot_nt(a, b):
    return lax.dot_general(a, b, (((1,), (1,)), ((), ())), preferred_element_type=F32)


def _dot_tn(a, b):
    return lax.dot_general(a, b, (((0,), (0,)), ((), ())), preferred_element_type=F32)


def _inproj_kernel(x_ref, g_ref, w_ref, o_ref):
    h = _rms(x_ref[0], g_ref[...])
    o_ref[0] = _dot(h.astype(MM), w_ref[0])


def _inproj(x, g, w, l):
    b, s, d = x.shape
    n = w.shape[2]
    return pl.pallas_call(
        _inproj_kernel,
        grid=(b, s // TM_PROJ),
        in_specs=[pl.BlockSpec((1, TM_PROJ, d), lambda i, j: (i, j, 0)),
                  pl.BlockSpec((1, d), lambda i, j: (0, 0)),
                  pl.BlockSpec((1, d, n), lambda i, j: (l, 0, 0))],
        out_specs=pl.BlockSpec((1, TM_PROJ, n), lambda i, j: (i, j, 0)),
        out_shape=jax.ShapeDtypeStruct((b, s, n), F32),
        compiler_params=_cparams("parallel", "arbitrary"),
        name="inproj",
    )(x, g, w)


def _lru_kernel(u_ref, cw_ref, cb_ref, wab_ref, bab_ref, lam_ref, o_ref, ubuf, hcar, la_ref, lb_ref):
    t = T_LRU
    w = BRANCH_W

    @pl.when(pl.program_id(1) == 0)
    def _():
        ubuf[0:8, :] = jnp.zeros((8, w), F32)
        hcar[...] = jnp.zeros((1, w), F32)

    ubuf[8:8 + t, :] = u_ref[0]
    xc = cb_ref[...] + cw_ref[0:1, :] * ubuf[5:5 + t, :]
    for k in range(1, CONV_W):
        xc = xc + cw_ref[k:k + 1, :] * ubuf[5 + k:5 + k + t, :]
    ubuf[0:8, :] = ubuf[t:t + 8, :]

    ri = _dot(xc.astype(MM), wab_ref[0]) + bab_ref[...]
    r = jax.nn.sigmoid(ri[:, :w])
    gi = jax.nn.sigmoid(ri[:, w:])
    z = -lam_ref[...]
    softplus = jnp.maximum(z, 0.0) + jnp.log1p(jnp.exp(-jnp.abs(z)))
    a = jnp.exp(-LRU_C * r * softplus)
    bt = jnp.sqrt(1.0 - a * a) * gi * xc

    sub = 8
    grp = t // sub
    i8 = lax.broadcasted_iota(jnp.int32, (sub, LANE), 0)
    rowg = lax.broadcasted_iota(jnp.int32, (grp, LANE), 0)
    last = pl.ds(sub - 1, grp, stride=sub)
    tiles = range(w // LANE)
    loc = []
    for cb in tiles:
        lanes = slice(cb * LANE, (cb + 1) * LANE)
        a3 = a[:, lanes].reshape(grp, sub, LANE)
        b3 = bt[:, lanes].reshape(grp, sub, LANE)
        k = 1
        while k < sub:
            keep = (i8 >= k)[None]
            a_sh = jnp.where(keep, pltpu.roll(a3, k, 1), 1.0)
            b_sh = jnp.where(keep, pltpu.roll(b3, k, 1), 0.0)
            b3 = a3 * b_sh + b3
            a3 = a3 * a_sh
            k *= 2
        loc.append((a3, b3))
        la_ref[cb] = a3.reshape(t, LANE)
        lb_ref[cb] = b3.reshape(t, LANE)
    ends = [(la_ref[cb, last, :], lb_ref[cb, last, :]) for cb in tiles]
    k = 1
    while k < grp:
        keep = rowg >= k
        ends = [(ae * jnp.where(keep, pltpu.roll(ae, k, 0), 1.0),
                 ae * jnp.where(keep, pltpu.roll(be, k, 0), 0.0) + be) for ae, be in ends]
        k *= 2
    hc = hcar[...]
    full = jnp.concatenate([be + ae * hc[:, cb * LANE:(cb + 1) * LANE] for cb, (ae, be) in zip(tiles, ends)], axis=1)
    hcar[...] = full[grp - 1:grp, :]
    row_w = lax.broadcasted_iota(jnp.int32, (grp, w), 0)
    e = jnp.where(row_w >= 1, pltpu.roll(full, 1, 0), hc)
    e_hi = e.astype(MM)
    r1 = e - e_hi.astype(F32)
    e_mid = r1.astype(MM)
    e_lo = (r1 - e_mid.astype(F32)).astype(MM)
    rep = lax.broadcasted_iota(jnp.int32, (t, grp), 0) // sub == lax.broadcasted_iota(jnp.int32, (t, grp), 1)
    rep = jnp.where(rep, 1.0, 0.0).astype(MM)
    z_in = _dot(jnp.concatenate([rep, rep, rep], axis=1), jnp.concatenate([e_hi, e_mid, e_lo], axis=0))
    for cb, (a3, b3) in zip(tiles, loc):
        lanes = slice(cb * LANE, (cb + 1) * LANE)
        o_ref[0, :, lanes] = b3.reshape(t, LANE) + a3.reshape(t, LANE) * z_in[:, lanes]


def _lru(proj, cw, cb, wab, l, bab, lam):
    b, s, _ = proj.shape
    w = BRANCH_W
    return pl.pallas_call(
        _lru_kernel,
        grid=(b, s // T_LRU),
        in_specs=[pl.BlockSpec((1, T_LRU, w), lambda i, j: (i, j, 0)),
                  pl.BlockSpec((CONV_W, w), lambda i, j: (0, 0)),
                  pl.BlockSpec((1, w), lambda i, j: (0, 0)),
                  pl.BlockSpec((1, w, 2 * w), lambda i, j: (l, 0, 0)),
                  pl.BlockSpec((1, 2 * w), lambda i, j: (0, 0)),
                  pl.BlockSpec((1, w), lambda i, j: (0, 0))],
        out_specs=pl.BlockSpec((1, T_LRU, w), lambda i, j: (i, j, 0)),
        out_shape=jax.ShapeDtypeStruct((b, s, w), F32),
        scratch_shapes=[pltpu.VMEM((T_LRU + 8, w), F32), pltpu.VMEM((1, w), F32),
                        pltpu.VMEM((w // LANE, T_LRU, LANE), F32), pltpu.VMEM((w // LANE, T_LRU, LANE), F32)],
        compiler_params=_cparams("parallel", "arbitrary"),
        name="rg_lru",
    )(proj, cw, cb, wab, bab, lam)


def _block_diag(blocks):
    n, r, c = blocks.shape
    eye = jnp.eye(n, dtype=blocks.dtype)
    return jnp.einsum("nrc,nm->nrmc", blocks, eye).reshape(n * r, n * c)


def _cmul_add(xr, xi, ar, ai, br, bi):
    return xr + (ar * br - ai * bi), xi + (ar * bi + ai * br)


def _s5_kernel(u_ref, wb_ref, apr_ref, api_ref, wc_ref, d_ref, wg_ref, bg_ref, o_ref,
               car, lr_ref, li_ref, y_ref):
    t = T_S5
    n = S5_N
    w = BRANCH_W
    sub = 8
    grp = t // sub
    mxu = 256

    @pl.when(pl.program_id(1) == 0)
    def _():
        car[...] = jnp.zeros((2, n), F32)

    u = u_ref[0]
    ub = u.astype(MM)
    y_ref[...] = d_ref[...] * u
    first = lax.broadcasted_iota(jnp.int32, (t, LANE), 0) == 0
    i8 = lax.broadcasted_iota(jnp.int32, (sub, LANE), 0)
    rowg = lax.broadcasted_iota(jnp.int32, (grp, LANE), 0)
    rep = lax.broadcasted_iota(jnp.int32, (t, grp), 0) // sub == lax.broadcasted_iota(jnp.int32, (t, grp), 1)
    rep = jnp.where(rep, 1.0, 0.0).astype(MM)
    rep3 = jnp.concatenate([rep, rep, rep], axis=1)
    last = pl.ds(sub - 1, grp, stride=sub)

    ntile = n // LANE
    for c0 in range(0, ntile, S5_TILE_GROUP):
        tiles = range(c0, c0 + S5_TILE_GROUP)
        lanes = [slice(cb * LANE, (cb + 1) * LANE) for cb in tiles]
        ios = [slice(io * mxu, (io + 1) * mxu) for io in ((cb * LANE // S5_STATE * S5_GROUP) // mxu for cb in tiles)]
        aprs = [apr_ref[0, :, ln] for ln in lanes]
        apis = [api_ref[0, :, ln] for ln in lanes]
        xs = [_dot(ub[:, io], wb_ref[0, cb]) for cb, io in zip(tiles, ios)]
        loc = []
        for x, apr, api, ln in zip(xs, aprs, apis, lanes):
            cr, ci = _cmul_add(0.0, 0.0, apr[0:1], api[0:1], car[0:1, ln], car[1:2, ln])
            xr = (x[:, :LANE] + jnp.where(first, cr, 0.0)).reshape(grp, sub, LANE)
            xi = (x[:, LANE:] + jnp.where(first, ci, 0.0)).reshape(grp, sub, LANE)
            k = 1
            while k < sub:
                ar = jnp.where(i8 >= k, apr[k - 1:k], 0.0)[None]
                ai = jnp.where(i8 >= k, api[k - 1:k], 0.0)[None]
                xr, xi = _cmul_add(xr, xi, ar, ai, pltpu.roll(xr, k, 1), pltpu.roll(xi, k, 1))
                k *= 2
            loc.append((xr, xi))
        for j, (xr, xi) in enumerate(loc):
            lr_ref[j] = xr.reshape(t, LANE)
            li_ref[j] = xi.reshape(t, LANE)
        ends = [(lr_ref[j, last, :], li_ref[j, last, :]) for j in range(len(loc))]
        k = 1
        step = sub - 1
        while k < grp:
            keep = rowg >= k
            nxt = []
            for (fr, fi), apr, api in zip(ends, aprs, apis):
                sr = jnp.where(keep, pltpu.roll(fr, k, 0), 0.0)
                si = jnp.where(keep, pltpu.roll(fi, k, 0), 0.0)
                nxt.append(_cmul_add(fr, fi, apr[step:step + 1], api[step:step + 1], sr, si))
            ends = nxt
            k *= 2
            step += 1
        enter = rowg >= 1
        zs = []
        for (fr, fi), ln in zip(ends, lanes):
            car[0:1, ln] = fr[grp - 1:grp, :]
            car[1:2, ln] = fi[grp - 1:grp, :]
            e = jnp.concatenate([jnp.where(enter, pltpu.roll(fr, 1, 0), 0.0),
                                 jnp.where(enter, pltpu.roll(fi, 1, 0), 0.0)], axis=1)
            e_hi = e.astype(MM)
            r1 = e - e_hi.astype(F32)
            e_mid = r1.astype(MM)
            e_lo = (r1 - e_mid.astype(F32)).astype(MM)
            zs.append(_dot(rep3, jnp.concatenate([e_hi, e_mid, e_lo], axis=0)))
        for cb, io, (xr, xi), z, apr, api in zip(tiles, ios, loc, zs, aprs, apis):
            hr, hi = _cmul_add(xr, xi, apr[0:sub][None], api[0:sub][None],
                               z[:, :LANE].reshape(grp, sub, LANE), z[:, LANE:].reshape(grp, sub, LANE))
            h = jnp.concatenate([hr.reshape(t, LANE), hi.reshape(t, LANE)], axis=1).astype(MM)
            y_ref[:, io] += _dot(h, wc_ref[0, cb])

    z = jax.nn.gelu(y_ref[...])
    g = _dot(z.astype(MM), wg_ref[...]) + bg_ref[...]
    o_ref[0] = g[:, :w] * jax.nn.sigmoid(g[:, w:])


def _s5_tables(lam_re, lam_im, b_re, b_im, c_re, c_im, log_dt):
    dt = jnp.exp(log_dt)[:, None]
    lr = jnp.minimum(lam_re, -1e-4)
    li = lam_im
    mag = jnp.exp(lr * dt)
    ab_re = mag * jnp.cos(li * dt)
    ab_im = mag * jnp.sin(li * dt)
    den = lr * lr + li * li
    nr = ab_re - 1.0
    ni = ab_im
    f_re = (nr * lr + ni * li) / den
    f_im = (ni * lr - nr * li) / den
    bb_re = f_re[..., None] * b_re - f_im[..., None] * b_im
    bb_im = f_re[..., None] * b_im + f_im[..., None] * b_re
    ntile = S5_N // LANE
    gpt = LANE // S5_STATE
    rgs = 256 // S5_GROUP
    cb = jnp.arange(ntile)
    g = jnp.arange(S5_GROUPS)
    in_tile = (g[None, None, :] == gpt * cb[:, None, None] + jnp.arange(gpt)[None, :, None]).astype(F32)
    io = (cb * gpt) // rgs
    at_row = (g[None, :, None] - rgs * io[:, None, None] == jnp.arange(rgs)[None, None, :]).astype(F32)
    tiles_b = lambda bb: jnp.einsum("kjg,kgr,gpc->krcjp", in_tile, at_row, bb).reshape(ntile, 256, LANE)
    tiles_c = lambda cc: jnp.einsum("kjg,kgr,gcp->kjprc", in_tile, at_row, cc).reshape(ntile, LANE, 256)
    wb = jnp.concatenate([tiles_b(bb_re), tiles_b(bb_im)], axis=2)
    wc = jnp.concatenate([tiles_c(c_re), tiles_c(-c_im)], axis=1)
    a_re = ab_re.reshape(1, S5_N)
    a_im = ab_im.reshape(1, S5_N)
    pr, pi = [a_re], [a_im]
    for _ in range(7):
        pr.append(pr[-1] * a_re - pi[-1] * a_im)
        pi.append(pr[-2] * a_im + pi[-1] * a_re)
    for _ in range(int(math.log2(T_S5 // 8)) - 1):
        r0, i0 = pr[-1], pi[-1]
        pr.append(r0 * r0 - i0 * i0)
        pi.append(2.0 * r0 * i0)
    pad = [jnp.zeros_like(a_re)] * (16 - len(pr))
    return wb.astype(MM), jnp.concatenate(pr + pad, 0), jnp.concatenate(pi + pad, 0), wc.astype(MM)


def _s5(proj, wb, apr, api, wc, l, d, wg, bg):
    b, s, _ = proj.shape
    w = BRANCH_W
    n = S5_N
    c2 = lambda i, j: (0, 0)
    layer = lambda a: pl.BlockSpec((1,) + a.shape[1:], lambda i, j: (l,) + (0,) * (a.ndim - 1))
    return pl.pallas_call(
        _s5_kernel,
        grid=(b, s // T_S5),
        in_specs=[pl.BlockSpec((1, T_S5, w), lambda i, j: (i, j, 1)),
                  layer(wb), layer(apr), layer(api), layer(wc),
                  pl.BlockSpec((1, w), c2),
                  pl.BlockSpec((w, 2 * w), c2),
                  pl.BlockSpec((1, 2 * w), c2)],
        out_specs=pl.BlockSpec((1, T_S5, w), lambda i, j: (i, j, 0)),
        out_shape=jax.ShapeDtypeStruct((b, s, w), F32),
        scratch_shapes=[pltpu.VMEM((2, n), F32), pltpu.VMEM((S5_TILE_GROUP, T_S5, LANE), F32),
                        pltpu.VMEM((S5_TILE_GROUP, T_S5, LANE), F32), pltpu.VMEM((T_S5, w), F32)],
        compiler_params=_cparams("parallel", "arbitrary"),
        name="s5",
    )(proj, wb, apr, api, wc, d, wg, bg)


def _ret_kernel(q_ref, k_ref, v_ref, cos_ref, sin_ref, dm_ref, xi_ref, zt_ref, cd_ref, o_ref, rst):
    c = RET_CHUNK
    dh = RET_HD

    @pl.when(pl.program_id(1) == 0)
    def _():
        rst[...] = jnp.zeros((RET_HEADS, dh, dh), F32)

    for ci in range(T_RET // c):
        rows = slice(ci * c, (ci + 1) * c)
        cos = cos_ref[rows, :]
        sin = sin_ref[rows, :]
        heads = range(RET_HEADS)
        cols = [slice(h * dh, (h + 1) * dh) for h in heads]
        qs, ks, vbs = [], [], []
        for h in heads:
            q = q_ref[0, rows, cols[h]]
            k = k_ref[0, rows, cols[h]]
            qs.append(q * cos + pltpu.roll(q, dh // 2, 1) * sin)
            ks.append((k * cos + pltpu.roll(k, dh // 2, 1) * sin) * (dh ** -0.5))
            vbs.append(v_ref[0, rows, cols[h]].astype(MM))
        scores = [_dot_nt(qs[h].astype(MM), ks[h].astype(MM)) * dm_ref[h] for h in heads]
        r_prev = [rst[h] for h in heads]
        cross = [_dot((qs[h] * xi_ref[h]).astype(MM), r_prev[h].astype(MM)) for h in heads]
        kv = [_dot_tn((ks[h] * zt_ref[h]).astype(MM), vbs[h]) for h in heads]
        inner = [_dot(scores[h].astype(MM), vbs[h]) for h in heads]
        for h in heads:
            rst[h] = cd_ref[h] * r_prev[h] + kv[h]
            o = inner[h] + cross[h]
            mu = jnp.mean(o, axis=-1, keepdims=True)
            var = jnp.mean(jnp.square(o - mu), axis=-1, keepdims=True)
            o_ref[0, rows, cols[h]] = (o - mu) * lax.rsqrt(var + 1e-5)


def _ret_tables(s):
    dh = RET_HD
    c = RET_CHUNK
    half = dh // 2
    inv = RET_ROPE_BASE ** (-jnp.arange(half, dtype=F32) * 2.0 / dh)
    ang = jnp.arange(s).astype(F32)[:, None] * inv[None, :]
    cos = jnp.cos(ang)
    sin = jnp.sin(ang)
    cosf = jnp.concatenate([cos, cos], axis=1)
    sinf = jnp.concatenate([-sin, sin], axis=1)
    gamma = 1.0 - jnp.exp(jnp.linspace(math.log(1.0 / 32.0), math.log(1.0 / 512.0), RET_HEADS, dtype=F32))
    log_g = jnp.log(gamma)
    n = jnp.arange(c, dtype=F32)
    diff = n[:, None] - n[None, :]
    dmask = jnp.where(diff[None] >= 0, jnp.exp(jnp.maximum(diff, 0.0)[None] * log_g[:, None, None]), 0.0)
    xi = jnp.exp((n[None, :] + 1.0) * log_g[:, None])
    zeta = jnp.exp((c - 1.0 - n)[None, :] * log_g[:, None])
    cdec = jnp.exp(c * log_g)
    xi_b = jnp.broadcast_to(xi[:, :, None], (RET_HEADS, c, dh))
    zt_b = jnp.broadcast_to(zeta[:, :, None], (RET_HEADS, c, dh))
    cd_b = jnp.broadcast_to(cdec[:, None, None], (RET_HEADS, dh, dh))
    return cosf, sinf, dmask, xi_b, zt_b, cd_b


def _retention(proj, tabs):
    b, s, _ = proj.shape
    w = BRANCH_W
    cosf, sinf, dmask, xi_b, zt_b, cd_b = tabs
    c = RET_CHUNK
    dh = RET_HD
    full3 = lambda i, j: (0, 0, 0)
    return pl.pallas_call(
        _ret_kernel,
        grid=(b, s // T_RET),
        in_specs=[pl.BlockSpec((1, T_RET, w), lambda i, j: (i, j, 2)),
                  pl.BlockSpec((1, T_RET, w), lambda i, j: (i, j, 3)),
                  pl.BlockSpec((1, T_RET, w), lambda i, j: (i, j, 4)),
                  pl.BlockSpec((T_RET, dh), lambda i, j: (j, 0)),
                  pl.BlockSpec((T_RET, dh), lambda i, j: (j, 0)),
                  pl.BlockSpec((RET_HEADS, c, c), full3),
                  pl.BlockSpec((RET_HEADS, c, dh), full3),
                  pl.BlockSpec((RET_HEADS, c, dh), full3),
                  pl.BlockSpec((RET_HEADS, dh, dh), full3)],
        out_specs=pl.BlockSpec((1, T_RET, w), lambda i, j: (i, j, 0)),
        out_shape=jax.ShapeDtypeStruct((b, s, w), F32),
        scratch_shapes=[pltpu.VMEM((RET_HEADS, dh, dh), F32)],
        compiler_params=_cparams("parallel", "arbitrary"),
        name="retention",
    )(proj, proj, proj, cosf, sinf, dmask, xi_b, zt_b, cd_b)


def _rope_lanes(x, cos, s_up, s_dn):
    half = ROPE_DIM // 2
    n = x.shape[-1]
    return x * cos + pltpu.roll(x, half, 1) * s_up + pltpu.roll(x, n - half, 1) * s_dn


def _nsa_prep_kernel(q_ref, ks_ref, vs_ref, kw_ref, vw_ref, bg_ref, cos_ref, up_ref, dn_ref,
                     qt_ref, gt_ref, ka_ref, kwo_ref, vst_ref, vsd_ref, vwt_ref):
    t = NSA_T
    qb = Q_BLOCK
    dh = NSA_HD
    cos = cos_ref[...]
    up = up_ref[...]
    dn = dn_ref[...]
    scale = NSA_HD ** -0.5 * math.log2(math.e)
    lane = lax.broadcasted_iota(jnp.int32, (t, LANE), 1)
    low = lane < dh
    tok = pl.program_id(1) * t + lax.broadcasted_iota(jnp.int32, (t, LANE), 0)
    onehot = jnp.where(tok // SLC_BLOCK == lane, 1.0, 0.0).astype(MM)
    ks = _rope_lanes(ks_ref[0], cos, up, dn)
    kw = _rope_lanes(kw_ref[0], cos, up, dn)
    for h in range(NSA_KV_HEADS):
        ks_h = ks if h == 0 else pltpu.roll(ks, LANE - h * dh, 1)
        kw_h = kw if h == 0 else pltpu.roll(kw, LANE - h * dh, 1)
        ka_ref[0, h, :, 0:LANE] = onehot
        ka_ref[0, h, :, LANE:2 * LANE] = jnp.where(low, ks_h, 0.0).astype(MM)
        kwo_ref[0, h] = jnp.where(low, kw_h, 0.0).astype(MM)

    def ones_row(n):
        return jnp.where(lax.broadcasted_iota(jnp.int32, (V_ROWS - dh, n), 0) == 0, 1.0, 0.0).astype(MM)

    r8 = lax.broadcasted_iota(jnp.int32, (8, NSA_GQA * qb), 0)
    for i in range(t // qb):
        rows = slice(i * qb, (i + 1) * qb)
        vs_t = vs_ref[0, rows, :].T
        vw_t = vw_ref[0, rows, :].T
        bg_t = bg_ref[0, rows, :].T
        for h in range(NSA_KV_HEADS):
            vst_ref[0, h, 0, 0:dh, rows] = vs_t[h * dh:(h + 1) * dh].astype(MM)
            vsd_ref[0, h, i, 0:dh, :] = vs_t[h * dh:(h + 1) * dh].astype(MM)
            vsd_ref[0, h, i, dh:V_ROWS, :] = ones_row(qb)
            vwt_ref[0, h, i, 0:dh, :] = vw_t[h * dh:(h + 1) * dh].astype(MM)
            vwt_ref[0, h, i, dh:V_ROWS, :] = ones_row(qb)
            qt_ref[0, h, i, dh:LANE, :] = jnp.zeros((LANE - dh, NSA_GQA * qb), F32)
            gates = jnp.zeros((8, NSA_GQA * qb), F32)
            for br in range(3):
                base = br * NSA_HEADS + h * NSA_GQA
                row = jnp.concatenate([bg_t[base + g:base + g + 1] for g in range(NSA_GQA)], axis=1)
                gates = jnp.where(r8 == br, row, gates)
            gt_ref[0, h, i] = gates
        for cb in range(BRANCH_W // LANE):
            cols = slice(cb * LANE, (cb + 1) * LANE)
            x_t = (_rope_lanes(q_ref[0, rows, cols], cos[rows], up[rows], dn[rows]) * scale).T
            h = (2 * cb) // NSA_GQA
            g0 = (2 * cb) % NSA_GQA
            qt_ref[0, h, i, 0:dh, g0 * qb:(g0 + 1) * qb] = x_t[0:dh]
            qt_ref[0, h, i, 0:dh, (g0 + 1) * qb:(g0 + 2) * qb] = x_t[dh:2 * dh]
    for h in range(NSA_KV_HEADS):
        vst_ref[0, h, 0, dh:V_ROWS, :] = ones_row(t)


def _nsa_rope_tables(pos):
    half = ROPE_DIM // 2
    inv = ROPE_THETA ** (-jnp.arange(half, dtype=F32) * 2.0 / ROPE_DIM)
    ang = pos.astype(F32)[:, None] * inv[None, :]
    cos = jnp.cos(ang)
    sin = jnp.sin(ang)
    n = pos.shape[0]
    zeros = jnp.zeros((n, NSA_HD - ROPE_DIM), F32)
    zh = jnp.zeros((n, half), F32)
    cos_h = jnp.concatenate([cos, cos, jnp.ones((n, NSA_HD - ROPE_DIM), F32)], axis=1)
    up_h = jnp.concatenate([zh, sin, zeros], axis=1)
    dn_h = jnp.concatenate([-sin, zh, zeros], axis=1)
    tile2 = lambda a: jnp.concatenate([a, a], axis=1)
    return tile2(cos_h), tile2(up_h), tile2(dn_h)


def _nsa_prep(proj, tabs):
    b, s, _ = proj.shape
    t = NSA_T
    qb = Q_BLOCK
    hkv = NSA_KV_HEADS
    cols = NSA_GQA * qb
    assert t == SEL_CHUNK and s // SLC_BLOCK <= LANE
    q_blk = sum(IN_SIZES[:3]) // BRANCH_W
    kv_blk = sum(IN_SIZES[:4]) // LANE
    lane_blk = lambda n: pl.BlockSpec((1, t, LANE), lambda i, j: (i, j, n))
    tab_spec = pl.BlockSpec((t, LANE), lambda i, j: (j, 0))
    tile4 = lambda i, j: (i, 0, j, 0)
    tile5 = lambda i, j: (i, 0, j, 0, 0)
    return pl.pallas_call(
        _nsa_prep_kernel,
        grid=(b, s // t),
        in_specs=[pl.BlockSpec((1, t, BRANCH_W), lambda i, j: (i, j, q_blk)),
                  lane_blk(kv_blk + 2), lane_blk(kv_blk + 3), lane_blk(kv_blk + 4), lane_blk(kv_blk + 5),
                  lane_blk(BG_OFF // LANE), tab_spec, tab_spec, tab_spec],
        out_specs=[pl.BlockSpec((1, hkv, t // qb, LANE, cols), tile5),
                   pl.BlockSpec((1, hkv, t // qb, 8, cols), tile5),
                   pl.BlockSpec((1, hkv, t, 2 * LANE), tile4),
                   pl.BlockSpec((1, hkv, t, LANE), tile4),
                   pl.BlockSpec((1, hkv, 1, V_ROWS, t), tile5),
                   pl.BlockSpec((1, hkv, t // qb, V_ROWS, qb), tile5),
                   pl.BlockSpec((1, hkv, t // qb, V_ROWS, qb), tile5)],
        out_shape=[jax.ShapeDtypeStruct((b, hkv, s // qb, LANE, cols), F32),
                   jax.ShapeDtypeStruct((b, hkv, s // qb, 8, cols), F32),
                   jax.ShapeDtypeStruct((b, hkv, s, 2 * LANE), MM),
                   jax.ShapeDtypeStruct((b, hkv, s, LANE), MM),
                   jax.ShapeDtypeStruct((b, hkv, s // t, V_ROWS, t), MM),
                   jax.ShapeDtypeStruct((b, hkv, s // qb, V_ROWS, qb), MM),
                   jax.ShapeDtypeStruct((b, hkv, s // qb, V_ROWS, qb), MM)],
        compiler_params=_cparams("parallel", "parallel"),
        name="nsa_prep",
    )(proj, proj, proj, proj, proj, proj, *tabs)


def _cmp_kernel(xk_ref, xv_ref, pos_ref, w1_ref, w2_ref, cos_ref, up_ref, dn_ref, kc_ref, vct_ref):
    nh = kc_ref.shape[2]
    hid_w = NSA_KV_HEADS * CMP_HIDDEN
    for kind, x_ref in enumerate((xk_ref, xv_ref)):
        lo = jnp.zeros((nh, hid_w), F32)
        hi = jnp.zeros((nh, hid_w), F32)
        posb = jnp.zeros((8, hid_w), F32)
        for j in range(CMP_STRIDE):
            xj = x_ref[0, pl.ds(j, nh, stride=CMP_STRIDE), :].astype(MM)
            lo = lo + _dot(xj, w1_ref[0, kind, j])
            hi = hi + _dot(xj, w1_ref[0, kind, CMP_STRIDE + j])
        for j in range(CMP_LEN):
            posb = posb + _dot(pos_ref[0, kind, j].astype(MM), w1_ref[0, kind, j])
        hid = jax.nn.gelu(lo + pltpu.roll(hi, nh - 1, 0) + posb[0:1, :])
        for h in range(NSA_KV_HEADS):
            out = _dot(hid[:, h * CMP_HIDDEN:(h + 1) * CMP_HIDDEN].astype(MM), w2_ref[0, kind])
            if kind == 0:
                kc_ref[0, h] = _rope_lanes(out, cos_ref[...], up_ref[...], dn_ref[...])
            else:
                vct_ref[0, h] = out.T.astype(MM)


def _nsa_compress(proj, pos, w1, w2, l, tabs):
    b, s, _ = proj.shape
    nh = s // CMP_STRIDE
    hkv = NSA_KV_HEADS
    kv_blk = sum(IN_SIZES[:4]) // LANE
    once = pl.Buffered(1)
    c2 = lambda i: (0, 0)
    layer = lambda a, **kw: pl.BlockSpec((1,) + a.shape[1:], lambda i: (l,) + (0,) * (a.ndim - 1), **kw)
    return pl.pallas_call(
        _cmp_kernel,
        grid=(b,),
        in_specs=[pl.BlockSpec((1, s, LANE), lambda i: (i, 0, kv_blk)),
                  pl.BlockSpec((1, s, LANE), lambda i: (i, 0, kv_blk + 1)),
                  layer(pos), layer(w1, pipeline_mode=once), layer(w2),
                  pl.BlockSpec((nh, LANE), c2), pl.BlockSpec((nh, LANE), c2), pl.BlockSpec((nh, LANE), c2)],
        out_specs=[pl.BlockSpec((1, hkv, nh, LANE), lambda i: (i, 0, 0, 0)),
                   pl.BlockSpec((1, hkv, LANE, nh), lambda i: (i, 0, 0, 0))],
        out_shape=[jax.ShapeDtypeStruct((b, hkv, nh, LANE), F32),
                   jax.ShapeDtypeStruct((b, hkv, LANE, nh), MM)],
        compiler_params=_cparams("parallel"),
        name="nsa_compress",
    )(proj, proj, pos, w1, w2, *tabs)


def _softmax_update(s, vt, m, acc):
    m_new = jnp.maximum(m, jnp.max(s, axis=0, keepdims=True))
    alpha = jnp.exp2(m - m_new)
    p = jnp.exp2(s - m_new)
    acc = alpha * acc + _dot(vt, p.astype(MM))
    return m_new, acc


def _normalise(acc):
    return acc[:NSA_HD, :] * (1.0 / acc[NSA_HD:NSA_HD + 1, :])


def _nsa_attn_kernel(q_ref, gt_ref, kc_ref, vc_ref, ka_ref, vs_ref, vd_ref, kw_ref, vw_ref, o_ref,
                     ps_ref, sa_ref, sb_ref):
    blk = pl.program_id(2)
    qb = Q_BLOCK
    cols = NSA_GQA * qb
    ncmp_pad = kc_ref.shape[2]
    nsel = LANE
    qf = q_ref[0, 0, 0]
    qm = qf.astype(MM)
    t = blk * qb + (lax.broadcasted_iota(jnp.int32, (1, cols), 1) & (qb - 1))
    lower = lax.broadcasted_iota(jnp.int32, (qb, 1), 0) <= (t & (qb - 1))

    kb = SEL_CHUNK

    s_c = _dot(kc_ref[0, 0], qf)
    n_id = lax.broadcasted_iota(jnp.int32, (ncmp_pad, 1), 0)
    s_c = jnp.where(n_id * CMP_STRIDE + (CMP_LEN - 1) <= t, s_c, NEG)
    e_c = jnp.exp2(s_c - jnp.max(s_c, axis=0, keepdims=True))
    den = jnp.sum(e_c, axis=0, keepdims=True)
    p_c = e_c * jnp.where(t >= CMP_LEN - 1, 1.0 / den, 0.0)
    o_c = _dot(vc_ref[0, 0], p_c.astype(MM))[:NSA_HD, :]

    r = SLC_BLOCK // CMP_STRIDE
    nrow = ncmp_pad // r
    ps_ref[...] = p_c[:, 0:qb] + p_c[:, qb:2 * qb] + p_c[:, 2 * qb:3 * qb] + p_c[:, 3 * qb:4 * qb]
    imp = ps_ref[pl.ds(0, nrow, stride=r), :]
    for k in range(1, r - 1):
        imp = imp + ps_ref[pl.ds(k, nrow, stride=r), :]
    edge = 0.5 * ps_ref[pl.ds(r - 1, nrow, stride=r), :]
    first = lax.broadcasted_iota(jnp.int32, (nrow, qb), 0) == 0
    imp = imp + edge + jnp.where(first, 0.0, pltpu.roll(edge, 1, 0))
    if nrow < nsel:
        imp = jnp.concatenate([imp, jnp.zeros((nsel - nrow, qb), F32)], axis=0)

    cur = (blk * qb + lax.broadcasted_iota(jnp.int32, (1, qb), 1)) // SLC_BLOCK
    jid = lax.broadcasted_iota(jnp.int32, (nsel, qb), 0)
    valid = jid <= cur
    forced = (jid == 0) | (jid == cur) | (jid == cur - 1)
    score0 = jnp.where(valid, jnp.where(forced, FORCE_SCORE, imp), -1.0)
    score = jnp.where(forced & valid, -3e38, score0)
    for _ in range(SLC_TOPK - 3):
        mx = jnp.max(score, axis=0, keepdims=True)
        idx = jnp.min(jnp.where(score == mx, jid, nsel), axis=0, keepdims=True)
        score = jnp.where(jid == idx, -3e38, score)
    own = 2 * blk * (qb // (2 * SLC_BLOCK))
    bias = jnp.where((score < -1e38) & (score0 >= 0.0) & (jid < own), 0.0, NEG).astype(MM)
    s_w = []
    v_w = []
    for back in range(WIN // qb, -1, -1):
        c = blk - back
        cc = jnp.maximum(c, 0)
        off = pl.multiple_of(cc * qb, qb)
        s = _dot(kw_ref[0, 0, pl.ds(off, qb), :], qm)
        if back == WIN // qb:
            s = jnp.where(lower | (c < 0), NEG, s)
        elif back == 0:
            s = jnp.where(lower, s, NEG)
        else:
            s = jnp.where(c < 0, NEG, s)
        s_w.append(s)
        v_w.append(vw_ref[0, 0, cc])
    s_w = jnp.concatenate(s_w, axis=0)
    p_w = jnp.exp2(s_w - jnp.max(s_w, axis=0, keepdims=True))
    o_w = _normalise(_dot(jnp.concatenate(v_w, axis=1), p_w.astype(MM)))

    s_d = _dot(ka_ref[0, 0, pl.ds(pl.multiple_of(blk * qb, qb), qb), LANE:2 * LANE], qm)
    init = _softmax_update(jnp.where(lower, s_d, NEG), vd_ref[0, 0, blk],
                           jnp.full((1, cols), NEG, F32), jnp.zeros((V_ROWS, cols), F32))
    q_aug = jnp.concatenate([jnp.concatenate([bias] * NSA_GQA, axis=1), qm], axis=0)

    nchunk = ka_ref.shape[2] // kb
    diag = blk * qb

    def scores_to(s_ref, j):
        off = pl.multiple_of(jnp.minimum(j, nchunk - 1) * kb, kb)
        s_ref[...] = _dot(ka_ref[0, 0, pl.ds(off, kb), :], q_aug)

    def update_from(s_ref, j, m, acc):
        return _softmax_update(s_ref[...], vs_ref[0, 0, jnp.minimum(j, nchunk - 1)], m, acc)

    def pair_body(i, carry):
        scores_to(sb_ref, 2 * i + 1)
        carry = update_from(sa_ref, 2 * i, *carry)
        scores_to(sa_ref, 2 * i + 2)
        return update_from(sb_ref, 2 * i + 1, *carry)

    nneed = (diag + kb - 1) // kb
    scores_to(sa_ref, 0)
    carry = lax.fori_loop(0, nneed // 2, pair_body, init)
    carry = lax.cond(nneed % 2 == 1, lambda c: update_from(sa_ref, nneed - 1, *c), lambda c: c, carry)
    o_s = _normalise(carry[1])

    g = jax.nn.sigmoid(gt_ref[0, 0, 0])
    o = g[0:1, :] * o_c + g[1:2, :] * o_s + g[2:3, :] * o_w
    for gp in range(NSA_GQA // 2):
        pair = jnp.concatenate([o[:, 2 * gp * qb:(2 * gp + 1) * qb],
                                o[:, (2 * gp + 1) * qb:(2 * gp + 2) * qb]], axis=0)
        o_ref[0, :, gp * LANE:(gp + 1) * LANE] = pair.T


def _nsa_attn(qt, gt, kc, vct, ka, vst, vsd, kw, vwt):
    b, hkv, nqb, _, cols = qt.shape
    s = ka.shape[2]
    ncmp_pad = kc.shape[2]
    assert CMP_LEN == 2 * CMP_STRIDE and SLC_BLOCK % CMP_STRIDE == 0 and s // SLC_BLOCK <= LANE
    assert (s // SEL_CHUNK) % 2 == 0
    per_head = lambda i, h, j: (i, h, 0, 0)
    per_head5 = lambda i, h, j: (i, h, 0, 0, 0)
    per_blk = lambda i, h, j: (i, h, j, 0, 0)
    return pl.pallas_call(
        _nsa_attn_kernel,
        grid=(b, hkv, nqb),
        in_specs=[pl.BlockSpec((1, 1, 1, LANE, cols), per_blk),
                  pl.BlockSpec((1, 1, 1, 8, cols), per_blk),
                  pl.BlockSpec((1, 1, ncmp_pad, LANE), per_head),
                  pl.BlockSpec((1, 1, LANE, ncmp_pad), per_head),
                  pl.BlockSpec((1, 1, s, 2 * LANE), per_head),
                  pl.BlockSpec((1, 1, s // SEL_CHUNK, V_ROWS, SEL_CHUNK), per_head5),
                  pl.BlockSpec((1, 1, s // Q_BLOCK, V_ROWS, Q_BLOCK), per_head5),
                  pl.BlockSpec((1, 1, s, LANE), per_head),
                  pl.BlockSpec((1, 1, s // Q_BLOCK, V_ROWS, Q_BLOCK), per_head5)],
        out_specs=pl.BlockSpec((1, Q_BLOCK, NSA_GQA * NSA_HD), lambda i, h, j: (i, j, h)),
        out_shape=jax.ShapeDtypeStruct((b, s, hkv * NSA_GQA * NSA_HD), F32),
        scratch_shapes=[pltpu.VMEM((ncmp_pad, Q_BLOCK), F32),
                        pltpu.VMEM((SEL_CHUNK, cols), F32), pltpu.VMEM((SEL_CHUNK, cols), F32)],
        compiler_params=_cparams("parallel", "parallel", "arbitrary"),
        name="nsa_attn",
    )(qt, gt, kc, vct, ka, vst, vsd, kw, vwt)


def _nsa_weights(pos_k, pos_v, ck_w1, ck_w2, cv_w1, cv_w2):
    dh = NSA_HD
    eye = jnp.eye(NSA_KV_HEADS, dtype=F32)
    slabs = lambda w: jnp.einsum("jrc,hg->jhrgc", w.reshape(CMP_LEN, dh, CMP_HIDDEN), eye).reshape(
        CMP_LEN, NSA_KV_HEADS * dh, NSA_KV_HEADS * CMP_HIDDEN)
    w1 = jnp.stack([slabs(ck_w1), slabs(cv_w1)]).astype(MM)
    w2 = jnp.pad(jnp.stack([ck_w2, cv_w2]), ((0, 0), (0, 0), (0, LANE - dh))).astype(MM)
    pos = jnp.stack([pos_k, pos_v])
    pos = jnp.broadcast_to(jnp.concatenate([pos] * NSA_KV_HEADS, axis=-1)[:, :, None, :],
                           (2, CMP_LEN, 8, NSA_KV_HEADS * dh))
    return pos, w1, w2


def _nsa(proj, weights, l, tabs):
    tok_tabs, cmp_tabs = tabs
    qt, gt, ka, kw, vst, vsd, vwt = _nsa_prep(proj, tok_tabs)
    kc, vct = _nsa_compress(proj, *weights, l, cmp_tabs)
    return _nsa_attn(qt, gt, kc, vct, ka, vst, vsd, kw, vwt)


def _nsa_tables(s):
    nhalf = s // CMP_STRIDE
    tok_tabs = _nsa_rope_tables(jnp.arange(s))
    cmp_tabs = _nsa_rope_tables(jnp.arange(nhalf) * CMP_STRIDE + CMP_LEN - 1)
    return tok_tabs, cmp_tabs


def _out_kernel(x_ref, g_ref, yl_ref, ys_ref, yr_ref, yn_ref, wgm_ref, wb_ref, wo_ref,
                fg_ref, o_ref, *, final):
    x = x_ref[0]
    h = _rms(x, g_ref[...]).astype(MM)
    w = BRANCH_W
    d = D_MODEL
    m0 = N_BRANCH * w
    merged = jnp.zeros(x.shape, F32)
    for n, y_ref in enumerate((yl_ref, ys_ref, yr_ref, yn_ref)):
        gate = jax.nn.silu(_dot(h, wgm_ref[0, :, n * w:(n + 1) * w]))
        branch = _dot((y_ref[0] * gate).astype(MM), wb_ref[n])
        merged = merged + jax.nn.sigmoid(_dot(h, wgm_ref[0, :, m0 + n * d:m0 + (n + 1) * d])) * branch
    out = x + _dot(merged.astype(MM), wo_ref[...])
    if final:
        out = _rms(out, fg_ref[...])
    o_ref[0] = out


def _out(x, g, ys, wgm, l, wb, wo, fg, final):
    b, s, d = x.shape
    w = BRANCH_W
    tm = TM_PROJ
    row = lambda i, j: (i, j, 0)
    c2 = lambda i, j: (0, 0)
    once = pl.Buffered(1)
    return pl.pallas_call(
        functools.partial(_out_kernel, final=final),
        grid=(b, s // tm),
        in_specs=[pl.BlockSpec((1, tm, d), row),
                  pl.BlockSpec((1, d), c2)]
                 + [pl.BlockSpec((1, tm, w), row)] * N_BRANCH
                 + [pl.BlockSpec((1, d, N_BRANCH * (w + d)), lambda i, j: (l, 0, 0), pipeline_mode=once),
                    pl.BlockSpec((N_BRANCH, w, d), lambda i, j: (0, 0, 0), pipeline_mode=once),
                    pl.BlockSpec((d, d), c2, pipeline_mode=once),
                    pl.BlockSpec((1, d), c2)],
        out_specs=pl.BlockSpec((1, tm, d), row),
        out_shape=jax.ShapeDtypeStruct((b, s, d), F32),
        compiler_params=_cparams("parallel", "arbitrary"),
        name="gate_merge_out",
    )(x, g, *ys, wgm, wb, wo, fg)


def _split_w_in(w_in):
    return w_in[:, :, :PROJ_W].astype(MM), w_in[:, :, GATE_OFF:].astype(MM)


def _layer_tables(p):
    s5 = jax.vmap(_s5_tables)(p["s5_lambda_re"], p["s5_lambda_im"], p["s5_b_re"], p["s5_b_im"],
                              p["s5_c_re"], p["s5_c_im"], p["s5_log_dt"])
    nsa = jax.vmap(_nsa_weights)(p["nsa_pos_k"], p["nsa_pos_v"], p["nsa_ck_w1"], p["nsa_ck_w2"],
                                 p["nsa_cv_w1"], p["nsa_cv_w2"])
    wab = jax.vmap(lambda a, x: jnp.concatenate([_block_diag(a), _block_diag(x)], axis=1))(
        p["lru_wa"], p["lru_wx"]).astype(MM)
    return s5, nsa, wab


def _layer(x, p, l, w_mix, w_gm, tables, ret_tabs, nsa_tabs, final_g):
    s5_t, nsa_w, wab = tables
    g = p["norm_g"][l].reshape(1, D_MODEL)
    proj = _inproj(x, g, w_mix, l)

    bab = jnp.concatenate([p["lru_ba"][l], p["lru_bx"][l]]).reshape(1, 2 * BRANCH_W)
    y_lru = _lru(proj, p["lru_conv_w"][l], p["lru_conv_b"][l].reshape(1, -1), wab, l, bab,
                 p["lru_lambda"][l].reshape(1, -1))

    y_s5 = _s5(proj, *s5_t, l, p["s5_d"][l].reshape(1, -1), p["s5_w_glu"][l].astype(MM),
               p["s5_b_glu"][l].reshape(1, -1))

    y_ret = _retention(proj, ret_tabs)

    y_nsa = _nsa(proj, nsa_w, l, nsa_tabs)

    final = l == DEPTH - 1
    return _out(x, g, (y_lru, y_s5, y_ret, y_nsa), w_gm, l, p["w_branch"][l].astype(MM),
                p["w_out"][l].astype(MM), final_g.reshape(1, D_MODEL), final)


def kernel(x, norm_g, w_in, lru_conv_w, lru_conv_b, lru_wa, lru_ba, lru_wx, lru_bx, lru_lambda,
           s5_lambda_re, s5_lambda_im, s5_b_re, s5_b_im, s5_c_re, s5_c_im, s5_log_dt, s5_d,
           s5_w_glu, s5_b_glu, nsa_pos_k, nsa_pos_v, nsa_ck_w1, nsa_ck_w2, nsa_cv_w1, nsa_cv_w2,
           w_branch, w_out, final_norm_g):
    p = dict(norm_g=norm_g, w_in=w_in, lru_conv_w=lru_conv_w, lru_conv_b=lru_conv_b, lru_wa=lru_wa,
             lru_ba=lru_ba, lru_wx=lru_wx, lru_bx=lru_bx, lru_lambda=lru_lambda,
             s5_lambda_re=s5_lambda_re, s5_lambda_im=s5_lambda_im, s5_b_re=s5_b_re, s5_b_im=s5_b_im,
             s5_c_re=s5_c_re, s5_c_im=s5_c_im, s5_log_dt=s5_log_dt, s5_d=s5_d, s5_w_glu=s5_w_glu,
             s5_b_glu=s5_b_glu, nsa_pos_k=nsa_pos_k, nsa_pos_v=nsa_pos_v, nsa_ck_w1=nsa_ck_w1,
             nsa_ck_w2=nsa_ck_w2, nsa_cv_w1=nsa_cv_w1, nsa_cv_w2=nsa_cv_w2, w_branch=w_branch,
             w_out=w_out)
    s = x.shape[1]
    ret_tabs = _ret_tables(s)
    nsa_tabs = _nsa_tables(s)
    w_mix, w_gm = _split_w_in(w_in)
    tables = _layer_tables(p)
    for l in range(DEPTH):
        x = _layer(x, p, l, w_mix, w_gm, tables, ret_tabs, nsa_tabs, final_norm_g)
    return x
```

```python
import functools
import math

import jax
import jax.numpy as jnp
from jax import lax
from jax.experimental import pallas as pl
from jax.experimental.pallas import tpu as pltpu

F32 = jnp.float32
MM = jnp.bfloat16

D_MODEL = 1024
DEPTH = 4
N_BRANCH = 4
BRANCH_W = 512
NORM_EPS = 1e-6
LRU_HEADS = 8
LRU_HD = BRANCH_W // LRU_HEADS
CONV_W = 4
LRU_C = 8.0
S5_GROUP = 16
S5_GROUPS = BRANCH_W // S5_GROUP
S5_STATE = 64
S5_N = S5_GROUPS * S5_STATE
RET_HEADS = 4
RET_HD = BRANCH_W // RET_HEADS
RET_CHUNK = 128
RET_ROPE_BASE = 10000.0
NSA_HEADS = 8
NSA_KV_HEADS = 2
NSA_HD = BRANCH_W // NSA_HEADS
NSA_GQA = NSA_HEADS // NSA_KV_HEADS
CMP_LEN = 32
CMP_STRIDE = 16
CMP_HIDDEN = 256
SLC_BLOCK = 64
SLC_TOPK = 16
WIN = 512
Q_BLOCK = 128
ROPE_THETA = 500000.0
ROPE_DIM = NSA_HD // 4
FORCE_SCORE = 1e4
NEG = -1e30
NSA_KV_W = 6 * NSA_KV_HEADS * NSA_HD
IN_SIZES = (BRANCH_W, BRANCH_W, 3 * BRANCH_W, NSA_HEADS * NSA_HD, NSA_KV_W, 3 * NSA_HEADS,
            N_BRANCH * BRANCH_W, N_BRANCH * D_MODEL)
LANE = 128
MIX_W = sum(IN_SIZES[:5])
BG_OFF = MIX_W
GATE_OFF = BG_OFF + IN_SIZES[5]
MERGE_OFF = GATE_OFF + IN_SIZES[6]
PROJ_W = MIX_W + LANE
VMEM_LIMIT = 56 * 1024 * 1024

TM_PROJ = 512
T_LRU = 512
T_S5 = 512
T_RET = 512
S5_TILE_GROUP = 4
SEL_CHUNK = 512
NSA_T = SEL_CHUNK
V_ROWS = NSA_HD + 16


def _cparams(*sem):
    return pltpu.CompilerParams(dimension_semantics=sem, vmem_limit_bytes=VMEM_LIMIT)


def _rms(x, g):
    ms = jnp.mean(x * x, axis=-1, keepdims=True)
    return (x * lax.rsqrt(ms + NORM_EPS)) * g


def _dot(a, b):
    return jnp.dot(a, b, preferred_element_type=F32)


def _dot_nt(a, b):
    return lax.dot_general(a, b, (((1,), (1,)), ((), ())), preferred_element_type=F32)


def _dot_tn(a, b):
    return lax.dot_general(a, b, (((0,), (0,)), ((), ())), preferred_element_type=F32)


def _inproj_kernel(x_ref, g_ref, w_ref, o_ref):
    h = _rms(x_ref[0], g_ref[...])
    o_ref[0] = _dot(h.astype(MM), w_ref[0])


def _inproj(x, g, w, l):
    b, s, d = x.shape
    n = w.shape[2]
    return pl.pallas_call(
        _inproj_kernel,
        grid=(b, s // TM_PROJ),
        in_specs=[pl.BlockSpec((1, TM_PROJ, d), lambda i, j: (i, j, 0)),
                  pl.BlockSpec((1, d), lambda i, j: (0, 0)),
                  pl.BlockSpec((1, d, n), lambda i, j: (l, 0, 0))],
        out_specs=pl.BlockSpec((1, TM_PROJ, n), lambda i, j: (i, j, 0)),
        out_shape=jax.ShapeDtypeStruct((b, s, n), F32),
        compiler_params=_cparams("parallel", "arbitrary"),
        name="inproj",
    )(x, g, w)


def _lru_kernel(u_ref, cw_ref, cb_ref, wab_ref, bab_ref, lam_ref, o_ref, ubuf, hcar, la_ref, lb_ref):
    t = T_LRU
    w = BRANCH_W

    @pl.when(pl.program_id(1) == 0)
    def _():
        ubuf[0:8, :] = jnp.zeros((8, w), F32)
        hcar[...] = jnp.zeros((1, w), F32)

    ubuf[8:8 + t, :] = u_ref[0]
    xc = cb_ref[...] + cw_ref[0:1, :] * ubuf[5:5 + t, :]
    for k in range(1, CONV_W):
        xc = xc + cw_ref[k:k + 1, :] * ubuf[5 + k:5 + k + t, :]
    ubuf[0:8, :] = ubuf[t:t + 8, :]

    ri = _dot(xc.astype(MM), wab_ref[0]) + bab_ref[...]
    r = jax.nn.sigmoid(ri[:, :w])
    gi = jax.nn.sigmoid(ri[:, w:])
    z = -lam_ref[...]
    softplus = jnp.maximum(z, 0.0) + jnp.log1p(jnp.exp(-jnp.abs(z)))
    a = jnp.exp(-LRU_C * r * softplus)
    bt = jnp.sqrt(1.0 - a * a) * gi * xc

    sub = 8
    grp = t // sub
    i8 = lax.broadcasted_iota(jnp.int32, (sub, LANE), 0)
    rowg = lax.broadcasted_iota(jnp.int32, (grp, LANE), 0)
    last = pl.ds(sub - 1, grp, stride=sub)
    tiles = range(w // LANE)
    loc = []
    for cb in tiles:
        lanes = slice(cb * LANE, (cb + 1) * LANE)
        a3 = a[:, lanes].reshape(grp, sub, LANE)
        b3 = bt[:, lanes].reshape(grp, sub, LANE)
        k = 1
        while k < sub:
            keep = (i8 >= k)[None]
            a_sh = jnp.where(keep, pltpu.roll(a3, k, 1), 1.0)
            b_sh = jnp.where(keep, pltpu.roll(b3, k, 1), 0.0)
            b3 = a3 * b_sh + b3
            a3 = a3 * a_sh
            k *= 2
        loc.append((a3, b3))
        la_ref[cb] = a3.reshape(t, LANE)
        lb_ref[cb] = b3.reshape(t, LANE)
    ends = [(la_ref[cb, last, :], lb_ref[cb, last, :]) for cb in tiles]
    k = 1
    while k < grp:
        keep = rowg >= k
        ends = [(ae * jnp.where(keep, pltpu.roll(ae, k, 0), 1.0),
                 ae * jnp.where(keep, pltpu.roll(be, k, 0), 0.0) + be) for ae, be in ends]
        k *= 2
    hc = hcar[...]
    full = jnp.concatenate([be + ae * hc[:, cb * LANE:(cb + 1) * LANE] for cb, (ae, be) in zip(tiles, ends)], axis=1)
    hcar[...] = full[grp - 1:grp, :]
    row_w = lax.broadcasted_iota(jnp.int32, (grp, w), 0)
    e = jnp.where(row_w >= 1, pltpu.roll(full, 1, 0), hc)
    e_hi = e.astype(MM)
    r1 = e - e_hi.astype(F32)
    e_mid = r1.astype(MM)
    e_lo = (r1 - e_mid.astype(F32)).astype(MM)
    rep = lax.broadcasted_iota(jnp.int32, (t, grp), 0) // sub == lax.broadcasted_iota(jnp.int32, (t, grp), 1)
    rep = jnp.where(rep, 1.0, 0.0).astype(MM)
    z_in = _dot(jnp.concatenate([rep, rep, rep], axis=1), jnp.concatenate([e_hi, e_mid, e_lo], axis=0))
    for cb, (a3, b3) in zip(tiles, loc):
        lanes = slice(cb * LANE, (cb + 1) * LANE)
        o_ref[0, :, lanes] = b3.reshape(t, LANE) + a3.reshape(t, LANE) * z_in[:, lanes]


def _lru(proj, cw, cb, wab, l, bab, lam):
    b, s, _ = proj.shape
    w = BRANCH_W
    return pl.pallas_call(
        _lru_kernel,
        grid=(b, s // T_LRU),
        in_specs=[pl.BlockSpec((1, T_LRU, w), lambda i, j: (i, j, 0)),
                  pl.BlockSpec((CONV_W, w), lambda i, j: (0, 0)),
                  pl.BlockSpec((1, w), lambda i, j: (0, 0)),
                  pl.BlockSpec((1, w, 2 * w), lambda i, j: (l, 0, 0)),
                  pl.BlockSpec((1, 2 * w), lambda i, j: (0, 0)),
                  pl.BlockSpec((1, w), lambda i, j: (0, 0))],
        out_specs=pl.BlockSpec((1, T_LRU, w), lambda i, j: (i, j, 0)),
        out_shape=jax.ShapeDtypeStruct((b, s, w), F32),
        scratch_shapes=[pltpu.VMEM((T_LRU + 8, w), F32), pltpu.VMEM((1, w), F32),
                        pltpu.VMEM((w // LANE, T_LRU, LANE), F32), pltpu.VMEM((w // LANE, T_LRU, LANE), F32)],
        compiler_params=_cparams("parallel", "arbitrary"),
        name="rg_lru",
    )(proj, cw, cb, wab, bab, lam)


def _block_diag(blocks):
    n, r, c = blocks.shape
    eye = jnp.eye(n, dtype=blocks.dtype)
    return jnp.einsum("nrc,nm->nrmc", blocks, eye).reshape(n * r, n * c)


def _cmul_add(xr, xi, ar, ai, br, bi):
    return xr + (ar * br - ai * bi), xi + (ar * bi + ai * br)


def _s5_kernel(u_ref, wb_ref, apr_ref, api_ref, wc_ref, d_ref, wg_ref, bg_ref, o_ref,
               car, lr_ref, li_ref, y_ref):
    t = T_S5
    n = S5_N
    w = BRANCH_W
    sub = 8
    grp = t // sub
    mxu = 256

    @pl.when(pl.program_id(1) == 0)
    def _():
        car[...] = jnp.zeros((2, n), F32)

    u = u_ref[0]
    ub = u.astype(MM)
    y_ref[...] = d_ref[...] * u
    first = lax.broadcasted_iota(jnp.int32, (t, LANE), 0) == 0
    i8 = lax.broadcasted_iota(jnp.int32, (sub, LANE), 0)
    rowg = lax.broadcasted_iota(jnp.int32, (grp, LANE), 0)
    rep = lax.broadcasted_iota(jnp.int32, (t, grp), 0) // sub == lax.broadcasted_iota(jnp.int32, (t, grp), 1)
    rep = jnp.where(rep, 1.0, 0.0).astype(MM)
    rep3 = jnp.concatenate([rep, rep, rep], axis=1)
    last = pl.ds(sub - 1, grp, stride=sub)

    ntile = n // LANE
    for c0 in range(0, ntile, S5_TILE_GROUP):
        tiles = range(c0, c0 + S5_TILE_GROUP)
        lanes = [slice(cb * LANE, (cb + 1) * LANE) for cb in tiles]
        ios = [slice(io * mxu, (io + 1) * mxu) for io in ((cb * LANE // S5_STATE * S5_GROUP) // mxu for cb in tiles)]
        aprs = [apr_ref[0, :, ln] for ln in lanes]
        apis = [api_ref[0, :, ln] for ln in lanes]
        xs = [_dot(ub[:, io], wb_ref[0, cb]) for cb, io in zip(tiles, ios)]
        loc = []
        for x, apr, api, ln in zip(xs, aprs, apis, lanes):
            cr, ci = _cmul_add(0.0, 0.0, apr[0:1], api[0:1], car[0:1, ln], car[1:2, ln])
            xr = (x[:, :LANE] + jnp.where(first, cr, 0.0)).reshape(grp, sub, LANE)
            xi = (x[:, LANE:] + jnp.where(first, ci, 0.0)).reshape(grp, sub, LANE)
            k = 1
            while k < sub:
                ar = jnp.where(i8 >= k, apr[k - 1:k], 0.0)[None]
                ai = jnp.where(i8 >= k, api[k - 1:k], 0.0)[None]
                xr, xi = _cmul_add(xr, xi, ar, ai, pltpu.roll(xr, k, 1), pltpu.roll(xi, k, 1))
                k *= 2
            loc.append((xr, xi))
        for j, (xr, xi) in enumerate(loc):
            lr_ref[j] = xr.reshape(t, LANE)
            li_ref[j] = xi.reshape(t, LANE)
        ends = [(lr_ref[j, last, :], li_ref[j, last, :]) for j in range(len(loc))]
        k = 1
        step = sub - 1
        while k < grp:
            keep = rowg >= k
            nxt = []
            for (fr, fi), apr, api in zip(ends, aprs, apis):
                sr = jnp.where(keep, pltpu.roll(fr, k, 0), 0.0)
                si = jnp.where(keep, pltpu.roll(fi, k, 0), 0.0)
                nxt.append(_cmul_add(fr, fi, apr[step:step + 1], api[step:step + 1], sr, si))
            ends = nxt
            k *= 2
            step += 1
        enter = rowg >= 1
        zs = []
        for (fr, fi), ln in zip(ends, lanes):
            car[0:1, ln] = fr[grp - 1:grp, :]
            car[1:2, ln] = fi[grp - 1:grp, :]
            e = jnp.concatenate([jnp.where(enter, pltpu.roll(fr, 1, 0), 0.0),
                                 jnp.where(enter, pltpu.roll(fi, 1, 0), 0.0)], axis=1)
            e_hi = e.astype(MM)
            r1 = e - e_hi.astype(F32)
            e_mid = r1.astype(MM)
            e_lo = (r1 - e_mid.astype(F32)).astype(MM)
            zs.append(_dot(rep3, jnp.concatenate([e_hi, e_mid, e_lo], axis=0)))
        for cb, io, (xr, xi), z, apr, api in zip(tiles, ios, loc, zs, aprs, apis):
            hr, hi = _cmul_add(xr, xi, apr[0:sub][None], api[0:sub][None],
                               z[:, :LANE].reshape(grp, sub, LANE), z[:, LANE:].reshape(grp, sub, LANE))
            h = jnp.concatenate([hr.reshape(t, LANE), hi.reshape(t, LANE)], axis=1).astype(MM)
            y_ref[:, io] += _dot(h, wc_ref[0, cb])

    z = jax.nn.gelu(y_ref[...])
    g = _dot(z.astype(MM), wg_ref[...]) + bg_ref[...]
    o_ref[0] = g[:, :w] * jax.nn.sigmoid(g[:, w:])


def _s5_tables(lam_re, lam_im, b_re, b_im, c_re, c_im, log_dt):
    dt = jnp.exp(log_dt)[:, None]
    lr = jnp.minimum(lam_re, -1e-4)
    li = lam_im
    mag = jnp.exp(lr * dt)
    ab_re = mag * jnp.cos(li * dt)
    ab_im = mag * jnp.sin(li * dt)
    den = lr * lr + li * li
    nr = ab_re - 1.0
    ni = ab_im
    f_re = (nr * lr + ni * li) / den
    f_im = (ni * lr - nr * li) / den
    bb_re = f_re[..., None] * b_re - f_im[..., None] * b_im
    bb_im = f_re[..., None] * b_im + f_im[..., None] * b_re
    ntile = S5_N // LANE
    gpt = LANE // S5_STATE
    rgs = 256 // S5_GROUP
    cb = jnp.arange(ntile)
    g = jnp.arange(S5_GROUPS)
    in_tile = (g[None, None, :] == gpt * cb[:, None, None] + jnp.arange(gpt)[None, :, None]).astype(F32)
    io = (cb * gpt) // rgs
    at_row = (g[None, :, None] - rgs * io[:, None, None] == jnp.arange(rgs)[None, None, :]).astype(F32)
    tiles_b = lambda bb: jnp.einsum("kjg,kgr,gpc->krcjp", in_tile, at_row, bb).reshape(ntile, 256, LANE)
    tiles_c = lambda cc: jnp.einsum("kjg,kgr,gcp->kjprc", in_tile, at_row, cc).reshape(ntile, LANE, 256)
    wb = jnp.concatenate([tiles_b(bb_re), tiles_b(bb_im)], axis=2)
    wc = jnp.concatenate([tiles_c(c_re), tiles_c(-c_im)], axis=1)
    a_re = ab_re.reshape(1, S5_N)
    a_im = ab_im.reshape(1, S5_N)
    pr, pi = [a_re], [a_im]
    for _ in range(7):
        pr.append(pr[-1] * a_re - pi[-1] * a_im)
        pi.append(pr[-2] * a_im + pi[-1] * a_re)
    for _ in range(int(math.log2(T_S5 // 8)) - 1):
        r0, i0 = pr[-1], pi[-1]
        pr.append(r0 * r0 - i0 * i0)
        pi.append(2.0 * r0 * i0)
    pad = [jnp.zeros_like(a_re)] * (16 - len(pr))
    return wb.astype(MM), jnp.concatenate(pr + pad, 0), jnp.concatenate(pi + pad, 0), wc.astype(MM)


def _s5(proj, wb, apr, api, wc, l, d, wg, bg):
    b, s, _ = proj.shape
    w = BRANCH_W
    n = S5_N
    c2 = lambda i, j: (0, 0)
    layer = lambda a: pl.BlockSpec((1,) + a.shape[1:], lambda i, j: (l,) + (0,) * (a.ndim - 1))
    return pl.pallas_call(
        _s5_kernel,
        grid=(b, s // T_S5),
        in_specs=[pl.BlockSpec((1, T_S5, w), lambda i, j: (i, j, 1)),
                  layer(wb), layer(apr), layer(api), layer(wc),
                  pl.BlockSpec((1, w), c2),
                  pl.BlockSpec((w, 2 * w), c2),
                  pl.BlockSpec((1, 2 * w), c2)],
        out_specs=pl.BlockSpec((1, T_S5, w), lambda i, j: (i, j, 0)),
        out_shape=jax.ShapeDtypeStruct((b, s, w), F32),
        scratch_shapes=[pltpu.VMEM((2, n), F32), pltpu.VMEM((S5_TILE_GROUP, T_S5, LANE), F32),
                        pltpu.VMEM((S5_TILE_GROUP, T_S5, LANE), F32), pltpu.VMEM((T_S5, w), F32)],
        compiler_params=_cparams("parallel", "arbitrary"),
        name="s5",
    )(proj, wb, apr, api, wc, d, wg, bg)


def _ret_kernel(q_ref, k_ref, v_ref, cos_ref, sin_ref, dm_ref, xi_ref, zt_ref, cd_ref, o_ref, rst):
    c = RET_CHUNK
    dh = RET_HD

    @pl.when(pl.program_id(1) == 0)
    def _():
        rst[...] = jnp.zeros((RET_HEADS, dh, dh), F32)

    for ci in range(T_RET // c):
        rows = slice(ci * c, (ci + 1) * c)
        cos = cos_ref[rows, :]
        sin = sin_ref[rows, :]
        heads = range(RET_HEADS)
        cols = [slice(h * dh, (h + 1) * dh) for h in heads]
        qs, ks, vbs = [], [], []
        for h in heads:
            q = q_ref[0, rows, cols[h]]
            k = k_ref[0, rows, cols[h]]
            qs.append(q * cos + pltpu.roll(q, dh // 2, 1) * sin)
            ks.append((k * cos + pltpu.roll(k, dh // 2, 1) * sin) * (dh ** -0.5))
            vbs.append(v_ref[0, rows, cols[h]].astype(MM))
        scores = [_dot_nt(qs[h].astype(MM), ks[h].astype(MM)) * dm_ref[h] for h in heads]
        r_prev = [rst[h] for h in heads]
        cross = [_dot((qs[h] * xi_ref[h]).astype(MM), r_prev[h].astype(MM)) for h in heads]
        kv = [_dot_tn((ks[h] * zt_ref[h]).astype(MM), vbs[h]) for h in heads]
        inner = [_dot(scores[h].astype(MM), vbs[h]) for h in heads]
        for h in heads:
            rst[h] = cd_ref[h] * r_prev[h] + kv[h]
            o = inner[h] + cross[h]
            mu = jnp.mean(o, axis=-1, keepdims=True)
            var = jnp.mean(jnp.square(o - mu), axis=-1, keepdims=True)
            o_ref[0, rows, cols[h]] = (o - mu) * lax.rsqrt(var + 1e-5)


def _ret_tables(s):
    dh = RET_HD
    c = RET_CHUNK
    half = dh // 2
    inv = RET_ROPE_BASE ** (-jnp.arange(half, dtype=F32) * 2.0 / dh)
    ang = jnp.arange(s).astype(F32)[:, None] * inv[None, :]
    cos = jnp.cos(ang)
    sin = jnp.sin(ang)
    cosf = jnp.concatenate([cos, cos], axis=1)
    sinf = jnp.concatenate([-sin, sin], axis=1)
    gamma = 1.0 - jnp.exp(jnp.linspace(math.log(1.0 / 32.0), math.log(1.0 / 512.0), RET_HEADS, dtype=F32))
    log_g = jnp.log(gamma)
    n = jnp.arange(c, dtype=F32)
    diff = n[:, None] - n[None, :]
    dmask = jnp.where(diff[None] >= 0, jnp.exp(jnp.maximum(diff, 0.0)[None] * log_g[:, None, None]), 0.0)
    xi = jnp.exp((n[None, :] + 1.0) * log_g[:, None])
    zeta = jnp.exp((c - 1.0 - n)[None, :] * log_g[:, None])
    cdec = jnp.exp(c * log_g)
    xi_b = jnp.broadcast_to(xi[:, :, None], (RET_HEADS, c, dh))
    zt_b = jnp.broadcast_to(zeta[:, :, None], (RET_HEADS, c, dh))
    cd_b = jnp.broadcast_to(cdec[:, None, None], (RET_HEADS, dh, dh))
    return cosf, sinf, dmask, xi_b, zt_b, cd_b


def _retention(proj, tabs):
    b, s, _ = proj.shape
    w = BRANCH_W
    cosf, sinf, dmask, xi_b, zt_b, cd_b = tabs
    c = RET_CHUNK
    dh = RET_HD
    full3 = lambda i, j: (0, 0, 0)
    return pl.pallas_call(
        _ret_kernel,
        grid=(b, s // T_RET),
        in_specs=[pl.BlockSpec((1, T_RET, w), lambda i, j: (i, j, 2)),
                  pl.BlockSpec((1, T_RET, w), lambda i, j: (i, j, 3)),
                  pl.BlockSpec((1, T_RET, w), lambda i, j: (i, j, 4)),
                  pl.BlockSpec((T_RET, dh), lambda i, j: (j, 0)),
                  pl.BlockSpec((T_RET, dh), lambda i, j: (j, 0)),
                  pl.BlockSpec((RET_HEADS, c, c), full3),
                  pl.BlockSpec((RET_HEADS, c, dh), full3),
                  pl.BlockSpec((RET_HEADS, c, dh), full3),
                  pl.BlockSpec((RET_HEADS, dh, dh), full3)],
        out_specs=pl.BlockSpec((1, T_RET, w), lambda i, j: (i, j, 0)),
        out_shape=jax.ShapeDtypeStruct((b, s, w), F32),
        scratch_shapes=[pltpu.VMEM((RET_HEADS, dh, dh), F32)],
        compiler_params=_cparams("parallel", "arbitrary"),
        name="retention",
    )(proj, proj, proj, cosf, sinf, dmask, xi_b, zt_b, cd_b)


def _rope_lanes(x, cos, s_up, s_dn):
    half = ROPE_DIM // 2
    n = x.shape[-1]
    return x * cos + pltpu.roll(x, half, 1) * s_up + pltpu.roll(x, n - half, 1) * s_dn


def _nsa_prep_kernel(q_ref, ks_ref, vs_ref, kw_ref, vw_ref, bg_ref, cos_ref, up_ref, dn_ref,
                     qt_ref, gt_ref, ka_ref, kwo_ref, vst_ref, vsd_ref, vwt_ref):
    t = NSA_T
    qb = Q_BLOCK
    dh = NSA_HD
    cos = cos_ref[...]
    up = up_ref[...]
    dn = dn_ref[...]
    scale = NSA_HD ** -0.5 * math.log2(math.e)
    lane = lax.broadcasted_iota(jnp.int32, (t, LANE), 1)
    low = lane < dh
    tok = pl.program_id(1) * t + lax.broadcasted_iota(jnp.int32, (t, LANE), 0)
    onehot = jnp.where(tok // SLC_BLOCK == lane, 1.0, 0.0).astype(MM)
    ks = _rope_lanes(ks_ref[0], cos, up, dn)
    kw = _rope_lanes(kw_ref[0], cos, up, dn)
    for h in range(NSA_KV_HEADS):
        ks_h = ks if h == 0 else pltpu.roll(ks, LANE - h * dh, 1)
        kw_h = kw if h == 0 else pltpu.roll(kw, LANE - h * dh, 1)
        ka_ref[0, h, :, 0:LANE] = onehot
        ka_ref[0, h, :, LANE:2 * LANE] = jnp.where(low, ks_h, 0.0).astype(MM)
        kwo_ref[0, h] = jnp.where(low, kw_h, 0.0).astype(MM)

    def ones_row(n):
        return jnp.where(lax.broadcasted_iota(jnp.int32, (V_ROWS - dh, n), 0) == 0, 1.0, 0.0).astype(MM)

    r8 = lax.broadcasted_iota(jnp.int32, (8, NSA_GQA * qb), 0)
    for i in range(t // qb):
        rows = slice(i * qb, (i + 1) * qb)
        vs_t = vs_ref[0, rows, :].T
        vw_t = vw_ref[0, rows, :].T
        bg_t = bg_ref[0, rows, :].T
        for h in range(NSA_KV_HEADS):
            vst_ref[0, h, 0, 0:dh, rows] = vs_t[h * dh:(h + 1) * dh].astype(MM)
            vsd_ref[0, h, i, 0:dh, :] = vs_t[h * dh:(h + 1) * dh].astype(MM)
            vsd_ref[0, h, i, dh:V_ROWS, :] = ones_row(qb)
            vwt_ref[0, h, i, 0:dh, :] = vw_t[h * dh:(h + 1) * dh].astype(MM)
            vwt_ref[0, h, i, dh:V_ROWS, :] = ones_row(qb)
            qt_ref[0, h, i, dh:LANE, :] = jnp.zeros((LANE - dh, NSA_GQA * qb), F32)
            gates = jnp.zeros((8, NSA_GQA * qb), F32)
            for br in range(3):
                base = br * NSA_HEADS + h * NSA_GQA
                row = jnp.concatenate([bg_t[base + g:base + g + 1] for g in range(NSA_GQA)], axis=1)
                gates = jnp.where(r8 == br, row, gates)
            gt_ref[0, h, i] = gates
        for cb in range(BRANCH_W // LANE):
            cols = slice(cb * LANE, (cb + 1) * LANE)
            x_t = (_rope_lanes(q_ref[0, rows, cols], cos[rows], up[rows], dn[rows]) * scale).T
            h = (2 * cb) // NSA_GQA
            g0 = (2 * cb) % NSA_GQA
            qt_ref[0, h, i, 0:dh, g0 * qb:(g0 + 1) * qb] = x_t[0:dh]
            qt_ref[0, h, i, 0:dh, (g0 + 1) * qb:(g0 + 2) * qb] = x_t[dh:2 * dh]
    for h in range(NSA_KV_HEADS):
        vst_ref[0, h, 0, dh:V_ROWS, :] = ones_row(t)


def _nsa_rope_tables(pos):
    half = ROPE_DIM // 2
    inv = ROPE_THETA ** (-jnp.arange(half, dtype=F32) * 2.0 / ROPE_DIM)
    ang = pos.astype(F32)[:, None] * inv[None, :]
    cos = jnp.cos(ang)
    sin = jnp.sin(ang)
    n = pos.shape[0]
    zeros = jnp.zeros((n, NSA_HD - ROPE_DIM), F32)
    zh = jnp.zeros((n, half), F32)
    cos_h = jnp.concatenate([cos, cos, jnp.ones((n, NSA_HD - ROPE_DIM), F32)], axis=1)
    up_h = jnp.concatenate([zh, sin, zeros], axis=1)
    dn_h = jnp.concatenate([-sin, zh, zeros], axis=1)
    tile2 = lambda a: jnp.concatenate([a, a], axis=1)
    return tile2(cos_h), tile2(up_h), tile2(dn_h)


def _nsa_prep(proj, tabs):
    b, s, _ = proj.shape
    t = NSA_T
    qb = Q_BLOCK
    hkv = NSA_KV_HEADS
    cols = NSA_GQA * qb
    assert t == SEL_CHUNK and s // SLC_BLOCK <= LANE
    q_blk = sum(IN_SIZES[:3]) // BRANCH_W
    kv_blk = sum(IN_SIZES[:4]) // LANE
    lane_blk = lambda n: pl.BlockSpec((1, t, LANE), lambda i, j: (i, j, n))
    tab_spec = pl.BlockSpec((t, LANE), lambda i, j: (j, 0))
    tile4 = lambda i, j: (i, 0, j, 0)
    tile5 = lambda i, j: (i, 0, j, 0, 0)
    return pl.pallas_call(
        _nsa_prep_kernel,
        grid=(b, s // t),
        in_specs=[pl.BlockSpec((1, t, BRANCH_W), lambda i, j: (i, j, q_blk)),
                  lane_blk(kv_blk + 2), lane_blk(kv_blk + 3), lane_blk(kv_blk + 4), lane_blk(kv_blk + 5),
                  lane_blk(BG_OFF // LANE), tab_spec, tab_spec, tab_spec],
        out_specs=[pl.BlockSpec((1, hkv, t // qb, LANE, cols), tile5),
                   pl.BlockSpec((1, hkv, t // qb, 8, cols), tile5),
                   pl.BlockSpec((1, hkv, t, 2 * LANE), tile4),
                   pl.BlockSpec((1, hkv, t, LANE), tile4),
                   pl.BlockSpec((1, hkv, 1, V_ROWS, t), tile5),
                   pl.BlockSpec((1, hkv, t // qb, V_ROWS, qb), tile5),
                   pl.BlockSpec((1, hkv, t // qb, V_ROWS, qb), tile5)],
        out_shape=[jax.ShapeDtypeStruct((b, hkv, s // qb, LANE, cols), F32),
                   jax.ShapeDtypeStruct((b, hkv, s // qb, 8, cols), F32),
                   jax.ShapeDtypeStruct((b, hkv, s, 2 * LANE), MM),
                   jax.ShapeDtypeStruct((b, hkv, s, LANE), MM),
                   jax.ShapeDtypeStruct((b, hkv, s // t, V_ROWS, t), MM),
                   jax.ShapeDtypeStruct((b, hkv, s // qb, V_ROWS, qb), MM),
                   jax.ShapeDtypeStruct((b, hkv, s // qb, V_ROWS, qb), MM)],
        compiler_params=_cparams("parallel", "parallel"),
        name="nsa_prep",
    )(proj, proj, proj, proj, proj, proj, *tabs)


def _cmp_kernel(xk_ref, xv_ref, pos_ref, w1_ref, w2_ref, cos_ref, up_ref, dn_ref, kc_ref, vct_ref):
    nh = kc_ref.shape[2]
    hid_w = NSA_KV_HEADS * CMP_HIDDEN
    for kind, x_ref in enumerate((xk_ref, xv_ref)):
        lo = jnp.zeros((nh, hid_w), F32)
        hi = jnp.zeros((nh, hid_w), F32)
        posb = jnp.zeros((8, hid_w), F32)
        for j in range(CMP_STRIDE):
            xj = x_ref[0, pl.ds(j, nh, stride=CMP_STRIDE), :].astype(MM)
            lo = lo + _dot(xj, w1_ref[0, kind, j])
            hi = hi + _dot(xj, w1_ref[0, kind, CMP_STRIDE + j])
        for j in range(CMP_LEN):
            posb = posb + _dot(pos_ref[0, kind, j].astype(MM), w1_ref[0, kind, j])
        hid = jax.nn.gelu(lo + pltpu.roll(hi, nh - 1, 0) + posb[0:1, :])
        for h in range(NSA_KV_HEADS):
            out = _dot(hid[:, h * CMP_HIDDEN:(h + 1) * CMP_HIDDEN].astype(MM), w2_ref[0, kind])
            if kind == 0:
                kc_ref[0, h] = _rope_lanes(out, cos_ref[...], up_ref[...], dn_ref[...])
            else:
                vct_ref[0, h] = out.T.astype(MM)


def _nsa_compress(proj, pos, w1, w2, l, tabs):
    b, s, _ = proj.shape
    nh = s // CMP_STRIDE
    hkv = NSA_KV_HEADS
    kv_blk = sum(IN_SIZES[:4]) // LANE
    once = pl.Buffered(1)
    c2 = lambda i: (0, 0)
    layer = lambda a, **kw: pl.BlockSpec((1,) + a.shape[1:], lambda i: (l,) + (0,) * (a.ndim - 1), **kw)
    return pl.pallas_call(
        _cmp_kernel,
        grid=(b,),
        in_specs=[pl.BlockSpec((1, s, LANE), lambda i: (i, 0, kv_blk)),
                  pl.BlockSpec((1, s, LANE), lambda i: (i, 0, kv_blk + 1)),
                  layer(pos), layer(w1, pipeline_mode=once), layer(w2),
                  pl.BlockSpec((nh, LANE), c2), pl.BlockSpec((nh, LANE), c2), pl.BlockSpec((nh, LANE), c2)],
        out_specs=[pl.BlockSpec((1, hkv, nh, LANE), lambda i: (i, 0, 0, 0)),
                   pl.BlockSpec((1, hkv, LANE, nh), lambda i: (i, 0, 0, 0))],
        out_shape=[jax.ShapeDtypeStruct((b, hkv, nh, LANE), F32),
                   jax.ShapeDtypeStruct((b, hkv, LANE, nh), MM)],
        compiler_params=_cparams("parallel"),
        name="nsa_compress",
    )(proj, proj, pos, w1, w2, *tabs)


def _softmax_update(s, vt, m, acc):
    m_new = jnp.maximum(m, jnp.max(s, axis=0, keepdims=True))
    alpha = jnp.exp2(m - m_new)
    p = jnp.exp2(s - m_new)
    acc = alpha * acc + _dot(vt, p.astype(MM))
    return m_new, acc


def _normalise(acc):
    return acc[:NSA_HD, :] * (1.0 / acc[NSA_HD:NSA_HD + 1, :])


def _nsa_attn_kernel(q_ref, gt_ref, kc_ref, vc_ref, ka_ref, vs_ref, vd_ref, kw_ref, vw_ref, o_ref,
                     ps_ref, sa_ref, sb_ref):
    blk = pl.program_id(2)
    qb = Q_BLOCK
    cols = NSA_GQA * qb
    ncmp_pad = kc_ref.shape[2]
    nsel = LANE
    qf = q_ref[0, 0, 0]
    qm = qf.astype(MM)
    t = blk * qb + (lax.broadcasted_iota(jnp.int32, (1, cols), 1) & (qb - 1))
    lower = lax.broadcasted_iota(jnp.int32, (qb, 1), 0) <= (t & (qb - 1))

    kb = SEL_CHUNK

    s_c = _dot(kc_ref[0, 0], qf)
    n_id = lax.broadcasted_iota(jnp.int32, (ncmp_pad, 1), 0)
    s_c = jnp.where(n_id * CMP_STRIDE + (CMP_LEN - 1) <= t, s_c, NEG)
    e_c = jnp.exp2(s_c - jnp.max(s_c, axis=0, keepdims=True))
    den = jnp.sum(e_c, axis=0, keepdims=True)
    p_c = e_c * jnp.where(t >= CMP_LEN - 1, 1.0 / den, 0.0)
    o_c = _dot(vc_ref[0, 0], p_c.astype(MM))[:NSA_HD, :]

    r = SLC_BLOCK // CMP_STRIDE
    nrow = ncmp_pad // r
    ps_ref[...] = p_c[:, 0:qb] + p_c[:, qb:2 * qb] + p_c[:, 2 * qb:3 * qb] + p_c[:, 3 * qb:4 * qb]
    imp = ps_ref[pl.ds(0, nrow, stride=r), :]
    for k in range(1, r - 1):
        imp = imp + ps_ref[pl.ds(k, nrow, stride=r), :]
    edge = 0.5 * ps_ref[pl.ds(r - 1, nrow, stride=r), :]
    first = lax.broadcasted_iota(jnp.int32, (nrow, qb), 0) == 0
    imp = imp + edge + jnp.where(first, 0.0, pltpu.roll(edge, 1, 0))
    if nrow < nsel:
        imp = jnp.concatenate([imp, jnp.zeros((nsel - nrow, qb), F32)], axis=0)

    cur = (blk * qb + lax.broadcasted_iota(jnp.int32, (1, qb), 1)) // SLC_BLOCK
    jid = lax.broadcasted_iota(jnp.int32, (nsel, qb), 0)
    valid = jid <= cur
    forced = (jid == 0) | (jid == cur) | (jid == cur - 1)
    score0 = jnp.where(valid, jnp.where(forced, FORCE_SCORE, imp), -1.0)
    score = jnp.where(forced & valid, -3e38, score0)
    for _ in range(SLC_TOPK - 3):
        mx = jnp.max(score, axis=0, keepdims=True)
        idx = jnp.min(jnp.where(score == mx, jid, nsel), axis=0, keepdims=True)
        score = jnp.where(jid == idx, -3e38, score)
    own = 2 * blk * (qb // (2 * SLC_BLOCK))
    bias = jnp.where((score < -1e38) & (score0 >= 0.0) & (jid < own), 0.0, NEG).astype(MM)
    s_w = []
    v_w = []
    for back in range(WIN // qb, -1, -1):
        c = blk - back
        cc = jnp.maximum(c, 0)
        off = pl.multiple_of(cc * qb, qb)
        s = _dot(kw_ref[0, 0, pl.ds(off, qb), :], qm)
        if back == WIN // qb:
            s = jnp.where(lower | (c < 0), NEG, s)
        elif back == 0:
            s = jnp.where(lower, s, NEG)
        else:
            s = jnp.where(c < 0, NEG, s)
        s_w.append(s)
        v_w.append(vw_ref[0, 0, cc])
    s_w = jnp.concatenate(s_w, axis=0)
    p_w = jnp.exp2(s_w - jnp.max(s_w, axis=0, keepdims=True))
    o_w = _normalise(_dot(jnp.concatenate(v_w, axis=1), p_w.astype(MM)))

    s_d = _dot(ka_ref[0, 0, pl.ds(pl.multiple_of(blk * qb, qb), qb), LANE:2 * LANE], qm)
    init = _softmax_update(jnp.where(lower, s_d, NEG), vd_ref[0, 0, blk],
                           jnp.full((1, cols), NEG, F32), jnp.zeros((V_ROWS, cols), F32))
    q_aug = jnp.concatenate([jnp.concatenate([bias] * NSA_GQA, axis=1), qm], axis=0)

    nchunk = ka_ref.shape[2] // kb
    diag = blk * qb

    def scores_to(s_ref, j):
        off = pl.multiple_of(jnp.minimum(j, nchunk - 1) * kb, kb)
        s_ref[...] = _dot(ka_ref[0, 0, pl.ds(off, kb), :], q_aug)

    def update_from(s_ref, j, m, acc):
        return _softmax_update(s_ref[...], vs_ref[0, 0, jnp.minimum(j, nchunk - 1)], m, acc)

    def pair_body(i, carry):
        scores_to(sb_ref, 2 * i + 1)
        carry = update_from(sa_ref, 2 * i, *carry)
        scores_to(sa_ref, 2 * i + 2)
        return update_from(sb_ref, 2 * i + 1, *carry)

    nneed = (diag + kb - 1) // kb
    scores_to(sa_ref, 0)
    carry = lax.fori_loop(0, nneed // 2, pair_body, init)
    carry = lax.cond(nneed % 2 == 1, lambda c: update_from(sa_ref, nneed - 1, *c), lambda c: c, carry)
    o_s = _normalise(carry[1])

    g = jax.nn.sigmoid(gt_ref[0, 0, 0])
    o = g[0:1, :] * o_c + g[1:2, :] * o_s + g[2:3, :] * o_w
    for gp in range(NSA_GQA // 2):
        pair = jnp.concatenate([o[:, 2 * gp * qb:(2 * gp + 1) * qb],
                                o[:, (2 * gp + 1) * qb:(2 * gp + 2) * qb]], axis=0)
        o_ref[0, :, gp * LANE:(gp + 1) * LANE] = pair.T


def _nsa_attn(qt, gt, kc, vct, ka, vst, vsd, kw, vwt):
    b, hkv, nqb, _, cols = qt.shape
    s = ka.shape[2]
    ncmp_pad = kc.shape[2]
    assert CMP_LEN == 2 * CMP_STRIDE and SLC_BLOCK % CMP_STRIDE == 0 and s // SLC_BLOCK <= LANE
    assert (s // SEL_CHUNK) % 2 == 0
    per_head = lambda i, h, j: (i, h, 0, 0)
    per_head5 = lambda i, h, j: (i, h, 0, 0, 0)
    per_blk = lambda i, h, j: (i, h, j, 0, 0)
    return pl.pallas_call(
        _nsa_attn_kernel,
        grid=(b, hkv, nqb),
        in_specs=[pl.BlockSpec((1, 1, 1, LANE, cols), per_blk),
                  pl.BlockSpec((1, 1, 1, 8, cols), per_blk),
                  pl.BlockSpec((1, 1, ncmp_pad, LANE), per_head),
                  pl.BlockSpec((1, 1, LANE, ncmp_pad), per_head),
                  pl.BlockSpec((1, 1, s, 2 * LANE), per_head),
                  pl.BlockSpec((1, 1, s // SEL_CHUNK, V_ROWS, SEL_CHUNK), per_head5),
                  pl.BlockSpec((1, 1, s // Q_BLOCK, V_ROWS, Q_BLOCK), per_head5),
                  pl.BlockSpec((1, 1, s, LANE), per_head),
                  pl.BlockSpec((1, 1, s // Q_BLOCK, V_ROWS, Q_BLOCK), per_head5)],
        out_specs=pl.BlockSpec((1, Q_BLOCK, NSA_GQA * NSA_HD), lambda i, h, j: (i, j, h)),
        out_shape=jax.ShapeDtypeStruct((b, s, hkv * NSA_GQA * NSA_HD), F32),
        scratch_shapes=[pltpu.VMEM((ncmp_pad, Q_BLOCK), F32),
                        pltpu.VMEM((SEL_CHUNK, cols), F32), pltpu.VMEM((SEL_CHUNK, cols), F32)],
        compiler_params=_cparams("parallel", "parallel", "arbitrary"),
        name="nsa_attn",
    )(qt, gt, kc, vct, ka, vst, vsd, kw, vwt)


def _nsa_weights(pos_k, pos_v, ck_w1, ck_w2, cv_w1, cv_w2):
    dh = NSA_HD
    eye = jnp.eye(NSA_KV_HEADS, dtype=F32)
    slabs = lambda w: jnp.einsum("jrc,hg->jhrgc", w.reshape(CMP_LEN, dh, CMP_HIDDEN), eye).reshape(
        CMP_LEN, NSA_KV_HEADS * dh, NSA_KV_HEADS * CMP_HIDDEN)
    w1 = jnp.stack([slabs(ck_w1), slabs(cv_w1)]).astype(MM)
    w2 = jnp.pad(jnp.stack([ck_w2, cv_w2]), ((0, 0), (0, 0), (0, LANE - dh))).astype(MM)
    pos = jnp.stack([pos_k, pos_v])
    pos = jnp.broadcast_to(jnp.concatenate([pos] * NSA_KV_HEADS, axis=-1)[:, :, None, :],
                           (2, CMP_LEN, 8, NSA_KV_HEADS * dh))
    return pos, w1, w2


def _nsa(proj, weights, l, tabs):
    tok_tabs, cmp_tabs = tabs
    qt, gt, ka, kw, vst, vsd, vwt = _nsa_prep(proj, tok_tabs)
    kc, vct = _nsa_compress(proj, *weights, l, cmp_tabs)
    return _nsa_attn(qt, gt, kc, vct, ka, vst, vsd, kw, vwt)


def _nsa_tables(s):
    nhalf = s // CMP_STRIDE
    tok_tabs = _nsa_rope_tables(jnp.arange(s))
    cmp_tabs = _nsa_rope_tables(jnp.arange(nhalf) * CMP_STRIDE + CMP_LEN - 1)
    return tok_tabs, cmp_tabs


def _out_kernel(x_ref, g_ref, yl_ref, ys_ref, yr_ref, yn_ref, wgm_ref, wb_ref, wo_ref,
                fg_ref, o_ref, *, final):
    x = x_ref[0]
    h = _rms(x, g_ref[...]).astype(MM)
    w = BRANCH_W
    d = D_MODEL
    m0 = N_BRANCH * w
    merged = jnp.zeros(x.shape, F32)
    for n, y_ref in enumerate((yl_ref, ys_ref, yr_ref, yn_ref)):
        gate = jax.nn.silu(_dot(h, wgm_ref[0, :, n * w:(n + 1) * w]))
        branch = _dot((y_ref[0] * gate).astype(MM), wb_ref[n])
        merged = merged + jax.nn.sigmoid(_dot(h, wgm_ref[0, :, m0 + n * d:m0 + (n + 1) * d])) * branch
    out = x + _dot(merged.astype(MM), wo_ref[...])
    if final:
        out = _rms(out, fg_ref[...])
    o_ref[0] = out


def _out(x, g, ys, wgm, l, wb, wo, fg, final):
    b, s, d = x.shape
    w = BRANCH_W
    tm = TM_PROJ
    row = lambda i, j: (i, j, 0)
    c2 = lambda i, j: (0, 0)
    once = pl.Buffered(1)
    return pl.pallas_call(
        functools.partial(_out_kernel, final=final),
        grid=(b, s // tm),
        in_specs=[pl.BlockSpec((1, tm, d), row),
                  pl.BlockSpec((1, d), c2)]
                 + [pl.BlockSpec((1, tm, w), row)] * N_BRANCH
                 + [pl.BlockSpec((1, d, N_BRANCH * (w + d)), lambda i, j: (l, 0, 0), pipeline_mode=once),
                    pl.BlockSpec((N_BRANCH, w, d), lambda i, j: (0, 0, 0), pipeline_mode=once),
                    pl.BlockSpec((d, d), c2, pipeline_mode=once),
                    pl.BlockSpec((1, d), c2)],
        out_specs=pl.BlockSpec((1, tm, d), row),
        out_shape=jax.ShapeDtypeStruct((b, s, d), F32),
        compiler_params=_cparams("parallel", "arbitrary"),
        name="gate_merge_out",
    )(x, g, *ys, wgm, wb, wo, fg)


def _split_w_in(w_in):
    return w_in[:, :, :PROJ_W].astype(MM), w_in[:, :, GATE_OFF:].astype(MM)


def _layer_tables(p):
    s5 = jax.vmap(_s5_tables)(p["s5_lambda_re"], p["s5_lambda_im"], p["s5_b_re"], p["s5_b_im"],
                              p["s5_c_re"], p["s5_c_im"], p["s5_log_dt"])
    nsa = jax.vmap(_nsa_weights)(p["nsa_pos_k"], p["nsa_pos_v"], p["nsa_ck_w1"], p["nsa_ck_w2"],
                                 p["nsa_cv_w1"], p["nsa_cv_w2"])
    wab = jax.vmap(lambda a, x: jnp.concatenate([_block_diag(a), _block_diag(x)], axis=1))(
        p["lru_wa"], p["lru_wx"]).astype(MM)
    return s5, nsa, wab


def _layer(x, p, l, w_mix, w_gm, tables, ret_tabs, nsa_tabs, final_g):
    s5_t, nsa_w, wab = tables
    g = p["norm_g"][l].reshape(1, D_MODEL)
    proj = _inproj(x, g, w_mix, l)

    bab = jnp.concatenate([p["lru_ba"][l], p["lru_bx"][l]]).reshape(1, 2 * BRANCH_W)
    y_lru = _lru(proj, p["lru_conv_w"][l], p["lru_conv_b"][l].reshape(1, -1), wab, l, bab,
                 p["lru_lambda"][l].reshape(1, -1))

    y_s5 = _s5(proj, *s5_t, l, p["s5_d"][l].reshape(1, -1), p["s5_w_glu"][l].astype(MM),
               p["s5_b_glu"][l].reshape(1, -1))

    y_ret = _retention(proj, ret_tabs)

    y_nsa = _nsa(proj, nsa_w, l, nsa_tabs)

    final = l == DEPTH - 1
    return _out(x, g, (y_lru, y_s5, y_ret, y_nsa), w_gm, l, p["w_branch"][l].astype(MM),
                p["w_out"][l].astype(MM), final_g.reshape(1, D_MODEL), final)


def kernel(x, norm_g, w_in, lru_conv_w, lru_conv_b, lru_wa, lru_ba, lru_wx, lru_bx, lru_lambda,
           s5_lambda_re, s5_lambda_im, s5_b_re, s5_b_im, s5_c_re, s5_c_im, s5_log_dt, s5_d,
           s5_w_glu, s5_b_glu, nsa_pos_k, nsa_pos_v, nsa_ck_w1, nsa_ck_w2, nsa_cv_w1, nsa_cv_w2,
           w_branch, w_out, final_norm_g):
    p = dict(norm_g=norm_g, w_in=w_in, lru_conv_w=lru_conv_w, lru_conv_b=lru_conv_b, lru_wa=lru_wa,
             lru_ba=lru_ba, lru_wx=lru_wx, lru_bx=lru_bx, lru_lambda=lru_lambda,
             s5_lambda_re=s5_lambda_re, s5_lambda_im=s5_lambda_im, s5_b_re=s5_b_re, s5_b_im=s5_b_im,
             s5_c_re=s5_c_re, s5_c_im=s5_c_im, s5_log_dt=s5_log_dt, s5_d=s5_d, s5_w_glu=s5_w_glu,
             s5_b_glu=s5_b_glu, nsa_pos_k=nsa_pos_k, nsa_pos_v=nsa_pos_v, nsa_ck_w1=nsa_ck_w1,
             nsa_ck_w2=nsa_ck_w2, nsa_cv_w1=nsa_cv_w1, nsa_cv_w2=nsa_cv_w2, w_branch=w_branch,
             w_out=w_out)
    s = x.shape[1]
    ret_tabs = _ret_tables(s)
    nsa_tabs = _nsa_tables(s)
    w_mix, w_gm = _split_w_in(w_in)
    tables = _layer_tables(p)
    for l in range(DEPTH):
        x = _layer(x, p, l, w_mix, w_gm, tables, ret_tabs, nsa_tabs, final_norm_g)
    return x
```
